```python
import math
import jax
import jax.numpy as jnp
from jax import lax
import numpy as np

D_MODEL = 1024
BATCH = 8
SEQ = 2048
DEPTH = 2
DEC_BATCH = 128
DEC_SEQ = 1
PAST_LEN = 16384
PAGE_SIZE = 128

GDN_HEADS = 8
GDN_DK = 128
GDN_DV = 128
GDN_KDIM = GDN_HEADS * GDN_DK
GDN_VDIM = GDN_HEADS * GDN_DV
GDN_CONV_CH = 2 * GDN_KDIM + GDN_VDIM
SSM_EXPAND = 2
SSM_DINNER = SSM_EXPAND * D_MODEL
SSM_HEADDIM = 64
SSM_HEADS = SSM_DINNER // SSM_HEADDIM
SSM_GROUPS = 4
SSM_HPG = SSM_HEADS // SSM_GROUPS
SSM_DSTATE = 128
SSM_CONV_CH = SSM_DINNER + 2 * SSM_GROUPS * SSM_DSTATE
CONV_W = 4
CHUNK = 64
N_EXPERTS = 32
TOP_K = 4
D_EXPERT = D_MODEL
SWIGLU_ALPHA = 1.702
SWIGLU_LIMIT = 7.0
EXPERT_BLOCK = 128
DN_ALPHA = (2.0 * DEPTH) ** 0.25
DN_BETA = (8.0 * DEPTH) ** -0.25
LN_EPS = 1e-5
RMS_EPS = 1e-6
L2_EPS = 1e-6
IN_SIZES = (GDN_CONV_CH, GDN_VDIM, GDN_HEADS, GDN_HEADS, SSM_DINNER, SSM_CONV_CH, SSM_HEADS, D_MODEL, D_MODEL)
D_IN_PROJ = GDN_CONV_CH + GDN_VDIM + 2 * GDN_HEADS + SSM_DINNER + SSM_CONV_CH + SSM_HEADS + 2 * D_MODEL

kernel_name = "hybrid_gdn_ssd_moe_deepnorm_step"


def _split_cols(t, sizes):
    idx = [int(v) for v in np.cumsum(sizes)[:-1]]
    return jnp.split(t, idx, axis=-1)


def _layer_norm(x, g, b):
    xf = x.astype(jnp.float32)
    mu = jnp.mean(xf, -1, keepdims=True)
    var = jnp.mean(jnp.square(xf - mu), -1, keepdims=True)
    return ((xf - mu) * lax.rsqrt(var + LN_EPS) * g + b).astype(x.dtype)


def _l2norm(t):
    return t * lax.rsqrt(jnp.sum(t * t, -1, keepdims=True) + L2_EPS)


def _causal_dwconv(u_full, w):
    L = u_full.shape[1] - (CONV_W - 1)
    out = w[0] * u_full[:, 0:L]
    for i in range(1, CONV_W):
        out = out + w[i] * u_full[:, i:i + L]
    return out


def _gdn_chunked(q, k, v, g, beta, S0):
    Bn, L, H, _ = q.shape
    NC = L // CHUNK
    def ch4(t):
        return t.reshape(Bn, NC, CHUNK, H, t.shape[-1]).transpose(0, 3, 1, 2, 4)
    def ch3(t):
        return t.reshape(Bn, NC, CHUNK, H).transpose(0, 3, 1, 2)
    qc, kc, vc = ch4(q), ch4(k), ch4(v)
    gc, bc = ch3(g), ch3(beta)
    G = jnp.cumsum(gc, axis=-1)
    causal = jnp.tril(jnp.ones((CHUNK, CHUNK), dtype=bool))
    strict = jnp.tril(jnp.ones((CHUNK, CHUNK), dtype=bool), -1)
    gamma = jnp.exp(jnp.where(causal, G[..., :, None] - G[..., None, :], -jnp.inf))
    kb = kc * bc[..., None]
    A = jnp.where(strict, jnp.einsum('bhnik,bhnjk->bhnij', kb, kc) * gamma, 0.0)
    eye = jnp.eye(CHUNK, dtype=A.dtype)
    T = lax.linalg.triangular_solve(eye + A, jnp.broadcast_to(eye, A.shape), left_side=True, lower=True, unit_diagonal=True)
    eG = jnp.exp(G)
    u = jnp.einsum('bhnij,bhnjv->bhniv', T, vc * bc[..., None])
    w = jnp.einsum('bhnij,bhnjk->bhnik', T, kb * eG[..., None])
    qk = jnp.einsum('bhnik,bhnjk->bhnij', qc, kc) * gamma
    qg = qc * eG[..., None]
    kdec = kc * jnp.exp(G[..., -1:] - G)[..., None]
    glast = jnp.exp(G[..., -1])

    def step(S, inp):
        u_n, w_n, qg_n, qk_n, kd_n, gl_n = inp
        v_new = u_n - jnp.einsum('bhck,bhkv->bhcv', w_n, S)
        o = jnp.einsum('bhck,bhkv->bhcv', qg_n, S) + jnp.einsum('bhcs,bhsv->bhcv', qk_n, v_new)
        S = S * gl_n[..., None, None] + jnp.einsum('bhck,bhcv->bhkv', kd_n, v_new)
        return S, o

    xs = tuple(jnp.moveaxis(t, 2, 0) for t in (u, w, qg, qk, kdec, glast))
    S_fin, o = lax.scan(step, S0, xs)
    o = o.transpose(1, 0, 3, 2, 4).reshape(Bn, L, H, o.shape[-1])
    return o, S_fin


def _gdn_recurrent(q, k, v, g, beta, S0):
    def step(S, inp):
        q_t, k_t, v_t, g_t, b_t = inp
        S = S * jnp.exp(g_t)[..., None, None]
        v_old = jnp.einsum('bhkv,bhk->bhv', S, k_t)
        S = S + jnp.einsum('bhk,bhv->bhkv', k_t, (v_t - v_old) * b_t[..., None])
        return S, jnp.einsum('bhkv,bhk->bhv', S, q_t)
    xs = tuple(jnp.swapaxes(t, 0, 1) for t in (q, k, v, g, beta))
    S_fin, o = lax.scan(step, S0, xs)
    return jnp.swapaxes(o, 0, 1), S_fin


def _ssd_chunked(x, dt, A, Bm, Cm, h0):
    Bn, L = x.shape[:2]
    NC = L // CHUNK
    G_, R_, P_, N_ = SSM_GROUPS, SSM_HPG, SSM_HEADDIM, SSM_DSTATE
    xdt = (x * dt[..., None]).reshape(Bn, NC, CHUNK, G_, R_, P_)
    dA = (dt * A).reshape(Bn, NC, CHUNK, G_, R_).transpose(0, 1, 3, 4, 2)
    Acs = jnp.cumsum(dA, axis=-1)
    Bc = Bm.reshape(Bn, NC, CHUNK, G_, N_)
    Cc = Cm.reshape(Bn, NC, CHUNK, G_, N_)
    causal = jnp.tril(jnp.ones((CHUNK, CHUNK), dtype=bool))
    Lmat = jnp.exp(jnp.where(causal, Acs[..., :, None] - Acs[..., None, :], -jnp.inf))
    CB = jnp.einsum('bclgn,bcsgn->bcgls', Cc, Bc)
    scores = CB[:, :, :, None] * Lmat
    y_diag = jnp.einsum('bcgrls,bcsgrp->bclgrp', scores, xdt)
    dec_states = jnp.exp(Acs[..., -1:] - Acs).transpose(0, 1, 4, 2, 3)[..., None]
    chunk_states = jnp.einsum('bcsgn,bcsgrp->bcgrpn', Bc, xdt * dec_states)
    chunk_decay = jnp.exp(Acs[..., -1])

    def step(h, inp):
        cs, cd = inp
        return h * cd[..., None, None] + cs, h

    h_fin, h_in = lax.scan(step, h0.reshape(Bn, G_, R_, P_, N_),
                           (jnp.swapaxes(chunk_states, 0, 1), jnp.swapaxes(chunk_decay, 0, 1)))
    h_in = jnp.swapaxes(h_in, 0, 1)
    y_off = jnp.einsum('bclgn,bcgrpn->bclgrp', Cc, h_in) * jnp.exp(Acs).transpose(0, 1, 4, 2, 3)[..., None]
    y = (y_diag + y_off).reshape(Bn, L, SSM_HEADS, P_)
    return y, h_fin.reshape(Bn, SSM_HEADS, P_, N_)


def _ssd_recurrent(x, dt, A, Bm, Cm, h0):
    def step(h, inp):
        x_t, dt_t, B_t, C_t = inp
        Bh = jnp.repeat(B_t, SSM_HPG, axis=1)
        Ch = jnp.repeat(C_t, SSM_HPG, axis=1)
        h = h * jnp.exp(dt_t * A)[..., None, None] + jnp.einsum('bhp,bhn->bhpn', x_t * dt_t[..., None], Bh)
        return h, jnp.einsum('bhpn,bhn->bhp', h, Ch)
    xs = tuple(jnp.swapaxes(t, 0, 1) for t in (x, dt, Bm, Cm))
    h_fin, y = lax.scan(step, h0, xs)
    return jnp.swapaxes(y, 0, 1), h_fin


def _token_mixers(x, gdn_S0, gdn_buf, ssm_h0, ssm_buf, w_in, gdn_conv_w, gdn_a_log, gdn_dt_bias, gdn_norm_w,
                  ssm_conv_w, ssm_conv_b, ssm_a_log, ssm_dt_bias, ssm_d, ssm_norm_w, w_br_gdn, w_br_ssm, w_out, chunked):
    f32 = jnp.float32
    Bn, L, _ = x.shape
    proj = jnp.einsum('bld,de->ble', x, w_in)
    qkv, z_g, b_g, a_g, z_s, xbc, dt_s, gate_a, gate_b = _split_cols(proj, IN_SIZES)

    qkv_full = jnp.concatenate([gdn_buf.astype(qkv.dtype), qkv], axis=1)
    new_gdn_buf = qkv_full[:, L:]
    qkv = jax.nn.silu(_causal_dwconv(qkv_full, gdn_conv_w)).astype(f32)
    q, k, v = _split_cols(qkv, (GDN_KDIM, GDN_KDIM, GDN_VDIM))
    q = _l2norm(q.reshape(Bn, L, GDN_HEADS, GDN_DK)) * (GDN_DK ** -0.5)
    k = _l2norm(k.reshape(Bn, L, GDN_HEADS, GDN_DK))
    v = v.reshape(Bn, L, GDN_HEADS, GDN_DV)
    beta = jax.nn.sigmoid(b_g.astype(f32))
    g = -jnp.exp(gdn_a_log.astype(f32)) * jax.nn.softplus(a_g.astype(f32) + gdn_dt_bias.astype(f32))
    S0 = gdn_S0.astype(f32)
    if chunked:
        o, S = _gdn_chunked(q, k, v, g, beta, S0)
    else:
        o, S = _gdn_recurrent(q, k, v, g, beta, S0)
    o = (o * lax.rsqrt(jnp.mean(o * o, -1, keepdims=True) + RMS_EPS) * gdn_norm_w.astype(f32)
         * jax.nn.silu(z_g.astype(f32).reshape(Bn, L, GDN_HEADS, GDN_DV)))
    o = o.reshape(Bn, L, GDN_VDIM).astype(x.dtype)

    xbc_full = jnp.concatenate([ssm_buf.astype(xbc.dtype), xbc], axis=1)
    new_ssm_buf = xbc_full[:, L:]
    xbc = jax.nn.silu(_causal_dwconv(xbc_full, ssm_conv_w) + ssm_conv_b).astype(f32)
    xs, Bm, Cm = _split_cols(xbc, (SSM_DINNER, SSM_GROUPS * SSM_DSTATE, SSM_GROUPS * SSM_DSTATE))
    xs = xs.reshape(Bn, L, SSM_HEADS, SSM_HEADDIM)
    Bm = Bm.reshape(Bn, L, SSM_GROUPS, SSM_DSTATE)
    Cm = Cm.reshape(Bn, L, SSM_GROUPS, SSM_DSTATE)
    dt = jax.nn.softplus(dt_s.astype(f32) + ssm_dt_bias.astype(f32))
    A = -jnp.exp(ssm_a_log.astype(f32))
    h0 = ssm_h0.astype(f32)
    if chunked:
        y, h = _ssd_chunked(xs, dt, A, Bm, Cm, h0)
    else:
        y, h = _ssd_recurrent(xs, dt, A, Bm, Cm, h0)
    y = y + ssm_d.astype(f32)[:, None] * xs
    yz = (y * jax.nn.silu(z_s.astype(f32).reshape(Bn, L, SSM_HEADS, SSM_HEADDIM))).reshape(Bn, L, SSM_GROUPS, SSM_DINNER // SSM_GROUPS)
    yz = yz * lax.rsqrt(jnp.mean(yz * yz, -1, keepdims=True) + RMS_EPS) * ssm_norm_w.astype(f32).reshape(SSM_GROUPS, -1)
    yz = yz.reshape(Bn, L, SSM_DINNER).astype(x.dtype)

    merged = (jax.nn.sigmoid(gate_a) * jnp.einsum('blv,vd->bld', o, w_br_gdn)
              + jax.nn.sigmoid(gate_b) * jnp.einsum('ble,ed->bld', yz, w_br_ssm))
    out = jnp.einsum('bld,de->ble', merged, w_out)
    return out, S.astype(x.dtype), new_gdn_buf, h.astype(x.dtype), new_ssm_buf


def _moe(x, router_w, router_b, w_gate, b_gate, w_up, b_up, w_down, b_down):
    Bn, L, D = x.shape
    t = x.reshape(-1, D)
    N = t.shape[0]
    logits = (jnp.einsum('nd,de->ne', t, router_w) + router_b).astype(jnp.float32)
    top_v, top_i = lax.top_k(logits, TOP_K)
    gates = jax.nn.softmax(top_v, axis=-1)
    M = N * TOP_K
    flat_e = top_i.reshape(-1).astype(jnp.int32)
    flat_w = gates.reshape(-1)
    flat_tok = jnp.arange(M, dtype=jnp.int32) // TOP_K
    order = jnp.argsort(flat_e)
    se = flat_e[order]
    counts = jnp.zeros((N_EXPERTS,), jnp.int32).at[flat_e].add(1)
    starts = jnp.cumsum(counts) - counts
    padded = (counts + EXPERT_BLOCK - 1) // EXPERT_BLOCK * EXPERT_BLOCK
    pad_ends = jnp.cumsum(padded)
    pad_starts = pad_ends - padded
    dest = pad_starts[se] + (jnp.arange(M, dtype=jnp.int32) - starts[se])
    n_blocks = (M + EXPERT_BLOCK - 1) // EXPERT_BLOCK + N_EXPERTS
    R = n_blocks * EXPERT_BLOCK
    row_tok = jnp.full((R,), N, jnp.int32).at[dest].set(flat_tok[order])
    row_w = jnp.zeros((R,), jnp.float32).at[dest].set(flat_w[order])
    block_e = jnp.minimum(jnp.searchsorted(pad_ends, jnp.arange(n_blocks, dtype=jnp.int32) * EXPERT_BLOCK, side='right'),
                          N_EXPERTS - 1).astype(jnp.int32)
    t_pad = jnp.concatenate([t, jnp.zeros((1, D), t.dtype)], axis=0)
    xb = t_pad[row_tok].reshape(n_blocks, EXPERT_BLOCK, D)

    def expert_block(args):
        xblk, e = args
        gt = xblk @ w_gate[e] + b_gate[e]
        up = xblk @ w_up[e] + b_up[e]
        gt = jnp.minimum(gt, SWIGLU_LIMIT)
        up = jnp.clip(up, -SWIGLU_LIMIT, SWIGLU_LIMIT)
        h = (up + 1.0) * (gt * jax.nn.sigmoid(SWIGLU_ALPHA * gt))
        return h @ w_down[e] + b_down[e]

    yb = lax.map(expert_block, (xb, block_e)).reshape(R, D)
    y = jnp.zeros((N + 1, D), jnp.float32).at[row_tok].add(yb * row_w[:, None])
    return y[:N].reshape(Bn, L, D).astype(x.dtype)


def setup_inputs(seed: int = 0) -> dict:
    key = jax.random.key(seed)
    ks = jax.random.split(key, 40)
    f32 = jnp.float32
    nrm = lambda k, s, sc: jax.random.normal(k, s, f32) * sc
    def dt_bias_init(k, h):
        dt = jnp.exp(jax.random.uniform(k, (DEPTH, h), f32, math.log(1e-3), math.log(1e-1)))
        return dt + jnp.log(-jnp.expm1(-dt))
    return {
        "x_prompt": nrm(ks[0], (BATCH, SEQ, D_MODEL), 1.0),
        "x_sample": nrm(ks[1], (DEC_BATCH, DEC_SEQ, D_MODEL), 1.0),
        "state_gdn": nrm(ks[2], (DEPTH, DEC_BATCH, GDN_HEADS, GDN_DK, GDN_DV), 0.5),
        "state_gdn_conv": nrm(ks[3], (DEPTH, DEC_BATCH, CONV_W - 1, GDN_CONV_CH), 1.0),
        "state_ssm": nrm(ks[4], (DEPTH, DEC_BATCH, SSM_HEADS, SSM_HEADDIM, SSM_DSTATE), 0.5),
        "state_ssm_conv": nrm(ks[5], (DEPTH, DEC_BATCH, CONV_W - 1, SSM_CONV_CH), 1.0),
        "w_in": nrm(ks[6], (DEPTH, D_MODEL, D_IN_PROJ), D_MODEL ** -0.5),
        "gdn_conv_w": nrm(ks[7], (DEPTH, CONV_W, GDN_CONV_CH), CONV_W ** -0.5),
        "gdn_a_log": jnp.log(jax.random.uniform(ks[8], (DEPTH, GDN_HEADS), f32, 1.0, 16.0)),
        "gdn_dt_bias": dt_bias_init(ks[9], GDN_HEADS),
        "gdn_norm_w": 1.0 + nrm(ks[10], (DEPTH, GDN_DV), 0.01),
        "ssm_conv_w": nrm(ks[11], (DEPTH, CONV_W, SSM_CONV_CH), CONV_W ** -0.5),
        "ssm_conv_b": nrm(ks[12], (DEPTH, SSM_CONV_CH), 0.01),
        "ssm_a_log": jnp.log(jax.random.uniform(ks[13], (DEPTH, SSM_HEADS), f32, 1.0, 16.0)),
        "ssm_dt_bias": dt_bias_init(ks[14], SSM_HEADS),
        "ssm_d": 1.0 + nrm(ks[15], (DEPTH, SSM_HEADS), 0.01),
        "ssm_norm_w": 1.0 + nrm(ks[16], (DEPTH, SSM_DINNER), 0.01),
        "w_br_gdn": nrm(ks[17], (DEPTH, GDN_VDIM, D_MODEL), GDN_VDIM ** -0.5),
        "w_br_ssm": nrm(ks[18], (DEPTH, SSM_DINNER, D_MODEL), SSM_DINNER ** -0.5),
        "w_out": nrm(ks[19], (DEPTH, D_MODEL, D_MODEL), DN_BETA * D_MODEL ** -0.5),
        "ln1_g": 1.0 + nrm(ks[20], (DEPTH, D_MODEL), 0.01),
        "ln1_b": nrm(ks[21], (DEPTH, D_MODEL), 0.01),
        "router_w": nrm(ks[22], (DEPTH, D_MODEL, N_EXPERTS), D_MODEL ** -0.5),
        "router_b": nrm(ks[23], (DEPTH, N_EXPERTS), 0.01),
        "exp_w_gate": nrm(ks[24], (DEPTH, N_EXPERTS, D_MODEL, D_EXPERT), D_MODEL ** -0.5),
        "exp_b_gate": nrm(ks[25], (DEPTH, N_EXPERTS, D_EXPERT), 0.01),
        "exp_w_up": nrm(ks[26], (DEPTH, N_EXPERTS, D_MODEL, D_EXPERT), D_MODEL ** -0.5),
        "exp_b_up": nrm(ks[27], (DEPTH, N_EXPERTS, D_EXPERT), 0.01),
        "exp_w_down": nrm(ks[28], (DEPTH, N_EXPERTS, D_EXPERT, D_MODEL), DN_BETA * D_EXPERT ** -0.5),
        "exp_b_down": nrm(ks[29], (DEPTH, N_EXPERTS, D_MODEL), 0.01),
        "ln2_g": 1.0 + nrm(ks[30], (DEPTH, D_MODEL), 0.01),
        "ln2_b": nrm(ks[31], (DEPTH, D_MODEL), 0.01),
    }


def reference(x_prompt, x_sample, state_gdn, state_gdn_conv, state_ssm, state_ssm_conv, w_in, gdn_conv_w,
              gdn_a_log, gdn_dt_bias, gdn_norm_w, ssm_conv_w, ssm_conv_b, ssm_a_log, ssm_dt_bias, ssm_d,
              ssm_norm_w, w_br_gdn, w_br_ssm, w_out, ln1_g, ln1_b, router_w, router_b, exp_w_gate, exp_b_gate,
              exp_w_up, exp_b_up, exp_w_down, exp_b_down, ln2_g, ln2_b):
    yp, ys = x_prompt, x_sample
    dt_ = x_prompt.dtype
    gdn_p, gdn_s, gconv_p, gconv_s, ssm_p, ssm_s, sconv_p, sconv_s = [], [], [], [], [], [], [], []
    for l in range(DEPTH):
        mix_w = (w_in[l], gdn_conv_w[l], gdn_a_log[l], gdn_dt_bias[l], gdn_norm_w[l], ssm_conv_w[l], ssm_conv_b[l],
                 ssm_a_log[l], ssm_dt_bias[l], ssm_d[l], ssm_norm_w[l], w_br_gdn[l], w_br_ssm[l], w_out[l])
        moe_w = (router_w[l], router_b[l], exp_w_gate[l], exp_b_gate[l], exp_w_up[l], exp_b_up[l], exp_w_down[l], exp_b_down[l])
        mp, Sp, gbp, hp, sbp = _token_mixers(
            yp, jnp.zeros((BATCH, GDN_HEADS, GDN_DK, GDN_DV), dt_), jnp.zeros((BATCH, CONV_W - 1, GDN_CONV_CH), dt_),
            jnp.zeros((BATCH, SSM_HEADS, SSM_HEADDIM, SSM_DSTATE), dt_), jnp.zeros((BATCH, CONV_W - 1, SSM_CONV_CH), dt_),
            *mix_w, chunked=True)
        ms, Ss, gbs, hs, sbs = _token_mixers(
            ys, state_gdn[l], state_gdn_conv[l], state_ssm[l], state_ssm_conv[l], *mix_w, chunked=False)
        yp = _layer_norm(DN_ALPHA * yp + mp, ln1_g[l], ln1_b[l])
        ys = _layer_norm(DN_ALPHA * ys + ms, ln1_g[l], ln1_b[l])
        yp = _layer_norm(DN_ALPHA * yp + _moe(yp, *moe_w), ln2_g[l], ln2_b[l])
        ys = _layer_norm(DN_ALPHA * ys + _moe(ys, *moe_w), ln2_g[l], ln2_b[l])
        gdn_p.append(Sp); gdn_s.append(Ss); gconv_p.append(gbp); gconv_s.append(gbs)
        ssm_p.append(hp); ssm_s.append(hs); sconv_p.append(sbp); sconv_s.append(sbs)
    return (yp, ys, jnp.stack(gdn_p), jnp.stack(gdn_s), jnp.stack(gconv_p), jnp.stack(gconv_s),
            jnp.stack(ssm_p), jnp.stack(ssm_s), jnp.stack(sconv_p), jnp.stack(sconv_s))
```

```python
import functools

import jax
import jax.numpy as jnp
from jax import lax
from jax.experimental import pallas as pl
from jax.experimental.pallas import tpu as pltpu

F32 = jnp.float32
BF16 = jnp.bfloat16
HI = lax.Precision.HIGHEST

D_MODEL = 1024
GDN_HEADS = 8
GDN_DK = 128
GDN_VDIM = 1024
GDN_CONV_CH = 3072
SSM_HEADS = 32
SSM_HEADDIM = 64
SSM_GROUPS = 4
SSM_DINNER = 2048
SSM_DSTATE = 128
SSM_CONV_CH = 3072
CONV_W = 4
N_EXPERTS = 32
TOP_K = 4
SWIGLU_ALPHA = 1.702
SWIGLU_LIMIT = 7.0
LN_EPS = 1e-5
RMS_EPS = 1e-6
L2_EPS = 1e-6
NEG_BIG = -1e30

W_BIG = 11264
SMALL_W = 128
COL_B, COL_A, COL_DT = 0, 8, 16

LANE = 128
SUBLANE = 8
CHUNK = 128
GDN_HB = 4
MOE_BLK = 256
TOK_TILE = 128
VMEM_LIMIT = 56 * 1024 * 1024


def _pick(n, cands):
    for c in cands:
        if n % c == 0:
            return c
    raise ValueError(f"no tile for {n}")


def _bf(x):
    return x.astype(BF16)


def _dot(a, b, prec=None):
    return jnp.dot(a, b, preferred_element_type=F32, precision=prec)


def _dot_nt(a, b):
    return lax.dot_general(a, b, (((1,), (1,)), ((), ())), preferred_element_type=F32)


def _dot_tn(a, b):
    return lax.dot_general(a, b, (((0,), (0,)), ((), ())), preferred_element_type=F32)


def _sigmoid(x):
    return jax.nn.sigmoid(x)


def _silu(x):
    return x * jax.nn.sigmoid(x)


def _softplus(x):
    return jnp.maximum(x, 0.0) + jnp.log(1.0 + jnp.exp(-jnp.abs(x)))


def _layer_norm(x, g, b):
    mu = jnp.mean(x, axis=-1, keepdims=True)
    xc = x - mu
    var = jnp.mean(xc * xc, axis=-1, keepdims=True)
    return xc * lax.rsqrt(var + LN_EPS) * g + b


def _small_act(raw, pv):
    col = lax.broadcasted_iota(jnp.int32, raw.shape, 1)
    sp = _softplus(raw + pv[0:1, :])
    act = jnp.where(col < COL_A, _sigmoid(raw), sp)
    gda = sp * (-jnp.exp(pv[1:2, :]))
    return act, gda


def _mm_kernel(x_ref, w_ref, o_ref):
    o_ref[...] = _dot(x_ref[...], w_ref[...])


def _matmul(x, w, tm, tn):
    m, k = x.shape
    n = w.shape[1]
    return pl.pallas_call(
        _mm_kernel,
        out_shape=jax.ShapeDtypeStruct((m, n), F32),
        grid=(n // tn, m // tm),
        in_specs=[pl.BlockSpec((tm, k), lambda j, i: (i, 0)),
                  pl.BlockSpec((k, tn), lambda j, i: (0, j))],
        out_specs=pl.BlockSpec((tm, tn), lambda j, i: (i, j)),
        compiler_params=pltpu.CompilerParams(
            dimension_semantics=("parallel", "parallel"), vmem_limit_bytes=VMEM_LIMIT),
        name="in_proj",
    )(x, w)


def _tri_inv(a, ii, jj, c):
    eye = (ii == jj).astype(F32)
    t = eye - jnp.where((ii >> 1) == (jj >> 1), a, 0.0)
    s = 1
    while (2 << s) <= c:
        same_outer = (ii >> (s + 1)) == (jj >> (s + 1))
        same_inner = (ii >> s) == (jj >> s)
        e = jnp.where(same_outer & jnp.logical_not(same_inner), a, 0.0)
        te = _dot(_bf(t), _bf(e))
        t = t - _dot(_bf(te), _bf(t))
        s += 1
    return t


def _gdn_kernel(q_ref, k_ref, v_ref, zg_ref, small_ref, cwq_ref, cwk_ref, cwv_ref, pv_ref, nw_ref,
                o_ref, sfin_ref, xf, tail, s_scr):
    c_len = q_ref.shape[0]
    hg = pl.program_id(1)
    c = pl.program_id(2)
    nc = pl.num_programs(2)

    @pl.when(c == 0)
    def _():
        tail[...] = jnp.zeros_like(tail)
        s_scr[...] = jnp.zeros_like(s_scr)

    for p, r in enumerate((q_ref, k_ref, v_ref)):
        xf[p, 0:SUBLANE, :] = tail[p]
        xf[p, SUBLANE:SUBLANE + c_len, :] = r[...]
        tail[p] = r[c_len - SUBLANE:c_len, :]
    cws = (cwq_ref, cwk_ref, cwv_ref)

    def conv(p, hs):
        acc = None
        for i in range(CONV_W):
            term = cws[p][i:i + 1, hs] * xf[p, pl.ds(SUBLANE - (CONV_W - 1) + i, c_len), hs]
            acc = term if acc is None else acc + term
        return _silu(acc)

    act, gda = _small_act(small_ref[...], pv_ref[...])
    ii = lax.broadcasted_iota(jnp.int32, (c_len, c_len), 0)
    jj = lax.broadcasted_iota(jnp.int32, (c_len, c_len), 1)
    incl = ii >= jj
    strict = ii > jj
    gcum = _dot(incl.astype(F32), gda, HI)
    gcum_t = gcum.T
    col = lax.broadcasted_iota(jnp.int32, (c_len, SMALL_W), 1)
    row_t = lax.broadcasted_iota(jnp.int32, (SMALL_W, c_len), 0)
    nw = nw_ref[...]

    for hh in range(GDN_HB):
        hs = slice(hh * LANE, (hh + 1) * LANE)
        head = hg * GDN_HB + hh
        beta = jnp.sum(jnp.where(col == COL_B + head, act, 0.0), axis=-1, keepdims=True)
        gc = jnp.sum(jnp.where(col == COL_A + head, gcum, 0.0), axis=-1, keepdims=True)
        gr = jnp.sum(jnp.where(row_t == COL_A + head, gcum_t, 0.0), axis=0, keepdims=True)
        q = conv(0, hs)
        k = conv(1, hs)
        v = conv(2, hs)
        q = q * lax.rsqrt(jnp.sum(q * q, axis=-1, keepdims=True) + L2_EPS) * (GDN_DK ** -0.5)
        k = k * lax.rsqrt(jnp.sum(k * k, axis=-1, keepdims=True) + L2_EPS)
        gam = jnp.exp(jnp.where(incl, gc - gr, NEG_BIG))
        kb = k * beta
        kbf = _bf(k)
        a = jnp.where(strict, _dot_nt(_bf(kb), kbf) * gam, 0.0)
        t = _tri_inv(a, ii, jj, c_len)
        eg = jnp.exp(gc)
        uw = _dot(_bf(t), _bf(jnp.concatenate([v * beta, kb * eg], axis=1)))
        u = uw[:, :LANE]
        w = uw[:, LANE:]
        qk = _dot_nt(_bf(q), kbf) * gam
        glast = gc[c_len - 1:c_len, :]
        kdec = k * jnp.exp(glast - gc)
        s_old = s_scr[hh]
        sb = _bf(s_old)
        v_new = u - _dot(_bf(w), sb)
        o = _dot(_bf(q * eg), sb) + _dot(_bf(qk), _bf(v_new))
        s_scr[hh] = s_old * jnp.exp(glast) + _dot_tn(_bf(kdec), _bf(v_new))
        o = o * lax.rsqrt(jnp.mean(o * o, axis=-1, keepdims=True) + RMS_EPS) * nw * _silu(zg_ref[:, hs])
        o_ref[:, hs] = _bf(o)

    @pl.when(c == nc - 1)
    def _():
        sfin_ref[0] = s_scr[...]


def _gdn_prompt(proj, small, conv_w, pv, norm_w, batch, seq):
    nc = seq // CHUNK
    hbw = GDN_HB * LANE
    ngrp = GDN_HEADS // GDN_HB
    kq, kk, kv, kz = 0, GDN_VDIM // hbw, 2 * GDN_VDIM // hbw, 3 * GDN_VDIM // hbw

    def row(b, hg, c):
        return b * nc + c

    def pspec(off):
        return pl.BlockSpec((CHUNK, hbw), lambda b, hg, c: (row(b, hg, c), off + hg))

    def wspec(off):
        return pl.BlockSpec((CONV_W, hbw), lambda b, hg, c: (0, off + hg))

    return pl.pallas_call(
        _gdn_kernel,
        out_shape=(jax.ShapeDtypeStruct((batch * seq, GDN_VDIM), BF16),
                   jax.ShapeDtypeStruct((batch, GDN_HEADS, GDN_DK, LANE), F32)),
        grid=(batch, ngrp, nc),
        in_specs=[pspec(kq), pspec(kk), pspec(kv), pspec(kz),
                  pl.BlockSpec((CHUNK, SMALL_W), lambda b, hg, c: (row(b, hg, c), 0)),
                  wspec(kq), wspec(kk), wspec(kv),
                  pl.BlockSpec((SUBLANE, SMALL_W), lambda b, hg, c: (0, 0)),
                  pl.BlockSpec((1, LANE), lambda b, hg, c: (0, 0))],
        out_specs=(pl.BlockSpec((CHUNK, hbw), lambda b, hg, c: (row(b, hg, c), hg)),
                   pl.BlockSpec((1, GDN_HB, GDN_DK, LANE), lambda b, hg, c: (b, hg, 0, 0))),
        scratch_shapes=[pltpu.VMEM((3, CHUNK + SUBLANE, hbw), F32),
                        pltpu.VMEM((3, SUBLANE, hbw), F32),
                        pltpu.VMEM((GDN_HB, GDN_DK, LANE), F32)],
        compiler_params=pltpu.CompilerParams(
            dimension_semantics=("parallel", "parallel", "arbitrary"), vmem_limit_bytes=VMEM_LIMIT),
        name="gdn_prompt",
    )(proj, proj, proj, proj, small, conv_w, conv_w, conv_w, pv, norm_w)


def _ssd_kernel(xs_ref, b_ref, c_ref, zs_ref, small_ref, cwx_ref, cwb_ref, cwc_ref, cbx_ref, cbb_ref,
                cbc_ref, pv_ref, ex_ref, dexp_ref, nw_ref, yz_ref, hfin_ref, xf, tail, ht, ydiag):
    c_len = xs_ref.shape[0]
    gw = xs_ref.shape[1]
    hpg = gw // SSM_HEADDIM
    g = pl.program_id(1)
    c = pl.program_id(2)
    nc = pl.num_programs(2)

    @pl.when(c == 0)
    def _():
        tail[...] = jnp.zeros_like(tail)
        ht[...] = jnp.zeros_like(ht)

    parts = ((xs_ref, cwx_ref, cbx_ref, 0, gw), (b_ref, cwb_ref, cbb_ref, gw, SSM_DSTATE),
             (c_ref, cwc_ref, cbc_ref, gw + SSM_DSTATE, SSM_DSTATE))
    convs = []
    for r, cw, cb, off, wd in parts:
        sl = slice(off, off + wd)
        xf[0:SUBLANE, sl] = tail[:, sl]
        xf[SUBLANE:SUBLANE + c_len, sl] = r[...]
        tail[:, sl] = r[c_len - SUBLANE:c_len, :]
        acc = cb[...]
        for i in range(CONV_W):
            acc = acc + cw[i:i + 1, :] * xf[pl.ds(SUBLANE - (CONV_W - 1) + i, c_len), sl]
        convs.append(_silu(acc))
    xs, bm, cm = convs

    act, gda = _small_act(small_ref[...], pv_ref[...])
    ii = lax.broadcasted_iota(jnp.int32, (c_len, c_len), 0)
    jj = lax.broadcasted_iota(jnp.int32, (c_len, c_len), 1)
    incl = ii >= jj
    acs = _dot(incl.astype(F32), gda, HI)
    acs_t = acs.T
    ex = ex_ref[...]
    dt_x = _dot(act, ex, HI)
    acs_x = _dot(acs, ex, HI)
    last = acs_x[c_len - 1:c_len, :]
    xdt = xs * dt_x
    cb_m = _dot_nt(_bf(cm), _bf(bm))
    h_old = ht[...]
    y_off = _dot(_bf(cm), _bf(h_old)) * jnp.exp(acs_x)
    col = lax.broadcasted_iota(jnp.int32, (c_len, SMALL_W), 1)
    row_t = lax.broadcasted_iota(jnp.int32, (SMALL_W, c_len), 0)
    lane = lax.broadcasted_iota(jnp.int32, (c_len, LANE), 1)
    for pr in range(hpg // 2):
        ps = slice(pr * LANE, (pr + 1) * LANE)
        xpair = xdt[:, ps]
        acc = None
        for half in range(2):
            head = g * hpg + pr * 2 + half
            ac = jnp.sum(jnp.where(col == COL_DT + head, acs, 0.0), axis=-1, keepdims=True)
            ar = jnp.sum(jnp.where(row_t == COL_DT + head, acs_t, 0.0), axis=0, keepdims=True)
            sc = cb_m * jnp.exp(jnp.where(incl, ac - ar, NEG_BIG))
            in_half = (lane >= half * SSM_HEADDIM) & (lane < (half + 1) * SSM_HEADDIM)
            term = _dot(_bf(sc), _bf(jnp.where(in_half, xpair, 0.0)))
            acc = term if acc is None else acc + term
        ydiag[:, ps] = acc
    ht[...] = h_old * jnp.exp(last) + _dot_tn(_bf(bm), _bf(xdt * jnp.exp(last - acs_x)))
    y = ydiag[...] + y_off + dexp_ref[...] * xs
    yz = y * _silu(zs_ref[...])
    yz = yz * lax.rsqrt(jnp.mean(yz * yz, axis=-1, keepdims=True) + RMS_EPS) * nw_ref[...]
    yz_ref[...] = _bf(yz)

    @pl.when(c == nc - 1)
    def _():
        hfin_ref[0] = ht[...].T


def _ssd_prompt(proj, small, conv_w, conv_b, pv, expand, d_exp, norm_w, batch, seq):
    nc = seq // CHUNK
    gw = SSM_DINNER // SSM_GROUPS
    x_off = (GDN_CONV_CH + GDN_VDIM + SSM_DINNER)
    z_off = GDN_CONV_CH + GDN_VDIM
    kx = x_off // gw
    kb = (x_off + SSM_DINNER) // SSM_DSTATE
    kc = kb + SSM_GROUPS
    kz = z_off // gw
    wb = SSM_DINNER // SSM_DSTATE
    wc = wb + SSM_GROUPS

    def row(b, g, c):
        return b * nc + c

    in_specs = [
        pl.BlockSpec((CHUNK, gw), lambda b, g, c: (row(b, g, c), kx + g)),
        pl.BlockSpec((CHUNK, SSM_DSTATE), lambda b, g, c: (row(b, g, c), kb + g)),
        pl.BlockSpec((CHUNK, SSM_DSTATE), lambda b, g, c: (row(b, g, c), kc + g)),
        pl.BlockSpec((CHUNK, gw), lambda b, g, c: (row(b, g, c), kz + g)),
        pl.BlockSpec((CHUNK, SMALL_W), lambda b, g, c: (row(b, g, c), 0)),
        pl.BlockSpec((CONV_W, gw), lambda b, g, c: (0, g)),
        pl.BlockSpec((CONV_W, SSM_DSTATE), lambda b, g, c: (0, wb + g)),
        pl.BlockSpec((CONV_W, SSM_DSTATE), lambda b, g, c: (0, wc + g)),
        pl.BlockSpec((1, gw), lambda b, g, c: (0, g)),
        pl.BlockSpec((1, SSM_DSTATE), lambda b, g, c: (0, wb + g)),
        pl.BlockSpec((1, SSM_DSTATE), lambda b, g, c: (0, wc + g)),
        pl.BlockSpec((SUBLANE, SMALL_W), lambda b, g, c: (0, 0)),
        pl.BlockSpec((None, SMALL_W, gw), lambda b, g, c: (g, 0, 0)),
        pl.BlockSpec((1, gw), lambda b, g, c: (0, g)),
        pl.BlockSpec((1, gw), lambda b, g, c: (0, g)),
    ]
    return pl.pallas_call(
        _ssd_kernel,
        out_shape=(jax.ShapeDtypeStruct((batch * seq, SSM_DINNER), BF16),
                   jax.ShapeDtypeStruct((batch, SSM_DINNER, SSM_DSTATE), F32)),
        grid=(batch, SSM_GROUPS, nc),
        in_specs=in_specs,
        out_specs=(pl.BlockSpec((CHUNK, gw), lambda b, g, c: (row(b, g, c), g)),
                   pl.BlockSpec((1, gw, SSM_DSTATE), lambda b, g, c: (b, g, 0))),
        scratch_shapes=[pltpu.VMEM((CHUNK + SUBLANE, gw + 2 * SSM_DSTATE), F32),
                        pltpu.VMEM((SUBLANE, gw + 2 * SSM_DSTATE), F32),
                        pltpu.VMEM((SSM_DSTATE, gw), F32),
                        pltpu.VMEM((CHUNK, gw), F32)],
        compiler_params=pltpu.CompilerParams(
            dimension_semantics=("parallel", "parallel", "arbitrary"), vmem_limit_bytes=VMEM_LIMIT),
        name="ssd_prompt",
    )(proj, proj, proj, proj, small, conv_w, conv_w, conv_w, conv_b, conv_b, conv_b, pv, expand,
      d_exp, norm_w)


def _sample_kernel(proj_ref, small_ref, gcs_ref, scs_ref, s0_ref, h0_ref, gcw_ref, scw_ref, scb_ref,
                   pv_ref, gnw_ref, ex_ref, dexp_ref, snw_ref, og_ref, yz_ref, s_ref, h_ref, stk, stk2):
    gw = SSM_DINNER // SSM_GROUPS
    z_off = GDN_CONV_CH + GDN_VDIM
    x_off = z_off + SSM_DINNER

    def conv1(state_ref, w_ref, new_row):
        acc = w_ref[CONV_W - 1:CONV_W, :] * new_row
        for i in range(CONV_W - 1):
            acc = acc + w_ref[i:i + 1, :] * state_ref[i:i + 1, :]
        return acc

    act, gda = _small_act(small_ref[...], pv_ref[...])

    qkv = _silu(conv1(gcs_ref, gcw_ref, proj_ref[:, 0:GDN_CONV_CH]))
    stk[...] = jnp.zeros_like(stk)
    qs, vs = [], []
    for h in range(GDN_HEADS):
        q = qkv[:, h * LANE:(h + 1) * LANE]
        k = qkv[:, GDN_VDIM + h * LANE:GDN_VDIM + (h + 1) * LANE]
        q = q * lax.rsqrt(jnp.sum(q * q, axis=-1, keepdims=True) + L2_EPS) * (GDN_DK ** -0.5)
        k = k * lax.rsqrt(jnp.sum(k * k, axis=-1, keepdims=True) + L2_EPS)
        stk[h:h + 1, :] = k
        stk[GDN_HEADS + h:GDN_HEADS + h + 1, :] = q
        vs.append(qkv[:, 2 * GDN_VDIM + h * LANE:2 * GDN_VDIM + (h + 1) * LANE])
    cols = stk[...].T
    gnw = gnw_ref[...]
    for h in range(GDN_HEADS):
        kc = cols[:, h:h + 1]
        qc = cols[:, GDN_HEADS + h:GDN_HEADS + h + 1]
        beta = act[:, COL_B + h:COL_B + h + 1]
        gh = gda[:, COL_A + h:COL_A + h + 1]
        sd = s0_ref[h] * jnp.exp(gh)
        v_old = jnp.sum(sd * kc, axis=0, keepdims=True)
        delta = (vs[h] - v_old) * beta
        s_new = sd + kc * delta
        s_ref[h] = s_new
        o = jnp.sum(s_new * qc, axis=0, keepdims=True)
        zg = proj_ref[:, GDN_CONV_CH + h * LANE:GDN_CONV_CH + (h + 1) * LANE]
        o = o * lax.rsqrt(jnp.mean(o * o, axis=-1, keepdims=True) + RMS_EPS) * gnw * _silu(zg)
        og_ref[:, h * LANE:(h + 1) * LANE] = _bf(o)

    xbc = _silu(conv1(scs_ref, scw_ref, proj_ref[:, x_off:x_off + SSM_CONV_CH]) + scb_ref[...])
    row16 = lax.broadcasted_iota(jnp.int32, (2 * SUBLANE, SMALL_W), 0)
    act16 = jnp.where(row16 == 0, jnp.broadcast_to(act, (2 * SUBLANE, SMALL_W)), 0.0)
    gda16 = jnp.where(row16 == 0, jnp.broadcast_to(gda, (2 * SUBLANE, SMALL_W)), 0.0)
    dexp = dexp_ref[...]
    stk2[...] = jnp.zeros_like(stk2)
    for g in range(SSM_GROUPS):
        gs = slice(g * gw, (g + 1) * gw)
        ex = ex_ref[g]
        dt_x = _dot(act16, ex, HI)[0:1, :]
        da_x = _dot(gda16, ex, HI)[0:1, :]
        xs = xbc[:, gs]
        bm = xbc[:, SSM_DINNER + g * SSM_DSTATE:SSM_DINNER + (g + 1) * SSM_DSTATE]
        cm = xbc[:, SSM_DINNER + SSM_GROUPS * SSM_DSTATE + g * SSM_DSTATE:
                 SSM_DINNER + SSM_GROUPS * SSM_DSTATE + (g + 1) * SSM_DSTATE]
        stk2[0:1, :] = jnp.exp(da_x)
        stk2[1:2, :] = xs * dt_x
        cols2 = stk2[...].T
        h_new = h0_ref[gs, :] * cols2[:, 0:1] + cols2[:, 1:2] * bm
        h_ref[gs, :] = h_new
        cm16 = jnp.broadcast_to(cm, (2 * SUBLANE, SSM_DSTATE))
        y = _dot_nt(_bf(cm16), _bf(h_new))[0:1, :] + dexp[:, gs] * xs
        yz = y * _silu(proj_ref[:, z_off + g * gw:z_off + (g + 1) * gw])
        yz = yz * lax.rsqrt(jnp.mean(yz * yz, axis=-1, keepdims=True) + RMS_EPS) * snw_ref[:, gs]
        yz_ref[:, gs] = _bf(yz)


def _sample_mix(proj_s, small_s, gconv_state, sconv_state, s0, h0, gconv_w, sconv_w, sconv_b, pv,
                gnorm_w, expand, d_exp, snorm_w):
    n = proj_s.shape[0]
    gw = SSM_DINNER // SSM_GROUPS
    proj3 = proj_s.reshape(n, 1, W_BIG)
    small3 = small_s.reshape(n, 1, SMALL_W)
    h0f = h0.reshape(n, SSM_DINNER, SSM_DSTATE)

    def full(shape):
        nd = len(shape)
        return pl.BlockSpec(shape, lambda i: (0,) * nd)

    def per_seq(shape):
        nd = len(shape)
        return pl.BlockSpec((None,) + shape, lambda i: (i,) + (0,) * nd)

    og, yz, s_new, h_new = pl.pallas_call(
        _sample_kernel,
        out_shape=(jax.ShapeDtypeStruct((n, 1, GDN_VDIM), BF16),
                   jax.ShapeDtypeStruct((n, 1, SSM_DINNER), BF16),
                   jax.ShapeDtypeStruct(s0.shape, F32),
                   jax.ShapeDtypeStruct(h0f.shape, F32)),
        grid=(n,),
        in_specs=[per_seq((1, W_BIG)), per_seq((1, SMALL_W)),
                  per_seq((CONV_W - 1, GDN_CONV_CH)), per_seq((CONV_W - 1, SSM_CONV_CH)),
                  per_seq((GDN_HEADS, GDN_DK, LANE)), per_seq((SSM_DINNER, SSM_DSTATE)),
                  full((CONV_W, GDN_CONV_CH)), full((CONV_W, SSM_CONV_CH)), full((1, SSM_CONV_CH)),
                  full((SUBLANE, SMALL_W)), full((1, LANE)),
                  full((SSM_GROUPS, SMALL_W, gw)), full((1, SSM_DINNER)), full((1, SSM_DINNER))],
        out_specs=(per_seq((1, GDN_VDIM)), per_seq((1, SSM_DINNER)),
                   per_seq((GDN_HEADS, GDN_DK, LANE)), per_seq((SSM_DINNER, SSM_DSTATE))),
        scratch_shapes=[pltpu.VMEM((LANE, LANE), F32), pltpu.VMEM((LANE, gw), F32)],
        compiler_params=pltpu.CompilerParams(
            dimension_semantics=("parallel",), vmem_limit_bytes=VMEM_LIMIT),
        name="sample_mix",
    )(proj3, small3, gconv_state, sconv_state, s0, h0f, gconv_w, sconv_w, sconv_b, pv, gnorm_w,
      expand, d_exp, snorm_w)
    return (og.reshape(n, GDN_VDIM), yz.reshape(n, SSM_DINNER), s_new,
            h_new.reshape(n, SSM_HEADS, SSM_HEADDIM, SSM_DSTATE))


def _merge_kernel(og_ref, yz_ref, ga_ref, gb_ref, x_ref, wbg_ref, wbs_ref, wout_ref, lng_ref, lnb_ref,
                  rw_ref, rb_ref, x1_ref, route_ref, *, alpha):
    a = _dot(og_ref[...], wbg_ref[...])
    b = _dot(yz_ref[...], wbs_ref[...])
    merged = _sigmoid(ga_ref[...]) * a + _sigmoid(gb_ref[...]) * b
    mix = _dot(_bf(merged), wout_ref[...])
    x1 = _layer_norm(alpha * x_ref[...] + mix, lng_ref[...], lnb_ref[...])
    x1_ref[...] = x1

    lg = _dot(x1, rw_ref[...], HI) + rb_ref[...]
    colf = lax.broadcasted_iota(jnp.int32, lg.shape, 1).astype(F32)
    vals, idxs = [], []
    for _ in range(TOP_K):
        m = jnp.max(lg, axis=-1, keepdims=True)
        idx = jnp.min(jnp.where(lg == m, colf, float(LANE)), axis=-1, keepdims=True)
        vals.append(m)
        idxs.append(idx)
        lg = jnp.where(colf == idx, 2.0 * NEG_BIG, lg)
    es = [jnp.exp(v - vals[0]) for v in vals]
    den = es[0] + es[1] + es[2] + es[3]
    route = jnp.zeros_like(lg)
    for kk in range(TOP_K):
        route = jnp.where(colf == float(kk), es[kk] / den, route)
        route = jnp.where(colf == float(TOP_K + kk), idxs[kk], route)
    route_ref[...] = route


def _merge(og, yz, proj, x, wbg, wbs, wout, lng, lnb, rw, rb, alpha, tm):
    nt = x.shape[0]
    ka = (W_BIG - 2 * D_MODEL) // D_MODEL
    kern = functools.partial(_merge_kernel, alpha=alpha)

    def full(shape):
        return pl.BlockSpec(shape, lambda i: (0, 0))

    return pl.pallas_call(
        kern,
        out_shape=(jax.ShapeDtypeStruct((nt, D_MODEL), F32),
                   jax.ShapeDtypeStruct((nt, LANE), F32)),
        grid=(nt // tm,),
        in_specs=[pl.BlockSpec((tm, GDN_VDIM), lambda i: (i, 0)),
                  pl.BlockSpec((tm, SSM_DINNER), lambda i: (i, 0)),
                  pl.BlockSpec((tm, D_MODEL), lambda i: (i, ka)),
                  pl.BlockSpec((tm, D_MODEL), lambda i: (i, ka + 1)),
                  pl.BlockSpec((tm, D_MODEL), lambda i: (i, 0)),
                  full((GDN_VDIM, D_MODEL)), full((SSM_DINNER, D_MODEL)), full((D_MODEL, D_MODEL)),
                  full((1, D_MODEL)), full((1, D_MODEL)), full((D_MODEL, LANE)), full((1, LANE))],
        out_specs=(pl.BlockSpec((tm, D_MODEL), lambda i: (i, 0)),
                   pl.BlockSpec((tm, LANE), lambda i: (i, 0))),
        compiler_params=pltpu.CompilerParams(
            dimension_semantics=("parallel",), vmem_limit_bytes=VMEM_LIMIT),
        name="merge_ln_router",
    )(og, yz, proj, proj, x, wbg, wbs, wout, lng, lnb, rw, rb)


def _dispatch_kernel(dest_hbm, x_ref, xb_in, xb_out, idx, sem_idx, sem):
    del xb_in
    i = pl.program_id(0)
    cp = pltpu.make_async_copy(dest_hbm.at[i], idx, sem_idx)
    cp.start()
    cp.wait()
    n_copies = TOK_TILE * TOP_K

    def row_copy(r, d):
        return pltpu.make_async_copy(x_ref.at[pl.ds(r, 1), :], xb_out.at[pl.ds(d, 1), :], sem)

    def issue(r, carry):
        for kk in range(TOP_K):
            row_copy(r, idx[r * TOP_K + kk]).start()
        return carry

    lax.fori_loop(0, TOK_TILE, issue, 0)

    def drain(j, carry):
        row_copy(0, 0).wait()
        return carry

    lax.fori_loop(0, n_copies, drain, 0)


def _dispatch(dest2, x1, xb_zero):
    nt = x1.shape[0]
    return pl.pallas_call(
        _dispatch_kernel,
        out_shape=jax.ShapeDtypeStruct(xb_zero.shape, F32),
        grid=(nt // TOK_TILE,),
        in_specs=[pl.BlockSpec(memory_space=pl.ANY),
                  pl.BlockSpec((TOK_TILE, D_MODEL), lambda i: (i, 0)),
                  pl.BlockSpec(memory_space=pl.ANY)],
        out_specs=pl.BlockSpec(memory_space=pl.ANY),
        scratch_shapes=[pltpu.SMEM((TOK_TILE * TOP_K,), jnp.int32),
                        pltpu.SemaphoreType.DMA, pltpu.SemaphoreType.DMA],
        input_output_aliases={2: 0},
        compiler_params=pltpu.CompilerParams(
            dimension_semantics=("arbitrary",), vmem_limit_bytes=VMEM_LIMIT),
        name="moe_dispatch",
    )(dest2, x1, xb_zero)


def _expert_kernel(be_ref, nu_ref, x_ref, wg_ref, wu_ref, wd_ref, bg_ref, bu_ref, bd_ref, y_ref,
                   wgb, wub, wdb):
    j = pl.program_id(0)
    e = be_ref[j]
    prev = be_ref[jnp.maximum(j - 1, 0)]
    used = j < nu_ref[0]

    @pl.when(used & ((j == 0) | (e != prev)))
    def _():
        wgb[...] = _bf(wg_ref[...])
        wub[...] = _bf(wu_ref[...])
        wdb[...] = _bf(wd_ref[...])

    @pl.when(used)
    def _():
        x = _bf(x_ref[...])
        gt = _dot(x, wgb[...]) + bg_ref[...]
        up = _dot(x, wub[...]) + bu_ref[...]
        gt = jnp.minimum(gt, SWIGLU_LIMIT)
        up = jnp.clip(up, -SWIGLU_LIMIT, SWIGLU_LIMIT)
        h = (up + 1.0) * (gt * _sigmoid(SWIGLU_ALPHA * gt))
        y_ref[...] = _dot(_bf(h), wdb[...]) + bd_ref[...]

    @pl.when(jnp.logical_not(used))
    def _():
        y_ref[...] = jnp.zeros_like(y_ref)


def _experts(block_e, n_used, xb, wg, wu, wd, bg, bu, bd):
    rows = xb.shape[0]
    nblk = rows // MOE_BLK
    d_e = wg.shape[-1]
    wspec_in = pl.BlockSpec((None, D_MODEL, d_e), lambda j, be, nu: (be[j], 0, 0))
    wspec_out = pl.BlockSpec((None, d_e, D_MODEL), lambda j, be, nu: (be[j], 0, 0))
    bspec_e = pl.BlockSpec((None, 1, d_e), lambda j, be, nu: (be[j], 0, 0))
    bspec_d = pl.BlockSpec((None, 1, D_MODEL), lambda j, be, nu: (be[j], 0, 0))
    grid_spec = pltpu.PrefetchScalarGridSpec(
        num_scalar_prefetch=2,
        grid=(nblk,),
        in_specs=[pl.BlockSpec((MOE_BLK, D_MODEL), lambda j, be, nu: (j, 0)),
                  wspec_in, wspec_in, wspec_out, bspec_e, bspec_e, bspec_d],
        out_specs=pl.BlockSpec((MOE_BLK, D_MODEL), lambda j, be, nu: (j, 0)),
        scratch_shapes=[pltpu.VMEM((D_MODEL, d_e), BF16), pltpu.VMEM((D_MODEL, d_e), BF16),
                        pltpu.VMEM((d_e, D_MODEL), BF16)],
    )
    return pl.pallas_call(
        _expert_kernel,
        out_shape=jax.ShapeDtypeStruct((rows, D_MODEL), F32),
        grid_spec=grid_spec,
        compiler_params=pltpu.CompilerParams(
            dimension_semantics=("arbitrary",), vmem_limit_bytes=VMEM_LIMIT),
        name="moe_experts",
    )(block_e, n_used, xb, wg, wu, wd, bg.reshape(N_EXPERTS, 1, d_e), bu.reshape(N_EXPERTS, 1, d_e),
      bd.reshape(N_EXPERTS, 1, D_MODEL))


def _combine_kernel(dest_hbm, gates_ref, x1_ref, lng_ref, lnb_ref, yb_hbm, y_ref, ybf_ref, idx, buf,
                    sem_idx, sem, *, alpha):
    i = pl.program_id(0)
    cp = pltpu.make_async_copy(dest_hbm.at[i], idx, sem_idx)
    cp.start()
    cp.wait()
    n_copies = TOK_TILE * TOP_K

    def row_copy(kk, r, d):
        return pltpu.make_async_copy(yb_hbm.at[pl.ds(d, 1), :], buf.at[kk, pl.ds(r, 1), :], sem)

    def issue(r, carry):
        for kk in range(TOP_K):
            row_copy(kk, r, idx[r * TOP_K + kk]).start()
        return carry

    lax.fori_loop(0, TOK_TILE, issue, 0)

    def drain(j, carry):
        row_copy(0, 0, 0).wait()
        return carry

    lax.fori_loop(0, n_copies, drain, 0)

    gates = gates_ref[...]
    moe = gates[:, 0:1] * buf[0]
    for kk in range(1, TOP_K):
        moe = moe + gates[:, kk:kk + 1] * buf[kk]
    y = _layer_norm(alpha * x1_ref[...] + moe, lng_ref[...], lnb_ref[...])
    y_ref[...] = y
    ybf_ref[...] = _bf(y)


def _combine(dest2, route, x1, lng, lnb, yb, alpha):
    nt = x1.shape[0]
    kern = functools.partial(_combine_kernel, alpha=alpha)
    return pl.pallas_call(
        kern,
        out_shape=(jax.ShapeDtypeStruct((nt, D_MODEL), F32),
                   jax.ShapeDtypeStruct((nt, D_MODEL), BF16)),
        grid=(nt // TOK_TILE,),
        in_specs=[pl.BlockSpec(memory_space=pl.ANY),
                  pl.BlockSpec((TOK_TILE, LANE), lambda i: (i, 0)),
                  pl.BlockSpec((TOK_TILE, D_MODEL), lambda i: (i, 0)),
                  pl.BlockSpec((1, D_MODEL), lambda i: (0, 0)),
                  pl.BlockSpec((1, D_MODEL), lambda i: (0, 0)),
                  pl.BlockSpec(memory_space=pl.ANY)],
        out_specs=(pl.BlockSpec((TOK_TILE, D_MODEL), lambda i: (i, 0)),
                   pl.BlockSpec((TOK_TILE, D_MODEL), lambda i: (i, 0))),
        scratch_shapes=[pltpu.SMEM((TOK_TILE * TOP_K,), jnp.int32),
                        pltpu.VMEM((TOP_K, TOK_TILE, D_MODEL), F32),
                        pltpu.SemaphoreType.DMA, pltpu.SemaphoreType.DMA],
        compiler_params=pltpu.CompilerParams(
            dimension_semantics=("arbitrary",), vmem_limit_bytes=VMEM_LIMIT),
        name="moe_combine_ln",
    )(dest2, route, x1, lng, lnb, yb)


def _routing_tables(top_i):
    m = top_i.size
    flat_e = top_i.reshape(-1)
    onehot = (flat_e[:, None] == jnp.arange(N_EXPERTS, dtype=jnp.int32)[None, :]).astype(jnp.int32)
    csum = jnp.cumsum(onehot, axis=0)
    rank = jnp.sum(onehot * csum, axis=1) - 1
    counts = csum[-1]
    padded = (counts + MOE_BLK - 1) // MOE_BLK * MOE_BLK
    pad_ends = jnp.cumsum(padded)
    pad_starts = pad_ends - padded
    dest = jnp.sum(onehot * pad_starts[None, :], axis=1) + rank
    nblk = m // MOE_BLK + N_EXPERTS
    blk_start = jnp.arange(nblk, dtype=jnp.int32) * MOE_BLK
    block_e = jnp.minimum(jnp.searchsorted(pad_ends, blk_start, side="right"), N_EXPERTS - 1)
    n_used = (pad_ends[-1] // MOE_BLK).astype(jnp.int32).reshape(1)
    return dest.astype(jnp.int32), block_e.astype(jnp.int32), n_used, nblk


def kernel(x_prompt, x_sample, state_gdn, state_gdn_conv, state_ssm, state_ssm_conv, w_in, gdn_conv_w,
           gdn_a_log, gdn_dt_bias, gdn_norm_w, ssm_conv_w, ssm_conv_b, ssm_a_log, ssm_dt_bias, ssm_d,
           ssm_norm_w, w_br_gdn, w_br_ssm, w_out, ln1_g, ln1_b, router_w, router_b, exp_w_gate,
           exp_b_gate, exp_w_up, exp_b_up, exp_w_down, exp_b_down, ln2_g, ln2_b):
    batch, seq, _ = x_prompt.shape
    dec = x_sample.shape[0]
    depth = w_in.shape[0]
    n_p = batch * seq
    nt = n_p + dec
    alpha = (2.0 * depth) ** 0.25
    gw = SSM_DINNER // SSM_GROUPS
    hpg = SSM_HEADS // SSM_GROUPS
    tm = _pick(nt, (384, 256, 128, 64, 32, 16))
    assert seq % CHUNK == 0 and nt % TOK_TILE == 0 and (nt * TOP_K) % MOE_BLK == 0

    x = jnp.concatenate([x_prompt.reshape(n_p, D_MODEL), x_sample.reshape(dec, D_MODEL)], axis=0)
    x_bf = _bf(x)

    o_zg = GDN_CONV_CH
    o_b = o_zg + GDN_VDIM
    o_a = o_b + GDN_HEADS
    o_zs = o_a + GDN_HEADS
    o_x = o_zs + SSM_DINNER
    o_dt = o_x + SSM_CONV_CH
    o_ga = o_dt + SSM_HEADS

    rows = jnp.arange(SMALL_W, dtype=jnp.int32)[None, :, None]
    lanes = jnp.arange(gw, dtype=jnp.int32)[None, None, :]
    grp = jnp.arange(SSM_GROUPS, dtype=jnp.int32)[:, None, None]
    expand = (rows == COL_DT + grp * hpg + lanes // SSM_HEADDIM).astype(F32)

    outs = {k: [] for k in ("gdn_p", "gdn_s", "gconv_p", "gconv_s", "ssm_p", "ssm_s", "sconv_p", "sconv_s")}
    for l in range(depth):
        w = w_in[l]
        w_big = _bf(jnp.concatenate([w[:, :o_b], w[:, o_zs:o_dt], w[:, o_ga:]], axis=1))
        w_small = _bf(jnp.concatenate(
            [w[:, o_b:o_zs], w[:, o_dt:o_ga],
             jnp.zeros((D_MODEL, SMALL_W - 2 * GDN_HEADS - SSM_HEADS), F32)], axis=1))
        zpad = jnp.zeros((SMALL_W - COL_DT - SSM_HEADS,), F32)
        pv = jnp.zeros((SUBLANE, SMALL_W), F32)
        pv = pv.at[0].set(jnp.concatenate([jnp.zeros((COL_A,), F32), gdn_dt_bias[l], ssm_dt_bias[l], zpad]))
        pv = pv.at[1].set(jnp.concatenate([jnp.zeros((COL_A,), F32), gdn_a_log[l], ssm_a_log[l], zpad]))
        d_exp = jnp.repeat(ssm_d[l], SSM_HEADDIM).reshape(1, SSM_DINNER)
        gnw = gdn_norm_w[l].reshape(1, LANE)
        snw = ssm_norm_w[l].reshape(1, SSM_DINNER)
        scb = ssm_conv_b[l].reshape(1, SSM_CONV_CH)

        proj = _matmul(x_bf, w_big, tm, 1024)
        small = _matmul(x_bf, w_small, tm, SMALL_W)

        og_p, s_p = _gdn_prompt(proj, small, gdn_conv_w[l], pv, gnw, batch, seq)
        yz_p, h_p = _ssd_prompt(proj, small, ssm_conv_w[l], scb, pv, expand, d_exp, snw, batch, seq)
        proj_s = lax.slice(proj, (n_p, 0), (nt, W_BIG))
        small_s = lax.slice(small, (n_p, 0), (nt, SMALL_W))
        og_s, yz_s, s_s, h_s = _sample_mix(proj_s, small_s, state_gdn_conv[l], state_ssm_conv[l],
                                           state_gdn[l], state_ssm[l], gdn_conv_w[l], ssm_conv_w[l],
                                           scb, pv, gnw, expand, d_exp, snw)
        og = jnp.concatenate([og_p, og_s], axis=0)
        yz = jnp.concatenate([yz_p, yz_s], axis=0)

        rw = jnp.concatenate([router_w[l], jnp.zeros((D_MODEL, LANE - N_EXPERTS), F32)], axis=1)
        rb = jnp.concatenate([router_b[l], jnp.full((LANE - N_EXPERTS,), NEG_BIG, F32)]).reshape(1, LANE)
        x1, route = _merge(og, yz, proj, x, _bf(w_br_gdn[l]), _bf(w_br_ssm[l]), _bf(w_out[l]),
                           ln1_g[l].reshape(1, D_MODEL), ln1_b[l].reshape(1, D_MODEL), rw, rb, alpha, tm)

        top_i = route[:, TOP_K:2 * TOP_K].astype(jnp.int32)
        dest, block_e, n_used, nblk = _routing_tables(top_i)
        dest2 = dest.reshape(nt // TOK_TILE, TOK_TILE * TOP_K)
        xb = _dispatch(dest2, x1, jnp.zeros((nblk * MOE_BLK, D_MODEL), F32))
        yb = _experts(block_e, n_used, xb, exp_w_gate[l], exp_w_up[l], exp_w_down[l],
                      exp_b_gate[l], exp_b_up[l], exp_b_down[l])
        x, x_bf = _combine(dest2, route, x1, ln2_g[l].reshape(1, D_MODEL), ln2_b[l].reshape(1, D_MODEL),
                           yb, alpha)

        outs["gdn_p"].append(s_p)
        outs["gdn_s"].append(s_s)
        outs["ssm_p"].append(h_p.reshape(batch, SSM_HEADS, SSM_HEADDIM, SSM_DSTATE))
        outs["ssm_s"].append(h_s)
        tails_g = [lax.slice(proj, (b * seq + seq - (CONV_W - 1), 0), (b * seq + seq, GDN_CONV_CH))
                   for b in range(batch)]
        tails_s = [lax.slice(proj, (b * seq + seq - (CONV_W - 1), o_x - 2 * GDN_HEADS),
                             (b * seq + seq, o_x - 2 * GDN_HEADS + SSM_CONV_CH)) for b in range(batch)]
        outs["gconv_p"].append(jnp.stack(tails_g))
        outs["sconv_p"].append(jnp.stack(tails_s))
        outs["gconv_s"].append(jnp.concatenate(
            [state_gdn_conv[l][:, 1:], proj_s[:, None, :GDN_CONV_CH]], axis=1))
        xbc_off = o_x - 2 * GDN_HEADS
        outs["sconv_s"].append(jnp.concatenate(
            [state_ssm_conv[l][:, 1:], proj_s[:, None, xbc_off:xbc_off + SSM_CONV_CH]], axis=1))

    yp = x[:n_p].reshape(batch, seq, D_MODEL)
    ys = x[n_p:].reshape(dec, 1, D_MODEL)
    return (yp, ys, jnp.stack(outs["gdn_p"]), jnp.stack(outs["gdn_s"]), jnp.stack(outs["gconv_p"]),
            jnp.stack(outs["gconv_s"]), jnp.stack(outs["ssm_p"]), jnp.stack(outs["ssm_s"]),
            jnp.stack(outs["sconv_p"]), jnp.stack(outs["sconv_s"]))
```

```python
import functools

import jax
import jax.numpy as jnp
from jax import lax
from jax.experimental import pallas as pl
from jax.experimental.pallas import tpu as pltpu

F32 = jnp.float32
BF16 = jnp.bfloat16
HI = lax.Precision.HIGHEST

D_MODEL = 1024
GDN_HEADS = 8
GDN_DK = 128
GDN_VDIM = 1024
GDN_CONV_CH = 3072
SSM_HEADS = 32
SSM_HEADDIM = 64
SSM_GROUPS = 4
SSM_DINNER = 2048
SSM_DSTATE = 128
SSM_CONV_CH = 3072
CONV_W = 4
N_EXPERTS = 32
TOP_K = 4
SWIGLU_ALPHA = 1.702
SWIGLU_LIMIT = 7.0
LN_EPS = 1e-5
RMS_EPS = 1e-6
L2_EPS = 1e-6
NEG_BIG = -1e30

W_BIG = 11264
SMALL_W = 128
COL_B, COL_A, COL_DT = 0, 8, 16

LANE = 128
SUBLANE = 8
CHUNK = 128
GDN_HB = 8
MOE_BLK = 256
TOK_TILE = 128
TOK_ROWS = D_MODEL // LANE
VMEM_LIMIT = 56 * 1024 * 1024


def _pick(n, cands):
    for c in cands:
        if n % c == 0:
            return c
    raise ValueError(f"no tile for {n}")


def _bf(x):
    return x.astype(BF16)


def _dot(a, b, prec=None):
    return jnp.dot(a, b, preferred_element_type=F32, precision=prec)


def _dot_nt(a, b):
    return lax.dot_general(a, b, (((1,), (1,)), ((), ())), preferred_element_type=F32)


def _dot_tn(a, b):
    return lax.dot_general(a, b, (((0,), (0,)), ((), ())), preferred_element_type=F32)


def _sigmoid(x):
    return jax.nn.sigmoid(x)


def _silu(x):
    return x * jax.nn.sigmoid(x)


def _softplus(x):
    return jnp.maximum(x, 0.0) + jnp.log(1.0 + jnp.exp(-jnp.abs(x)))


def _layer_norm(x, g, b):
    mu = jnp.mean(x, axis=-1, keepdims=True)
    xc = x - mu
    var = jnp.mean(xc * xc, axis=-1, keepdims=True)
    return xc * lax.rsqrt(var + LN_EPS) * g + b


def _small_act(raw, pv):
    col = lax.broadcasted_iota(jnp.int32, raw.shape, 1)
    sp = _softplus(raw + pv[0:1, :])
    act = jnp.where(col < COL_A, _sigmoid(raw), sp)
    gda = sp * (-jnp.exp(pv[1:2, :]))
    return act, gda


def _mm_kernel(x_ref, w_ref, o_ref):
    o_ref[...] = _dot(x_ref[...], w_ref[...])


def _matmul(x, w, tm, tn):
    m, k = x.shape
    n = w.shape[1]
    return pl.pallas_call(
        _mm_kernel,
        out_shape=jax.ShapeDtypeStruct((m, n), F32),
        grid=(n // tn, m // tm),
        in_specs=[pl.BlockSpec((tm, k), lambda j, i: (i, 0)),
                  pl.BlockSpec((k, tn), lambda j, i: (0, j))],
        out_specs=pl.BlockSpec((tm, tn), lambda j, i: (i, j)),
        compiler_params=pltpu.CompilerParams(
            dimension_semantics=("parallel", "parallel"), vmem_limit_bytes=VMEM_LIMIT),
        name="in_proj",
    )(x, w)


def _tri_inv_all(mats, ii, jj, c):
    eye = (ii == jj).astype(F32)
    pair = (ii >> 1) == (jj >> 1)
    ts = [eye - jnp.where(pair, a, 0.0) for a in mats]
    s = 1
    while (2 << s) <= c:
        same_outer = (ii >> (s + 1)) == (jj >> (s + 1))
        same_inner = (ii >> s) == (jj >> s)
        off = same_outer & jnp.logical_not(same_inner)
        tbs = [_bf(t) for t in ts]
        tes = [_dot(tb, _bf(jnp.where(off, a, 0.0))) for tb, a in zip(tbs, mats)]
        ts = [t - _dot(_bf(te), tb) for t, te, tb in zip(ts, tes, tbs)]
        s += 1
    return ts


def _gdn_kernel(q_ref, k_ref, v_ref, zg_ref, small_ref, cwq_ref, cwk_ref, cwv_ref, pv_ref, nw_ref,
                o_ref, sfin_ref, xf, tail, s_scr):
    c_len = q_ref.shape[0]
    hg = pl.program_id(1)
    c = pl.program_id(2)
    nc = pl.num_programs(2)

    @pl.when(c == 0)
    def _():
        tail[...] = jnp.zeros_like(tail)
        s_scr[...] = jnp.zeros_like(s_scr)

    for p, r in enumerate((q_ref, k_ref, v_ref)):
        xf[p, 0:SUBLANE, :] = tail[p]
        xf[p, SUBLANE:SUBLANE + c_len, :] = r[...]
        tail[p] = r[c_len - SUBLANE:c_len, :]
    cws = (cwq_ref, cwk_ref, cwv_ref)

    def conv(p, hs):
        acc = None
        for i in range(CONV_W):
            term = cws[p][i:i + 1, hs] * xf[p, pl.ds(SUBLANE - (CONV_W - 1) + i, c_len), hs]
            acc = term if acc is None else acc + term
        return _silu(acc)

    act, gda = _small_act(small_ref[...], pv_ref[...])
    ii = lax.broadcasted_iota(jnp.int32, (c_len, c_len), 0)
    jj = lax.broadcasted_iota(jnp.int32, (c_len, c_len), 1)
    incl = ii >= jj
    strict = ii > jj
    gcum = _dot(incl.astype(F32), gda, HI)
    gcum_t = gcum.T
    col = lax.broadcasted_iota(jnp.int32, (c_len, SMALL_W), 1)
    row_t = lax.broadcasted_iota(jnp.int32, (SMALL_W, c_len), 0)
    nw = nw_ref[...]

    heads = range(GDN_HB)
    hsl = [slice(hh * LANE, (hh + 1) * LANE) for hh in heads]
    betas, gcs, grs = [], [], []
    for hh in heads:
        head = hg * GDN_HB + hh
        betas.append(jnp.sum(jnp.where(col == COL_B + head, act, 0.0), axis=-1, keepdims=True))
        gcs.append(jnp.sum(jnp.where(col == COL_A + head, gcum, 0.0), axis=-1, keepdims=True))
        grs.append(jnp.sum(jnp.where(row_t == COL_A + head, gcum_t, 0.0), axis=0, keepdims=True))
    ks = []
    for hh in heads:
        k = conv(1, hsl[hh])
        ks.append(k * lax.rsqrt(jnp.sum(k * k, axis=-1, keepdims=True) + L2_EPS))
    kbs = [ks[hh] * betas[hh] for hh in heads]
    kbfs = [_bf(k) for k in ks]
    kks = [_dot_nt(_bf(kbs[hh]), kbfs[hh]) for hh in heads]
    gams = [jnp.exp(jnp.where(incl, gcs[hh] - grs[hh], NEG_BIG)) for hh in heads]
    amats = [jnp.where(strict, kks[hh] * gams[hh], 0.0) for hh in heads]
    qs = []
    for hh in heads:
        q = conv(0, hsl[hh])
        qs.append(q * lax.rsqrt(jnp.sum(q * q, axis=-1, keepdims=True) + L2_EPS) * (GDN_DK ** -0.5))
    qks = [_dot_nt(_bf(qs[hh]), kbfs[hh]) * gams[hh] for hh in heads]
    egs = [jnp.exp(gcs[hh]) for hh in heads]
    rhs = [_bf(jnp.concatenate([conv(2, hsl[hh]) * betas[hh], kbs[hh] * egs[hh]], axis=1)) for hh in heads]
    ts = _tri_inv_all(amats, ii, jj, c_len)
    uws = [_dot(_bf(ts[hh]), rhs[hh]) for hh in heads]
    glasts = [gcs[hh][c_len - 1:c_len, :] for hh in heads]
    s_olds = [s_scr[hh] for hh in heads]
    sbs = [_bf(s) for s in s_olds]
    v_news = [uws[hh][:, :LANE] - _dot(_bf(uws[hh][:, LANE:]), sbs[hh]) for hh in heads]
    vnbs = [_bf(v) for v in v_news]
    os_ = [_dot(_bf(qs[hh] * egs[hh]), sbs[hh]) + _dot(_bf(qks[hh]), vnbs[hh]) for hh in heads]
    for hh in heads:
        kdec = ks[hh] * jnp.exp(glasts[hh] - gcs[hh])
        s_scr[hh] = s_olds[hh] * jnp.exp(glasts[hh]) + _dot_tn(_bf(kdec), vnbs[hh])
    for hh in heads:
        o = os_[hh]
        o = (o * lax.rsqrt(jnp.mean(o * o, axis=-1, keepdims=True) + RMS_EPS) * nw
             * _silu(zg_ref[:, hsl[hh]]))
        o_ref[:, hsl[hh]] = _bf(o)

    @pl.when(c == nc - 1)
    def _():
        sfin_ref[0] = s_scr[...]


def _gdn_prompt(proj, small, conv_w, pv, norm_w, batch, seq):
    nc = seq // CHUNK
    hbw = GDN_HB * LANE
    ngrp = GDN_HEADS // GDN_HB
    kq, kk, kv, kz = 0, GDN_VDIM // hbw, 2 * GDN_VDIM // hbw, 3 * GDN_VDIM // hbw

    def row(b, hg, c):
        return b * nc + c

    def pspec(off):
        return pl.BlockSpec((CHUNK, hbw), lambda b, hg, c: (row(b, hg, c), off + hg))

    def wspec(off):
        return pl.BlockSpec((CONV_W, hbw), lambda b, hg, c: (0, off + hg))

    return pl.pallas_call(
        _gdn_kernel,
        out_shape=(jax.ShapeDtypeStruct((batch * seq, GDN_VDIM), BF16),
                   jax.ShapeDtypeStruct((batch, GDN_HEADS, GDN_DK, LANE), F32)),
        grid=(batch, ngrp, nc),
        in_specs=[pspec(kq), pspec(kk), pspec(kv), pspec(kz),
                  pl.BlockSpec((CHUNK, SMALL_W), lambda b, hg, c: (row(b, hg, c), 0)),
                  wspec(kq), wspec(kk), wspec(kv),
                  pl.BlockSpec((SUBLANE, SMALL_W), lambda b, hg, c: (0, 0)),
                  pl.BlockSpec((1, LANE), lambda b, hg, c: (0, 0))],
        out_specs=(pl.BlockSpec((CHUNK, hbw), lambda b, hg, c: (row(b, hg, c), hg)),
                   pl.BlockSpec((1, GDN_HB, GDN_DK, LANE), lambda b, hg, c: (b, hg, 0, 0))),
        scratch_shapes=[pltpu.VMEM((3, CHUNK + SUBLANE, hbw), F32),
                        pltpu.VMEM((3, SUBLANE, hbw), F32),
                        pltpu.VMEM((GDN_HB, GDN_DK, LANE), F32)],
        compiler_params=pltpu.CompilerParams(
            dimension_semantics=("parallel", "parallel", "arbitrary"), vmem_limit_bytes=VMEM_LIMIT),
        name="gdn_prompt",
    )(proj, proj, proj, proj, small, conv_w, conv_w, conv_w, pv, norm_w)


def _ssd_kernel(xs_ref, b_ref, c_ref, zs_ref, small_ref, cwx_ref, cwb_ref, cwc_ref, cbx_ref, cbb_ref,
                cbc_ref, pv_ref, ex_ref, dexp_ref, nw_ref, yz_ref, hfin_ref, xf, tail, ht, ydiag):
    c_len = xs_ref.shape[0]
    gw = xs_ref.shape[1]
    hpg = gw // SSM_HEADDIM
    g = pl.program_id(1)
    c = pl.program_id(2)
    nc = pl.num_programs(2)

    @pl.when(c == 0)
    def _():
        tail[...] = jnp.zeros_like(tail)
        ht[...] = jnp.zeros_like(ht)

    parts = ((xs_ref, cwx_ref, cbx_ref, 0, gw), (b_ref, cwb_ref, cbb_ref, gw, SSM_DSTATE),
             (c_ref, cwc_ref, cbc_ref, gw + SSM_DSTATE, SSM_DSTATE))
    convs = []
    for r, cw, cb, off, wd in parts:
        sl = slice(off, off + wd)
        xf[0:SUBLANE, sl] = tail[:, sl]
        xf[SUBLANE:SUBLANE + c_len, sl] = r[...]
        tail[:, sl] = r[c_len - SUBLANE:c_len, :]
        acc = cb[...]
        for i in range(CONV_W):
            acc = acc + cw[i:i + 1, :] * xf[pl.ds(SUBLANE - (CONV_W - 1) + i, c_len), sl]
        convs.append(_silu(acc))
    xs, bm, cm = convs

    act, gda = _small_act(small_ref[...], pv_ref[...])
    ii = lax.broadcasted_iota(jnp.int32, (c_len, c_len), 0)
    jj = lax.broadcasted_iota(jnp.int32, (c_len, c_len), 1)
    incl = ii >= jj
    acs = _dot(incl.astype(F32), gda, HI)
    acs_t = acs.T
    ex = ex_ref[...]
    dt_x = _dot(act, ex, HI)
    acs_x = _dot(acs, ex, HI)
    last = acs_x[c_len - 1:c_len, :]
    xdt = xs * dt_x
    cb_m = _dot_nt(_bf(cm), _bf(bm))
    h_old = ht[...]
    y_off = _dot(_bf(cm), _bf(h_old)) * jnp.exp(acs_x)
    col = lax.broadcasted_iota(jnp.int32, (c_len, SMALL_W), 1)
    row_t = lax.broadcasted_iota(jnp.int32, (SMALL_W, c_len), 0)
    lane = lax.broadcasted_iota(jnp.int32, (c_len, LANE), 1)
    for pr in range(hpg // 2):
        ps = slice(pr * LANE, (pr + 1) * LANE)
        xpair = xdt[:, ps]
        acc = None
        for half in range(2):
            head = g * hpg + pr * 2 + half
            ac = jnp.sum(jnp.where(col == COL_DT + head, acs, 0.0), axis=-1, keepdims=True)
            ar = jnp.sum(jnp.where(row_t == COL_DT + head, acs_t, 0.0), axis=0, keepdims=True)
            sc = cb_m * jnp.exp(jnp.where(incl, ac - ar, NEG_BIG))
            in_half = (lane >= half * SSM_HEADDIM) & (lane < (half + 1) * SSM_HEADDIM)
            term = _dot(_bf(sc), _bf(jnp.where(in_half, xpair, 0.0)))
            acc = term if acc is None else acc + term
        ydiag[:, ps] = acc
    ht[...] = h_old * jnp.exp(last) + _dot_tn(_bf(bm), _bf(xdt * jnp.exp(last - acs_x)))
    y = ydiag[...] + y_off + dexp_ref[...] * xs
    yz = y * _silu(zs_ref[...])
    yz = yz * lax.rsqrt(jnp.mean(yz * yz, axis=-1, keepdims=True) + RMS_EPS) * nw_ref[...]
    yz_ref[...] = _bf(yz)

    @pl.when(c == nc - 1)
    def _():
        hfin_ref[0] = ht[...].T


def _ssd_prompt(proj, small, conv_w, conv_b, pv, expand, d_exp, norm_w, batch, seq):
    nc = seq // CHUNK
    gw = SSM_DINNER // SSM_GROUPS
    x_off = (GDN_CONV_CH + GDN_VDIM + SSM_DINNER)
    z_off = GDN_CONV_CH + GDN_VDIM
    kx = x_off // gw
    kb = (x_off + SSM_DINNER) // SSM_DSTATE
    kc = kb + SSM_GROUPS
    kz = z_off // gw
    wb = SSM_DINNER // SSM_DSTATE
    wc = wb + SSM_GROUPS

    def row(b, g, c):
        return b * nc + c

    in_specs = [
        pl.BlockSpec((CHUNK, gw), lambda b, g, c: (row(b, g, c), kx + g)),
        pl.BlockSpec((CHUNK, SSM_DSTATE), lambda b, g, c: (row(b, g, c), kb + g)),
        pl.BlockSpec((CHUNK, SSM_DSTATE), lambda b, g, c: (row(b, g, c), kc + g)),
        pl.BlockSpec((CHUNK, gw), lambda b, g, c: (row(b, g, c), kz + g)),
        pl.BlockSpec((CHUNK, SMALL_W), lambda b, g, c: (row(b, g, c), 0)),
        pl.BlockSpec((CONV_W, gw), lambda b, g, c: (0, g)),
        pl.BlockSpec((CONV_W, SSM_DSTATE), lambda b, g, c: (0, wb + g)),
        pl.BlockSpec((CONV_W, SSM_DSTATE), lambda b, g, c: (0, wc + g)),
        pl.BlockSpec((1, gw), lambda b, g, c: (0, g)),
        pl.BlockSpec((1, SSM_DSTATE), lambda b, g, c: (0, wb + g)),
        pl.BlockSpec((1, SSM_DSTATE), lambda b, g, c: (0, wc + g)),
        pl.BlockSpec((SUBLANE, SMALL_W), lambda b, g, c: (0, 0)),
        pl.BlockSpec((None, SMALL_W, gw), lambda b, g, c: (g, 0, 0)),
        pl.BlockSpec((1, gw), lambda b, g, c: (0, g)),
        pl.BlockSpec((1, gw), lambda b, g, c: (0, g)),
    ]
    return pl.pallas_call(
        _ssd_kernel,
        out_shape=(jax.ShapeDtypeStruct((batch * seq, SSM_DINNER), BF16),
                   jax.ShapeDtypeStruct((batch, SSM_DINNER, SSM_DSTATE), F32)),
        grid=(batch, SSM_GROUPS, nc),
        in_specs=in_specs,
        out_specs=(pl.BlockSpec((CHUNK, gw), lambda b, g, c: (row(b, g, c), g)),
                   pl.BlockSpec((1, gw, SSM_DSTATE), lambda b, g, c: (b, g, 0))),
        scratch_shapes=[pltpu.VMEM((CHUNK + SUBLANE, gw + 2 * SSM_DSTATE), F32),
                        pltpu.VMEM((SUBLANE, gw + 2 * SSM_DSTATE), F32),
                        pltpu.VMEM((SSM_DSTATE, gw), F32),
                        pltpu.VMEM((CHUNK, gw), F32)],
        compiler_params=pltpu.CompilerParams(
            dimension_semantics=("parallel", "parallel", "arbitrary"), vmem_limit_bytes=VMEM_LIMIT),
        name="ssd_prompt",
    )(proj, proj, proj, proj, small, conv_w, conv_w, conv_w, conv_b, conv_b, conv_b, pv, expand,
      d_exp, norm_w)


def _sample_kernel(proj_ref, small_ref, gcs_ref, scs_ref, s0_ref, h0_ref, gcw_ref, scw_ref, scb_ref,
                   pv_ref, gnw_ref, ex_ref, dexp_ref, snw_ref, *rest, n_alias):
    og_ref, yz_ref, s_ref, h_ref, stk, stk2 = rest[n_alias:]
    if n_alias == 0:
        if s_ref.shape[0] > 1:
            s_ref[1:] = jnp.zeros((s_ref.shape[0] - 1,) + s_ref.shape[1:], F32)
            h_ref[1:] = jnp.zeros((h_ref.shape[0] - 1,) + h_ref.shape[1:], F32)
        s_ref = s_ref.at[0]
        h_ref = h_ref.at[0]
    gw = SSM_DINNER // SSM_GROUPS
    z_off = GDN_CONV_CH + GDN_VDIM
    x_off = z_off + SSM_DINNER

    def conv1(state_ref, w_ref, new_row):
        acc = w_ref[CONV_W - 1:CONV_W, :] * new_row
        for i in range(CONV_W - 1):
            acc = acc + w_ref[i:i + 1, :] * state_ref[i:i + 1, :]
        return acc

    act, gda = _small_act(small_ref[...], pv_ref[...])

    qkv = _silu(conv1(gcs_ref, gcw_ref, proj_ref[:, 0:GDN_CONV_CH]))
    stk[...] = jnp.zeros_like(stk)
    qs, vs = [], []
    for h in range(GDN_HEADS):
        q = qkv[:, h * LANE:(h + 1) * LANE]
        k = qkv[:, GDN_VDIM + h * LANE:GDN_VDIM + (h + 1) * LANE]
        q = q * lax.rsqrt(jnp.sum(q * q, axis=-1, keepdims=True) + L2_EPS) * (GDN_DK ** -0.5)
        k = k * lax.rsqrt(jnp.sum(k * k, axis=-1, keepdims=True) + L2_EPS)
        stk[h:h + 1, :] = k
        stk[GDN_HEADS + h:GDN_HEADS + h + 1, :] = q
        vs.append(qkv[:, 2 * GDN_VDIM + h * LANE:2 * GDN_VDIM + (h + 1) * LANE])
    cols = stk[...].T
    gnw = gnw_ref[...]
    for h in range(GDN_HEADS):
        kc = cols[:, h:h + 1]
        qc = cols[:, GDN_HEADS + h:GDN_HEADS + h + 1]
        beta = act[:, COL_B + h:COL_B + h + 1]
        gh = gda[:, COL_A + h:COL_A + h + 1]
        sd = s0_ref[h] * jnp.exp(gh)
        v_old = jnp.sum(sd * kc, axis=0, keepdims=True)
        delta = (vs[h] - v_old) * beta
        s_new = sd + kc * delta
        s_ref[h] = s_new
        o = jnp.sum(s_new * qc, axis=0, keepdims=True)
        zg = proj_ref[:, GDN_CONV_CH + h * LANE:GDN_CONV_CH + (h + 1) * LANE]
        o = o * lax.rsqrt(jnp.mean(o * o, axis=-1, keepdims=True) + RMS_EPS) * gnw * _silu(zg)
        og_ref[:, h * LANE:(h + 1) * LANE] = _bf(o)

    xbc = _silu(conv1(scs_ref, scw_ref, proj_ref[:, x_off:x_off + SSM_CONV_CH]) + scb_ref[...])
    row16 = lax.broadcasted_iota(jnp.int32, (2 * SUBLANE, SMALL_W), 0)
    act16 = jnp.where(row16 == 0, jnp.broadcast_to(act, (2 * SUBLANE, SMALL_W)), 0.0)
    gda16 = jnp.where(row16 == 0, jnp.broadcast_to(gda, (2 * SUBLANE, SMALL_W)), 0.0)
    dexp = dexp_ref[...]
    stk2[...] = jnp.zeros_like(stk2)
    for g in range(SSM_GROUPS):
        gs = slice(g * gw, (g + 1) * gw)
        ex = ex_ref[g]
        dt_x = _dot(act16, ex, HI)[0:1, :]
        da_x = _dot(gda16, ex, HI)[0:1, :]
        xs = xbc[:, gs]
        bm = xbc[:, SSM_DINNER + g * SSM_DSTATE:SSM_DINNER + (g + 1) * SSM_DSTATE]
        cm = xbc[:, SSM_DINNER + SSM_GROUPS * SSM_DSTATE + g * SSM_DSTATE:
                 SSM_DINNER + SSM_GROUPS * SSM_DSTATE + (g + 1) * SSM_DSTATE]
        stk2[0:1, :] = jnp.exp(da_x)
        stk2[1:2, :] = xs * dt_x
        cols2 = stk2[...].T
        h_new = h0_ref[gs, :] * cols2[:, 0:1] + cols2[:, 1:2] * bm
        h_ref[gs, :] = h_new
        cm16 = jnp.broadcast_to(cm, (2 * SUBLANE, SSM_DSTATE))
        y = _dot_nt(_bf(cm16), _bf(h_new))[0:1, :] + dexp[:, gs] * xs
        yz = y * _silu(proj_ref[:, z_off + g * gw:z_off + (g + 1) * gw])
        yz = yz * lax.rsqrt(jnp.mean(yz * yz, axis=-1, keepdims=True) + RMS_EPS) * snw_ref[:, gs]
        yz_ref[:, gs] = _bf(yz)


def _sample_mix(layer, proj_s, small_s, gconv_state, sconv_state, s_all, h_all, s_prev, h_prev, gconv_w,
                sconv_w, sconv_b, pv, gnorm_w, expand, d_exp, snorm_w):
    n = proj_s.shape[0]
    gw = SSM_DINNER // SSM_GROUPS
    proj3 = proj_s.reshape(n, 1, W_BIG)
    small3 = small_s.reshape(n, 1, SMALL_W)

    def full(shape):
        nd = len(shape)
        return pl.BlockSpec(shape, lambda i: (0,) * nd)

    def per_seq(shape):
        nd = len(shape)
        return pl.BlockSpec((None,) + shape, lambda i: (i,) + (0,) * nd)

    def per_layer_seq(shape):
        nd = len(shape)
        return pl.BlockSpec((None, None) + shape, lambda i: (layer, i) + (0,) * nd)

    in_specs = [per_seq((1, W_BIG)), per_seq((1, SMALL_W)),
                per_seq((CONV_W - 1, GDN_CONV_CH)), per_seq((CONV_W - 1, SSM_CONV_CH)),
                per_layer_seq((GDN_HEADS, GDN_DK, LANE)), per_layer_seq((SSM_DINNER, SSM_DSTATE)),
                full((CONV_W, GDN_CONV_CH)), full((CONV_W, SSM_CONV_CH)), full((1, SSM_CONV_CH)),
                full((SUBLANE, SMALL_W)), full((1, LANE)),
                full((SSM_GROUPS, SMALL_W, gw)), full((1, SSM_DINNER)), full((1, SSM_DINNER))]
    args = [proj3, small3, gconv_state, sconv_state, s_all, h_all, gconv_w, sconv_w, sconv_b, pv, gnorm_w,
            expand, d_exp, snorm_w]
    aliases = {}
    if s_prev is not None:
        aliases = {len(args): 2, len(args) + 1: 3}
        in_specs += [pl.BlockSpec(memory_space=pl.ANY), pl.BlockSpec(memory_space=pl.ANY)]
        args += [s_prev, h_prev]
    if s_prev is None:
        depth = s_all.shape[0]
        state_specs = [pl.BlockSpec((depth, None, GDN_HEADS, GDN_DK, LANE), lambda i: (0, i, 0, 0, 0)),
                       pl.BlockSpec((depth, None, SSM_DINNER, SSM_DSTATE), lambda i: (0, i, 0, 0))]
    else:
        state_specs = [per_layer_seq((GDN_HEADS, GDN_DK, LANE)), per_layer_seq((SSM_DINNER, SSM_DSTATE))]
    kern = functools.partial(_sample_kernel, n_alias=len(aliases))
    og, yz, s_new, h_new = pl.pallas_call(
        kern,
        out_shape=(jax.ShapeDtypeStruct((n, 1, GDN_VDIM), BF16),
                   jax.ShapeDtypeStruct((n, 1, SSM_DINNER), BF16),
                   jax.ShapeDtypeStruct(s_all.shape, F32),
                   jax.ShapeDtypeStruct(h_all.shape, F32)),
        grid=(n,),
        in_specs=in_specs,
        out_specs=(per_seq((1, GDN_VDIM)), per_seq((1, SSM_DINNER)), state_specs[0], state_specs[1]),
        scratch_shapes=[pltpu.VMEM((LANE, LANE), F32), pltpu.VMEM((LANE, gw), F32)],
        input_output_aliases=aliases,
        compiler_params=pltpu.CompilerParams(
            dimension_semantics=("parallel",), vmem_limit_bytes=VMEM_LIMIT),
        name="sample_mix",
    )(*args)
    return og.reshape(n, GDN_VDIM), yz.reshape(n, SSM_DINNER), s_new, h_new


def _merge_kernel(og_ref, yz_ref, ga_ref, gb_ref, x_ref, wbg_ref, wbs_ref, wout_ref, lng_ref, lnb_ref,
                  rw_ref, rb_ref, x1_ref, x1t_ref, route_ref, *, alpha):
    a = _dot(og_ref[...], wbg_ref[...])
    b = _dot(yz_ref[...], wbs_ref[...])
    merged = _sigmoid(ga_ref[...]) * a + _sigmoid(gb_ref[...]) * b
    mix = _dot(_bf(merged), wout_ref[...])
    x1 = _layer_norm(alpha * x_ref[...] + mix, lng_ref[...], lnb_ref[...])
    x1_ref[...] = x1
    _store_token_tiles(x1t_ref, x1)

    lg = _dot(x1, rw_ref[...], HI) + rb_ref[...]
    colf = lax.broadcasted_iota(jnp.int32, lg.shape, 1).astype(F32)
    vals, idxs = [], []
    for _ in range(TOP_K):
        m = jnp.max(lg, axis=-1, keepdims=True)
        idx = jnp.min(jnp.where(lg == m, colf, float(LANE)), axis=-1, keepdims=True)
        vals.append(m)
        idxs.append(idx)
        lg = jnp.where(colf == idx, 2.0 * NEG_BIG, lg)
    es = [jnp.exp(v - vals[0]) for v in vals]
    den = es[0] + es[1] + es[2] + es[3]
    route = jnp.zeros_like(lg)
    for kk in range(TOP_K):
        route = jnp.where(colf == float(kk), es[kk] / den, route)
        route = jnp.where(colf == float(TOP_K + kk), idxs[kk], route)
    route_ref[...] = route


def _merge(og, yz, proj, x, wbg, wbs, wout, lng, lnb, rw, rb, alpha, tm):
    nt = x.shape[0]
    ka = (W_BIG - 2 * D_MODEL) // D_MODEL
    kern = functools.partial(_merge_kernel, alpha=alpha)

    def full(shape):
        return pl.BlockSpec(shape, lambda i: (0, 0))

    return pl.pallas_call(
        kern,
        out_shape=(jax.ShapeDtypeStruct((nt, D_MODEL), F32),
                   jax.ShapeDtypeStruct((nt * TOK_ROWS, LANE), F32),
                   jax.ShapeDtypeStruct((nt, LANE), F32)),
        grid=(nt // tm,),
        in_specs=[pl.BlockSpec((tm, GDN_VDIM), lambda i: (i, 0)),
                  pl.BlockSpec((tm, SSM_DINNER), lambda i: (i, 0)),
                  pl.BlockSpec((tm, D_MODEL), lambda i: (i, ka)),
                  pl.BlockSpec((tm, D_MODEL), lambda i: (i, ka + 1)),
                  pl.BlockSpec((tm, D_MODEL), lambda i: (i, 0)),
                  full((GDN_VDIM, D_MODEL)), full((SSM_DINNER, D_MODEL)), full((D_MODEL, D_MODEL)),
                  full((1, D_MODEL)), full((1, D_MODEL)), full((D_MODEL, LANE)), full((1, LANE))],
        out_specs=(pl.BlockSpec((tm, D_MODEL), lambda i: (i, 0)),
                   pl.BlockSpec((tm * TOK_ROWS, LANE), lambda i: (i, 0)),
                   pl.BlockSpec((tm, LANE), lambda i: (i, 0))),
        compiler_params=pltpu.CompilerParams(
            dimension_semantics=("parallel",), vmem_limit_bytes=VMEM_LIMIT),
        name="merge_ln_router",
    )(og, yz, proj, proj, x, wbg, wbs, wout, lng, lnb, rw, rb)


def _store_token_tiles(ref, val):
    n = val.shape[0]
    for cc in range(TOK_ROWS):
        ref[pl.ds(cc, n, stride=TOK_ROWS), :] = val[:, cc * LANE:(cc + 1) * LANE]


def _load_token_chunk(ref, cc, n):
    return ref[pl.ds(cc, n, stride=TOK_ROWS), :]


def _token_rows(t):
    return pl.ds(pl.multiple_of(t * TOK_ROWS, TOK_ROWS), TOK_ROWS)


def _dispatch_kernel(dest_hbm, x_ref, xb_in, xb_out, idx, sem_idx, sem):
    del xb_in
    i = pl.program_id(0)
    cp = pltpu.make_async_copy(dest_hbm.at[i], idx, sem_idx)
    cp.start()
    cp.wait()

    def issue(r, carry):
        for kk in range(TOP_K):
            pltpu.make_async_copy(x_ref.at[_token_rows(r), :],
                                  xb_out.at[_token_rows(idx[r * TOP_K + kk]), :], sem).start()
        return carry

    lax.fori_loop(0, TOK_TILE, issue, 0, unroll=8)
    for _ in range(TOP_K):
        pltpu.make_async_copy(x_ref, xb_out.at[pl.ds(0, TOK_TILE * TOK_ROWS), :], sem).wait()


def _dispatch(dest2, x1t, xb_zero):
    n_tiles = dest2.shape[0]
    return pl.pallas_call(
        _dispatch_kernel,
        out_shape=jax.ShapeDtypeStruct(xb_zero.shape, F32),
        grid=(n_tiles,),
        in_specs=[pl.BlockSpec(memory_space=pl.ANY),
                  pl.BlockSpec((TOK_TILE * TOK_ROWS, LANE), lambda i: (i, 0)),
                  pl.BlockSpec(memory_space=pl.ANY)],
        out_specs=pl.BlockSpec(memory_space=pl.ANY),
        scratch_shapes=[pltpu.SMEM((TOK_TILE * TOP_K,), jnp.int32),
                        pltpu.SemaphoreType.DMA, pltpu.SemaphoreType.DMA],
        input_output_aliases={2: 0},
        compiler_params=pltpu.CompilerParams(
            dimension_semantics=("arbitrary",), vmem_limit_bytes=VMEM_LIMIT),
        name="moe_dispatch",
    )(dest2, x1t, xb_zero)


def _expert_kernel(be_ref, nu_ref, x_ref, wg_ref, wu_ref, wd_ref, bg_ref, bu_ref, bd_ref, y_ref,
                   wgb, wub, wdb):
    j = pl.program_id(0)
    e = be_ref[j]
    prev = be_ref[jnp.maximum(j - 1, 0)]
    used = j < nu_ref[0]

    @pl.when(used & ((j == 0) | (e != prev)))
    def _():
        wgb[...] = _bf(wg_ref[...])
        wub[...] = _bf(wu_ref[...])
        wdb[...] = _bf(wd_ref[...])

    @pl.when(used)
    def _():
        x = jnp.concatenate([_bf(_load_token_chunk(x_ref, cc, MOE_BLK)) for cc in range(TOK_ROWS)], axis=1)
        gt = _dot(x, wgb[...]) + bg_ref[...]
        up = _dot(x, wub[...]) + bu_ref[...]
        gt = jnp.minimum(gt, SWIGLU_LIMIT)
        up = jnp.clip(up, -SWIGLU_LIMIT, SWIGLU_LIMIT)
        h = (up + 1.0) * (gt * _sigmoid(SWIGLU_ALPHA * gt))
        _store_token_tiles(y_ref, _dot(_bf(h), wdb[...]) + bd_ref[...])

    @pl.when(jnp.logical_not(used))
    def _():
        y_ref[...] = jnp.zeros_like(y_ref)


def _experts(block_e, n_used, xb, wg, wu, wd, bg, bu, bd):
    rows = xb.shape[0]
    blk_rows = MOE_BLK * TOK_ROWS
    nblk = rows // blk_rows
    d_e = wg.shape[-1]
    wspec_in = pl.BlockSpec((None, D_MODEL, d_e), lambda j, be, nu: (be[j], 0, 0))
    wspec_out = pl.BlockSpec((None, d_e, D_MODEL), lambda j, be, nu: (be[j], 0, 0))
    bspec_e = pl.BlockSpec((None, 1, d_e), lambda j, be, nu: (be[j], 0, 0))
    bspec_d = pl.BlockSpec((None, 1, D_MODEL), lambda j, be, nu: (be[j], 0, 0))
    grid_spec = pltpu.PrefetchScalarGridSpec(
        num_scalar_prefetch=2,
        grid=(nblk,),
        in_specs=[pl.BlockSpec((blk_rows, LANE), lambda j, be, nu: (j, 0)),
                  wspec_in, wspec_in, wspec_out, bspec_e, bspec_e, bspec_d],
        out_specs=pl.BlockSpec((blk_rows, LANE), lambda j, be, nu: (j, 0)),
        scratch_shapes=[pltpu.VMEM((D_MODEL, d_e), BF16), pltpu.VMEM((D_MODEL, d_e), BF16),
                        pltpu.VMEM((d_e, D_MODEL), BF16)],
    )
    return pl.pallas_call(
        _expert_kernel,
        out_shape=jax.ShapeDtypeStruct((rows, LANE), F32),
        grid_spec=grid_spec,
        compiler_params=pltpu.CompilerParams(
            dimension_semantics=("arbitrary",), vmem_limit_bytes=VMEM_LIMIT),
        name="moe_experts",
    )(block_e, n_used, xb, wg, wu, wd, bg.reshape(N_EXPERTS, 1, d_e), bu.reshape(N_EXPERTS, 1, d_e),
      bd.reshape(N_EXPERTS, 1, D_MODEL))


def _combine_kernel(dest_hbm, gates_ref, x1_ref, lng_ref, lnb_ref, yb_hbm, y_ref, ybf_ref, idx, buf,
                    sem_idx, sem, *, alpha):
    i = pl.program_id(0)
    cp = pltpu.make_async_copy(dest_hbm.at[i], idx, sem_idx)
    cp.start()
    cp.wait()

    def issue(r, carry):
        for kk in range(TOP_K):
            pltpu.make_async_copy(yb_hbm.at[_token_rows(idx[r * TOP_K + kk]), :],
                                  buf.at[kk, _token_rows(r), :], sem).start()
        return carry

    lax.fori_loop(0, TOK_TILE, issue, 0, unroll=8)
    for kk in range(TOP_K):
        pltpu.make_async_copy(yb_hbm.at[pl.ds(0, TOK_TILE * TOK_ROWS), :], buf.at[kk], sem).wait()

    gates = gates_ref[...]
    chunks = []
    for cc in range(TOK_ROWS):
        acc = gates[:, 0:1] * _load_token_chunk(buf.at[0], cc, TOK_TILE)
        for kk in range(1, TOP_K):
            acc = acc + gates[:, kk:kk + 1] * _load_token_chunk(buf.at[kk], cc, TOK_TILE)
        chunks.append(acc)
    moe = jnp.concatenate(chunks, axis=1)
    y = _layer_norm(alpha * x1_ref[...] + moe, lng_ref[...], lnb_ref[...])
    y_ref[...] = y
    ybf_ref[...] = _bf(y)


def _combine(dest2, route, x1, lng, lnb, yb, alpha):
    nt = x1.shape[0]
    kern = functools.partial(_combine_kernel, alpha=alpha)
    return pl.pallas_call(
        kern,
        out_shape=(jax.ShapeDtypeStruct((nt, D_MODEL), F32),
                   jax.ShapeDtypeStruct((nt, D_MODEL), BF16)),
        grid=(nt // TOK_TILE,),
        in_specs=[pl.BlockSpec(memory_space=pl.ANY),
                  pl.BlockSpec((TOK_TILE, LANE), lambda i: (i, 0)),
                  pl.BlockSpec((TOK_TILE, D_MODEL), lambda i: (i, 0)),
                  pl.BlockSpec((1, D_MODEL), lambda i: (0, 0)),
                  pl.BlockSpec((1, D_MODEL), lambda i: (0, 0)),
                  pl.BlockSpec(memory_space=pl.ANY)],
        out_specs=(pl.BlockSpec((TOK_TILE, D_MODEL), lambda i: (i, 0)),
                   pl.BlockSpec((TOK_TILE, D_MODEL), lambda i: (i, 0))),
        scratch_shapes=[pltpu.SMEM((TOK_TILE * TOP_K,), jnp.int32),
                        pltpu.VMEM((TOP_K, TOK_TILE * TOK_ROWS, LANE), F32),
                        pltpu.SemaphoreType.DMA, pltpu.SemaphoreType.DMA],
        compiler_params=pltpu.CompilerParams(
            dimension_semantics=("arbitrary",), vmem_limit_bytes=VMEM_LIMIT),
        name="moe_combine_ln",
    )(dest2, route, x1, lng, lnb, yb)


def _routing_tables(top_i):
    m = top_i.size
    flat_e = top_i.reshape(-1)
    onehot = (flat_e[:, None] == jnp.arange(N_EXPERTS, dtype=jnp.int32)[None, :]).astype(jnp.int32)
    oh3 = onehot.reshape(m // LANE, LANE, N_EXPERTS)
    tri = jnp.tril(jnp.ones((LANE, LANE), F32))
    within = jnp.einsum("ij,tjk->tik", tri, oh3.astype(F32)).astype(jnp.int32)
    tile_tot = within[:, -1, :]
    tile_off = jnp.cumsum(tile_tot, axis=0) - tile_tot
    csum = (within + tile_off[:, None, :]).reshape(m, N_EXPERTS)
    rank = jnp.sum(onehot * csum, axis=1) - 1
    counts = csum[-1]
    padded = (counts + MOE_BLK - 1) // MOE_BLK * MOE_BLK
    pad_ends = jnp.cumsum(padded)
    pad_starts = pad_ends - padded
    dest = jnp.sum(onehot * pad_starts[None, :], axis=1) + rank
    nblk = m // MOE_BLK + N_EXPERTS
    blk_start = jnp.arange(nblk, dtype=jnp.int32) * MOE_BLK
    block_e = jnp.minimum(jnp.searchsorted(pad_ends, blk_start, side="right"), N_EXPERTS - 1)
    n_used = (pad_ends[-1] // MOE_BLK).astype(jnp.int32).reshape(1)
    return dest.astype(jnp.int32), block_e.astype(jnp.int32), n_used, nblk


def kernel(x_prompt, x_sample, state_gdn, state_gdn_conv, state_ssm, state_ssm_conv, w_in, gdn_conv_w,
           gdn_a_log, gdn_dt_bias, gdn_norm_w, ssm_conv_w, ssm_conv_b, ssm_a_log, ssm_dt_bias, ssm_d,
           ssm_norm_w, w_br_gdn, w_br_ssm, w_out, ln1_g, ln1_b, router_w, router_b, exp_w_gate,
           exp_b_gate, exp_w_up, exp_b_up, exp_w_down, exp_b_down, ln2_g, ln2_b):
    batch, seq, _ = x_prompt.shape
    dec = x_sample.shape[0]
    depth = w_in.shape[0]
    n_p = batch * seq
    nt = n_p + dec
    alpha = (2.0 * depth) ** 0.25
    gw = SSM_DINNER // SSM_GROUPS
    hpg = SSM_HEADS // SSM_GROUPS
    tm = _pick(nt, (384, 256, 128, 64, 32, 16))
    assert seq % CHUNK == 0 and nt % TOK_TILE == 0 and (nt * TOP_K) % MOE_BLK == 0

    x = jnp.concatenate([x_prompt.reshape(n_p, D_MODEL), x_sample.reshape(dec, D_MODEL)], axis=0)
    x_bf = _bf(x)

    o_zg = GDN_CONV_CH
    o_b = o_zg + GDN_VDIM
    o_a = o_b + GDN_HEADS
    o_zs = o_a + GDN_HEADS
    o_x = o_zs + SSM_DINNER
    o_dt = o_x + SSM_CONV_CH
    o_ga = o_dt + SSM_HEADS

    rows = jnp.arange(SMALL_W, dtype=jnp.int32)[None, :, None]
    lanes = jnp.arange(gw, dtype=jnp.int32)[None, None, :]
    grp = jnp.arange(SSM_GROUPS, dtype=jnp.int32)[:, None, None]
    expand = (rows == COL_DT + grp * hpg + lanes // SSM_HEADDIM).astype(F32)

    outs = {k: [] for k in ("gdn_p", "gconv_p", "gconv_s", "ssm_p", "sconv_p", "sconv_s")}
    ssm_all = state_ssm.reshape(depth, dec, SSM_DINNER, SSM_DSTATE)
    gdn_s = ssm_s = None
    for l in range(depth):
        w = w_in[l]
        w_big = _bf(jnp.concatenate([w[:, :o_b], w[:, o_zs:o_dt], w[:, o_ga:]], axis=1))
        w_small = _bf(jnp.concatenate(
            [w[:, o_b:o_zs], w[:, o_dt:o_ga],
             jnp.zeros((D_MODEL, SMALL_W - 2 * GDN_HEADS - SSM_HEADS), F32)], axis=1))
        zpad = jnp.zeros((SMALL_W - COL_DT - SSM_HEADS,), F32)
        pv = jnp.zeros((SUBLANE, SMALL_W), F32)
        pv = pv.at[0].set(jnp.concatenate([jnp.zeros((COL_A,), F32), gdn_dt_bias[l], ssm_dt_bias[l], zpad]))
        pv = pv.at[1].set(jnp.concatenate([jnp.zeros((COL_A,), F32), gdn_a_log[l], ssm_a_log[l], zpad]))
        d_exp = jnp.repeat(ssm_d[l], SSM_HEADDIM).reshape(1, SSM_DINNER)
        gnw = gdn_norm_w[l].reshape(1, LANE)
        snw = ssm_norm_w[l].reshape(1, SSM_DINNER)
        scb = ssm_conv_b[l].reshape(1, SSM_CONV_CH)

        proj = _matmul(x_bf, w_big, tm, 1024)
        small = _matmul(x_bf, w_small, tm, SMALL_W)

        og_p, s_p = _gdn_prompt(proj, small, gdn_conv_w[l], pv, gnw, batch, seq)
        yz_p, h_p = _ssd_prompt(proj, small, ssm_conv_w[l], scb, pv, expand, d_exp, snw, batch, seq)
        proj_s = lax.slice(proj, (n_p, 0), (nt, W_BIG))
        small_s = lax.slice(small, (n_p, 0), (nt, SMALL_W))
        og_s, yz_s, gdn_s, ssm_s = _sample_mix(l, proj_s, small_s, state_gdn_conv[l], state_ssm_conv[l],
                                               state_gdn, ssm_all, gdn_s, ssm_s, gdn_conv_w[l],
                                               ssm_conv_w[l], scb, pv, gnw, expand, d_exp, snw)
        og = jnp.concatenate([og_p, og_s], axis=0)
        yz = jnp.concatenate([yz_p, yz_s], axis=0)

        rw = jnp.concatenate([router_w[l], jnp.zeros((D_MODEL, LANE - N_EXPERTS), F32)], axis=1)
        rb = jnp.concatenate([router_b[l], jnp.full((LANE - N_EXPERTS,), NEG_BIG, F32)]).reshape(1, LANE)
        x1, x1t, route = _merge(og, yz, proj, x, _bf(w_br_gdn[l]), _bf(w_br_ssm[l]), _bf(w_out[l]),
                                ln1_g[l].reshape(1, D_MODEL), ln1_b[l].reshape(1, D_MODEL), rw, rb, alpha, tm)

        top_i = route[:, TOP_K:2 * TOP_K].astype(jnp.int32)
        dest, block_e, n_used, nblk = _routing_tables(top_i)
        dest2 = dest.reshape(nt // TOK_TILE, TOK_TILE * TOP_K)
        xb = _dispatch(dest2, x1t, jnp.zeros((nblk * MOE_BLK * TOK_ROWS, LANE), F32))
        yb = _experts(block_e, n_used, xb, exp_w_gate[l], exp_w_up[l], exp_w_down[l],
                      exp_b_gate[l], exp_b_up[l], exp_b_down[l])
        x, x_bf = _combine(dest2, route, x1, ln2_g[l].reshape(1, D_MODEL), ln2_b[l].reshape(1, D_MODEL),
                           yb, alpha)

        outs["gdn_p"].append(s_p)
        outs["ssm_p"].append(h_p.reshape(batch, SSM_HEADS, SSM_HEADDIM, SSM_DSTATE))
        tails_g = [lax.slice(proj, (b * seq + seq - (CONV_W - 1), 0), (b * seq + seq, GDN_CONV_CH))
                   for b in range(batch)]
        tails_s = [lax.slice(proj, (b * seq + seq - (CONV_W - 1), o_x - 2 * GDN_HEADS),
                             (b * seq + seq, o_x - 2 * GDN_HEADS + SSM_CONV_CH)) for b in range(batch)]
        outs["gconv_p"].append(jnp.stack(tails_g))
        outs["sconv_p"].append(jnp.stack(tails_s))
        outs["gconv_s"].append(jnp.concatenate(
            [state_gdn_conv[l][:, 1:], proj_s[:, None, :GDN_CONV_CH]], axis=1))
        xbc_off = o_x - 2 * GDN_HEADS
        outs["sconv_s"].append(jnp.concatenate(
            [state_ssm_conv[l][:, 1:], proj_s[:, None, xbc_off:xbc_off + SSM_CONV_CH]], axis=1))

    yp = x[:n_p].reshape(batch, seq, D_MODEL)
    ys = x[n_p:].reshape(dec, 1, D_MODEL)
    return (yp, ys, jnp.stack(outs["gdn_p"]), gdn_s, jnp.stack(outs["gconv_p"]),
            jnp.stack(outs["gconv_s"]), jnp.stack(outs["ssm_p"]),
            ssm_s.reshape(depth, dec, SSM_HEADS, SSM_HEADDIM, SSM_DSTATE),
            jnp.stack(outs["sconv_p"]), jnp.stack(outs["sconv_s"]))
```

```python
import functools

import jax
import jax.numpy as jnp
from jax import lax
from jax.experimental import pallas as pl
from jax.experimental.pallas import tpu as pltpu

F32 = jnp.float32
BF16 = jnp.bfloat16

D_MODEL = 1024
GDN_HEADS = 8
GDN_DK = 128
GDN_VDIM = 1024
GDN_CONV_CH = 3072
SSM_HEADS = 32
SSM_HEADDIM = 64
SSM_GROUPS = 4
SSM_DINNER = 2048
SSM_DSTATE = 128
SSM_CONV_CH = 3072
CONV_W = 4
N_EXPERTS = 32
TOP_K = 4
SWIGLU_ALPHA = 1.702
SWIGLU_LIMIT = 7.0
LN_EPS = 1e-5
RMS_EPS = 1e-6
L2_EPS = 1e-6
NEG_BIG = -1e30

W_BIG = 11264
SMALL_W = 128
COL_B, COL_A, COL_DT = 0, 8, 16

LANE = 128
SUBLANE = 8
CHUNK = 128
GDN_HB = 8
MOE_BLK = 512
TOK_TILE = 128
TOK_ROWS = D_MODEL // LANE
VMEM_LIMIT = 56 * 1024 * 1024


def _pick(n, cands):
    for c in cands:
        if n % c == 0:
            return c
    raise ValueError(f"no tile for {n}")


def _bf(x):
    return x.astype(BF16)


def _dot(a, b, prec=None):
    return jnp.dot(a, b, preferred_element_type=F32, precision=prec)


def _dot_nt(a, b):
    return lax.dot_general(a, b, (((1,), (1,)), ((), ())), preferred_element_type=F32)


def _dot_tn(a, b):
    return lax.dot_general(a, b, (((0,), (0,)), ((), ())), preferred_element_type=F32)


def _split3(x):
    hi = _bf(x)
    r = x - hi.astype(F32)
    mid = _bf(r)
    return hi, mid, _bf(r - mid.astype(F32))


def _dot_sel_rhs(x, sel):
    hi, mid, lo = _split3(x)
    return (_dot(lo, sel) + _dot(mid, sel)) + _dot(hi, sel)


def _dot_sel_lhs(sel, x):
    hi, mid, lo = _split3(x)
    return (_dot(sel, lo) + _dot(sel, mid)) + _dot(sel, hi)


def _sigmoid(x):
    return jax.nn.sigmoid(x)


def _silu(x):
    return x * jax.nn.sigmoid(x)


def _softplus(x):
    return jnp.maximum(x, 0.0) + jnp.log(1.0 + jnp.exp(-jnp.abs(x)))


def _layer_norm(x, g, b):
    mu = jnp.mean(x, axis=-1, keepdims=True)
    xc = x - mu
    var = jnp.mean(xc * xc, axis=-1, keepdims=True)
    return xc * lax.rsqrt(var + LN_EPS) * g + b


def _small_act(raw, pv):
    col = lax.broadcasted_iota(jnp.int32, raw.shape, 1)
    sp = _softplus(raw + pv[0:1, :])
    act = jnp.where(col < COL_A, _sigmoid(raw), sp)
    gda = sp * (-jnp.exp(pv[1:2, :]))
    return act, gda


def _mm_kernel(x_ref, w_ref, o_ref):
    o_ref[...] = _dot(x_ref[...], w_ref[...])


def _matmul(x, w, tm, tn):
    m, k = x.shape
    n = w.shape[1]
    return pl.pallas_call(
        _mm_kernel,
        out_shape=jax.ShapeDtypeStruct((m, n), F32),
        grid=(n // tn, m // tm),
        in_specs=[pl.BlockSpec((tm, k), lambda j, i: (i, 0)),
                  pl.BlockSpec((k, tn), lambda j, i: (0, j))],
        out_specs=pl.BlockSpec((tm, tn), lambda j, i: (i, j)),
        compiler_params=pltpu.CompilerParams(
            dimension_semantics=("parallel", "parallel"), vmem_limit_bytes=VMEM_LIMIT),
        name="in_proj",
    )(x, w)


def _tri_inv_all(mats, ii, jj, c):
    eye = (ii == jj).astype(F32)
    pair = (ii >> 1) == (jj >> 1)
    ts = [eye - jnp.where(pair, a, 0.0) for a in mats]
    s = 1
    while (2 << s) <= c:
        same_outer = (ii >> (s + 1)) == (jj >> (s + 1))
        same_inner = (ii >> s) == (jj >> s)
        off = same_outer & jnp.logical_not(same_inner)
        tbs = [_bf(t) for t in ts]
        tes = [_dot(tb, _bf(jnp.where(off, a, 0.0))) for tb, a in zip(tbs, mats)]
        ts = [t - _dot(_bf(te), tb) for t, te, tb in zip(ts, tes, tbs)]
        s += 1
    return ts


def _gdn_kernel(q_ref, k_ref, v_ref, zg_ref, small_ref, cwq_ref, cwk_ref, cwv_ref, pv_ref, nw_ref,
                o_ref, sfin_ref, xf, tail, s_scr):
    c_len = q_ref.shape[0]
    hg = pl.program_id(1)
    c = pl.program_id(2)
    nc = pl.num_programs(2)

    @pl.when(c == 0)
    def _():
        tail[...] = jnp.zeros_like(tail)
        s_scr[...] = jnp.zeros_like(s_scr)

    for p, r in enumerate((q_ref, k_ref, v_ref)):
        xf[p, 0:SUBLANE, :] = tail[p]
        xf[p, SUBLANE:SUBLANE + c_len, :] = r[...]
        tail[p] = r[c_len - SUBLANE:c_len, :]
    cws = (cwq_ref, cwk_ref, cwv_ref)

    def conv(p, hs):
        acc = None
        for i in range(CONV_W):
            term = cws[p][i:i + 1, hs] * xf[p, pl.ds(SUBLANE - (CONV_W - 1) + i, c_len), hs]
            acc = term if acc is None else acc + term
        return _silu(acc)

    act, gda = _small_act(small_ref[...], pv_ref[...])
    ii = lax.broadcasted_iota(jnp.int32, (c_len, c_len), 0)
    jj = lax.broadcasted_iota(jnp.int32, (c_len, c_len), 1)
    incl = ii >= jj
    strict = ii > jj
    gcum = _dot_sel_lhs(_bf(incl.astype(F32)), gda)
    gcum_t = gcum.T
    col = lax.broadcasted_iota(jnp.int32, (c_len, SMALL_W), 1)
    row_t = lax.broadcasted_iota(jnp.int32, (SMALL_W, c_len), 0)
    nw = nw_ref[...]

    heads = range(GDN_HB)
    hsl = [slice(hh * LANE, (hh + 1) * LANE) for hh in heads]
    betas, gcs, grs = [], [], []
    for hh in heads:
        head = hg * GDN_HB + hh
        betas.append(jnp.sum(jnp.where(col == COL_B + head, act, 0.0), axis=-1, keepdims=True))
        gcs.append(jnp.sum(jnp.where(col == COL_A + head, gcum, 0.0), axis=-1, keepdims=True))
        grs.append(jnp.sum(jnp.where(row_t == COL_A + head, gcum_t, 0.0), axis=0, keepdims=True))
    ks = []
    for hh in heads:
        k = conv(1, hsl[hh])
        ks.append(k * lax.rsqrt(jnp.sum(k * k, axis=-1, keepdims=True) + L2_EPS))
    kbs = [ks[hh] * betas[hh] for hh in heads]
    kbfs = [_bf(k) for k in ks]
    kks = [_dot_nt(_bf(kbs[hh]), kbfs[hh]) for hh in heads]
    gams = [jnp.exp(jnp.where(incl, gcs[hh] - grs[hh], NEG_BIG)) for hh in heads]
    amats = [jnp.where(strict, kks[hh] * gams[hh], 0.0) for hh in heads]
    qs = []
    for hh in heads:
        q = conv(0, hsl[hh])
        qs.append(q * lax.rsqrt(jnp.sum(q * q, axis=-1, keepdims=True) + L2_EPS) * (GDN_DK ** -0.5))
    qks = [_dot_nt(_bf(qs[hh]), kbfs[hh]) * gams[hh] for hh in heads]
    egs = [jnp.exp(gcs[hh]) for hh in heads]
    rhs = [_bf(jnp.concatenate([conv(2, hsl[hh]) * betas[hh], kbs[hh] * egs[hh]], axis=1)) for hh in heads]
    ts = _tri_inv_all(amats, ii, jj, c_len)
    uws = [_dot(_bf(ts[hh]), rhs[hh]) for hh in heads]
    glasts = [gcs[hh][c_len - 1:c_len, :] for hh in heads]
    s_olds = [s_scr[hh] for hh in heads]
    sbs = [_bf(s) for s in s_olds]
    v_news = [uws[hh][:, :LANE] - _dot(_bf(uws[hh][:, LANE:]), sbs[hh]) for hh in heads]
    vnbs = [_bf(v) for v in v_news]
    os_ = [_dot(_bf(qs[hh] * egs[hh]), sbs[hh]) + _dot(_bf(qks[hh]), vnbs[hh]) for hh in heads]
    for hh in heads:
        kdec = ks[hh] * jnp.exp(glasts[hh] - gcs[hh])
        s_scr[hh] = s_olds[hh] * jnp.exp(glasts[hh]) + _dot_tn(_bf(kdec), vnbs[hh])
    for hh in heads:
        o = os_[hh]
        o = (o * lax.rsqrt(jnp.mean(o * o, axis=-1, keepdims=True) + RMS_EPS) * nw
             * _silu(zg_ref[:, hsl[hh]]))
        o_ref[:, hsl[hh]] = _bf(o)

    @pl.when(c == nc - 1)
    def _():
        sfin_ref[0] = s_scr[...]


def _gdn_prompt(proj, small, conv_w, pv, norm_w, batch, seq):
    nc = seq // CHUNK
    hbw = GDN_HB * LANE
    ngrp = GDN_HEADS // GDN_HB
    kq, kk, kv, kz = 0, GDN_VDIM // hbw, 2 * GDN_VDIM // hbw, 3 * GDN_VDIM // hbw

    def row(b, hg, c):
        return b * nc + c

    def pspec(off):
        return pl.BlockSpec((CHUNK, hbw), lambda b, hg, c: (row(b, hg, c), off + hg))

    def wspec(off):
        return pl.BlockSpec((CONV_W, hbw), lambda b, hg, c: (0, off + hg))

    return pl.pallas_call(
        _gdn_kernel,
        out_shape=(jax.ShapeDtypeStruct((batch * seq, GDN_VDIM), BF16),
                   jax.ShapeDtypeStruct((batch, GDN_HEADS, GDN_DK, LANE), F32)),
        grid=(batch, ngrp, nc),
        in_specs=[pspec(kq), pspec(kk), pspec(kv), pspec(kz),
                  pl.BlockSpec((CHUNK, SMALL_W), lambda b, hg, c: (row(b, hg, c), 0)),
                  wspec(kq), wspec(kk), wspec(kv),
                  pl.BlockSpec((SUBLANE, SMALL_W), lambda b, hg, c: (0, 0)),
                  pl.BlockSpec((1, LANE), lambda b, hg, c: (0, 0))],
        out_specs=(pl.BlockSpec((CHUNK, hbw), lambda b, hg, c: (row(b, hg, c), hg)),
                   pl.BlockSpec((1, GDN_HB, GDN_DK, LANE), lambda b, hg, c: (b, hg, 0, 0))),
        scratch_shapes=[pltpu.VMEM((3, CHUNK + SUBLANE, hbw), F32),
                        pltpu.VMEM((3, SUBLANE, hbw), F32),
                        pltpu.VMEM((GDN_HB, GDN_DK, LANE), F32)],
        compiler_params=pltpu.CompilerParams(
            dimension_semantics=("parallel", "parallel", "arbitrary"), vmem_limit_bytes=VMEM_LIMIT),
        name="gdn_prompt",
    )(proj, proj, proj, proj, small, conv_w, conv_w, conv_w, pv, norm_w)


def _ssd_kernel(xs_ref, b_ref, c_ref, zs_ref, small_ref, cwx_ref, cwb_ref, cwc_ref, cbx_ref, cbb_ref,
                cbc_ref, pv_ref, ex_ref, dexp_ref, nw_ref, yz_ref, hfin_ref, xf, tail, ht, ydiag):
    c_len = xs_ref.shape[0]
    gw = xs_ref.shape[1]
    hpg = gw // SSM_HEADDIM
    g = pl.program_id(1)
    c = pl.program_id(2)
    nc = pl.num_programs(2)

    @pl.when(c == 0)
    def _():
        tail[...] = jnp.zeros_like(tail)
        ht[...] = jnp.zeros_like(ht)

    parts = ((xs_ref, cwx_ref, cbx_ref, 0, gw), (b_ref, cwb_ref, cbb_ref, gw, SSM_DSTATE),
             (c_ref, cwc_ref, cbc_ref, gw + SSM_DSTATE, SSM_DSTATE))
    convs = []
    for r, cw, cb, off, wd in parts:
        sl = slice(off, off + wd)
        xf[0:SUBLANE, sl] = tail[:, sl]
        xf[SUBLANE:SUBLANE + c_len, sl] = r[...]
        tail[:, sl] = r[c_len - SUBLANE:c_len, :]
        acc = cb[...]
        for i in range(CONV_W):
            acc = acc + cw[i:i + 1, :] * xf[pl.ds(SUBLANE - (CONV_W - 1) + i, c_len), sl]
        convs.append(_silu(acc))
    xs, bm, cm = convs

    act, gda = _small_act(small_ref[...], pv_ref[...])
    ii = lax.broadcasted_iota(jnp.int32, (c_len, c_len), 0)
    jj = lax.broadcasted_iota(jnp.int32, (c_len, c_len), 1)
    incl = ii >= jj
    acs = _dot_sel_lhs(_bf(incl.astype(F32)), gda)
    acs_t = acs.T
    ex = ex_ref[...]
    dt_x = _dot_sel_rhs(act, ex)
    acs_x = _dot_sel_rhs(acs, ex)
    last = acs_x[c_len - 1:c_len, :]
    xdt = xs * dt_x
    cb_m = _dot_nt(_bf(cm), _bf(bm))
    h_old = ht[...]
    y_off = _dot(_bf(cm), _bf(h_old)) * jnp.exp(acs_x)
    col = lax.broadcasted_iota(jnp.int32, (c_len, SMALL_W), 1)
    row_t = lax.broadcasted_iota(jnp.int32, (SMALL_W, c_len), 0)
    lane = lax.broadcasted_iota(jnp.int32, (c_len, LANE), 1)
    for pr in range(hpg // 2):
        ps = slice(pr * LANE, (pr + 1) * LANE)
        xpair = xdt[:, ps]
        acc = None
        for half in range(2):
            head = g * hpg + pr * 2 + half
            ac = jnp.sum(jnp.where(col == COL_DT + head, acs, 0.0), axis=-1, keepdims=True)
            ar = jnp.sum(jnp.where(row_t == COL_DT + head, acs_t, 0.0), axis=0, keepdims=True)
            sc = cb_m * jnp.exp(jnp.where(incl, ac - ar, NEG_BIG))
            in_half = (lane >= half * SSM_HEADDIM) & (lane < (half + 1) * SSM_HEADDIM)
            term = _dot(_bf(sc), _bf(jnp.where(in_half, xpair, 0.0)))
            acc = term if acc is None else acc + term
        ydiag[:, ps] = acc
    ht[...] = h_old * jnp.exp(last) + _dot_tn(_bf(bm), _bf(xdt * jnp.exp(last - acs_x)))
    y = ydiag[...] + y_off + dexp_ref[...] * xs
    yz = y * _silu(zs_ref[...])
    yz = yz * lax.rsqrt(jnp.mean(yz * yz, axis=-1, keepdims=True) + RMS_EPS) * nw_ref[...]
    yz_ref[...] = _bf(yz)

    @pl.when(c == nc - 1)
    def _():
        hfin_ref[0] = ht[...].T


def _ssd_prompt(proj, small, conv_w, conv_b, pv, expand, d_exp, norm_w, batch, seq):
    nc = seq // CHUNK
    gw = SSM_DINNER // SSM_GROUPS
    x_off = (GDN_CONV_CH + GDN_VDIM + SSM_DINNER)
    z_off = GDN_CONV_CH + GDN_VDIM
    kx = x_off // gw
    kb = (x_off + SSM_DINNER) // SSM_DSTATE
    kc = kb + SSM_GROUPS
    kz = z_off // gw
    wb = SSM_DINNER // SSM_DSTATE
    wc = wb + SSM_GROUPS

    def row(b, g, c):
        return b * nc + c

    in_specs = [
        pl.BlockSpec((CHUNK, gw), lambda b, g, c: (row(b, g, c), kx + g)),
        pl.BlockSpec((CHUNK, SSM_DSTATE), lambda b, g, c: (row(b, g, c), kb + g)),
        pl.BlockSpec((CHUNK, SSM_DSTATE), lambda b, g, c: (row(b, g, c), kc + g)),
        pl.BlockSpec((CHUNK, gw), lambda b, g, c: (row(b, g, c), kz + g)),
        pl.BlockSpec((CHUNK, SMALL_W), lambda b, g, c: (row(b, g, c), 0)),
        pl.BlockSpec((CONV_W, gw), lambda b, g, c: (0, g)),
        pl.BlockSpec((CONV_W, SSM_DSTATE), lambda b, g, c: (0, wb + g)),
        pl.BlockSpec((CONV_W, SSM_DSTATE), lambda b, g, c: (0, wc + g)),
        pl.BlockSpec((1, gw), lambda b, g, c: (0, g)),
        pl.BlockSpec((1, SSM_DSTATE), lambda b, g, c: (0, wb + g)),
        pl.BlockSpec((1, SSM_DSTATE), lambda b, g, c: (0, wc + g)),
        pl.BlockSpec((SUBLANE, SMALL_W), lambda b, g, c: (0, 0)),
        pl.BlockSpec((None, SMALL_W, gw), lambda b, g, c: (g, 0, 0)),
        pl.BlockSpec((1, gw), lambda b, g, c: (0, g)),
        pl.BlockSpec((1, gw), lambda b, g, c: (0, g)),
    ]
    return pl.pallas_call(
        _ssd_kernel,
        out_shape=(jax.ShapeDtypeStruct((batch * seq, SSM_DINNER), BF16),
                   jax.ShapeDtypeStruct((batch, SSM_DINNER, SSM_DSTATE), F32)),
        grid=(batch, SSM_GROUPS, nc),
        in_specs=in_specs,
        out_specs=(pl.BlockSpec((CHUNK, gw), lambda b, g, c: (row(b, g, c), g)),
                   pl.BlockSpec((1, gw, SSM_DSTATE), lambda b, g, c: (b, g, 0))),
        scratch_shapes=[pltpu.VMEM((CHUNK + SUBLANE, gw + 2 * SSM_DSTATE), F32),
                        pltpu.VMEM((SUBLANE, gw + 2 * SSM_DSTATE), F32),
                        pltpu.VMEM((SSM_DSTATE, gw), F32),
                        pltpu.VMEM((CHUNK, gw), F32)],
        compiler_params=pltpu.CompilerParams(
            dimension_semantics=("parallel", "parallel", "arbitrary"), vmem_limit_bytes=VMEM_LIMIT),
        name="ssd_prompt",
    )(proj, proj, proj, proj, small, conv_w, conv_w, conv_w, conv_b, conv_b, conv_b, pv, expand,
      d_exp, norm_w)


def _sample_prep_kernel(small_ref, pv_ref, ex_ref, act_ref, gda_ref, dtx_ref, dax_ref):
    act, gda = _small_act(small_ref[...], pv_ref[...])
    act_ref[...] = act
    gda_ref[...] = gda
    ex = ex_ref[...]
    dtx_ref[...] = _dot_sel_rhs(act, ex)
    dax_ref[...] = _dot_sel_rhs(gda, ex)


def _sample_prep(small_s, pv, expand_all):
    n = small_s.shape[0]
    return pl.pallas_call(
        _sample_prep_kernel,
        out_shape=(jax.ShapeDtypeStruct((n, SMALL_W), F32), jax.ShapeDtypeStruct((n, SMALL_W), F32),
                   jax.ShapeDtypeStruct((n, SSM_DINNER), F32), jax.ShapeDtypeStruct((n, SSM_DINNER), F32)),
        name="sample_prep",
    )(small_s, pv, expand_all)


def _sample_kernel(proj_ref, act_ref, gda_ref, dtx_ref, dax_ref, gcs_ref, scs_ref, s0_ref, h0_ref, gcw_ref,
                   scw_ref, scb_ref, gnw_ref, dexp_ref, snw_ref, *rest, n_alias):
    og_ref, yz_ref, s_ref, h_ref, stk, stk2 = rest[n_alias:]
    if n_alias == 0:
        if s_ref.shape[0] > 1:
            s_ref[1:] = jnp.zeros((s_ref.shape[0] - 1,) + s_ref.shape[1:], F32)
            h_ref[1:] = jnp.zeros((h_ref.shape[0] - 1,) + h_ref.shape[1:], F32)
        s_ref = s_ref.at[0]
        h_ref = h_ref.at[0]
    gw = SSM_DINNER // SSM_GROUPS
    z_off = GDN_CONV_CH + GDN_VDIM
    x_off = z_off + SSM_DINNER

    def conv1(state_ref, w_ref, new_row):
        acc = w_ref[CONV_W - 1:CONV_W, :] * new_row
        for i in range(CONV_W - 1):
            acc = acc + w_ref[i:i + 1, :] * state_ref[i:i + 1, :]
        return acc

    act = act_ref[...]
    gda = gda_ref[...]

    qkv = _silu(conv1(gcs_ref, gcw_ref, proj_ref[:, 0:GDN_CONV_CH]))
    stk[...] = jnp.zeros_like(stk)
    qs, vs = [], []
    for h in range(GDN_HEADS):
        q = qkv[:, h * LANE:(h + 1) * LANE]
        k = qkv[:, GDN_VDIM + h * LANE:GDN_VDIM + (h + 1) * LANE]
        q = q * lax.rsqrt(jnp.sum(q * q, axis=-1, keepdims=True) + L2_EPS) * (GDN_DK ** -0.5)
        k = k * lax.rsqrt(jnp.sum(k * k, axis=-1, keepdims=True) + L2_EPS)
        stk[h:h + 1, :] = k
        stk[GDN_HEADS + h:GDN_HEADS + h + 1, :] = q
        vs.append(qkv[:, 2 * GDN_VDIM + h * LANE:2 * GDN_VDIM + (h + 1) * LANE])
    cols = stk[...].T
    gnw = gnw_ref[...]
    for h in range(GDN_HEADS):
        kc = cols[:, h:h + 1]
        qc = cols[:, GDN_HEADS + h:GDN_HEADS + h + 1]
        beta = act[:, COL_B + h:COL_B + h + 1]
        gh = gda[:, COL_A + h:COL_A + h + 1]
        sd = s0_ref[h] * jnp.exp(gh)
        v_old = jnp.sum(sd * kc, axis=0, keepdims=True)
        delta = (vs[h] - v_old) * beta
        s_new = sd + kc * delta
        s_ref[h] = s_new
        o = jnp.sum(s_new * qc, axis=0, keepdims=True)
        zg = proj_ref[:, GDN_CONV_CH + h * LANE:GDN_CONV_CH + (h + 1) * LANE]
        o = o * lax.rsqrt(jnp.mean(o * o, axis=-1, keepdims=True) + RMS_EPS) * gnw * _silu(zg)
        og_ref[:, h * LANE:(h + 1) * LANE] = _bf(o)

    xbc = _silu(conv1(scs_ref, scw_ref, proj_ref[:, x_off:x_off + SSM_CONV_CH]) + scb_ref[...])
    dexp = dexp_ref[...]
    stk2[...] = jnp.zeros_like(stk2)
    for g in range(SSM_GROUPS):
        gs = slice(g * gw, (g + 1) * gw)
        dt_x = dtx_ref[:, gs]
        da_x = dax_ref[:, gs]
        xs = xbc[:, gs]
        bm = xbc[:, SSM_DINNER + g * SSM_DSTATE:SSM_DINNER + (g + 1) * SSM_DSTATE]
        cm = xbc[:, SSM_DINNER + SSM_GROUPS * SSM_DSTATE + g * SSM_DSTATE:
                 SSM_DINNER + SSM_GROUPS * SSM_DSTATE + (g + 1) * SSM_DSTATE]
        stk2[0:1, :] = jnp.exp(da_x)
        stk2[1:2, :] = xs * dt_x
        cols2 = stk2[...].T
        h_new = h0_ref[gs, :] * cols2[:, 0:1] + cols2[:, 1:2] * bm
        h_ref[gs, :] = h_new
        cm16 = jnp.broadcast_to(cm, (2 * SUBLANE, SSM_DSTATE))
        y = _dot_nt(_bf(cm16), _bf(h_new))[0:1, :] + dexp[:, gs] * xs
        yz = y * _silu(proj_ref[:, z_off + g * gw:z_off + (g + 1) * gw])
        yz = yz * lax.rsqrt(jnp.mean(yz * yz, axis=-1, keepdims=True) + RMS_EPS) * snw_ref[:, gs]
        yz_ref[:, gs] = _bf(yz)


def _sample_mix(layer, proj_s, small_s, gconv_state, sconv_state, s_all, h_all, s_prev, h_prev, gconv_w,
                sconv_w, sconv_b, pv, gnorm_w, expand_all, d_exp, snorm_w):
    n = proj_s.shape[0]
    gw = SSM_DINNER // SSM_GROUPS
    proj3 = proj_s.reshape(n, 1, W_BIG)
    act, gda, dtx, dax = _sample_prep(small_s, pv, expand_all)

    def full(shape):
        nd = len(shape)
        return pl.BlockSpec(shape, lambda i: (0,) * nd)

    def per_seq(shape):
        nd = len(shape)
        return pl.BlockSpec((None,) + shape, lambda i: (i,) + (0,) * nd)

    def per_layer_seq(shape):
        nd = len(shape)
        return pl.BlockSpec((None, None) + shape, lambda i: (layer, i) + (0,) * nd)

    in_specs = [per_seq((1, W_BIG)), per_seq((1, SMALL_W)), per_seq((1, SMALL_W)),
                per_seq((1, SSM_DINNER)), per_seq((1, SSM_DINNER)),
                per_seq((CONV_W - 1, GDN_CONV_CH)), per_seq((CONV_W - 1, SSM_CONV_CH)),
                per_layer_seq((GDN_HEADS, GDN_DK, LANE)), per_layer_seq((SSM_DINNER, SSM_DSTATE)),
                full((CONV_W, GDN_CONV_CH)), full((CONV_W, SSM_CONV_CH)), full((1, SSM_CONV_CH)),
                full((1, LANE)), full((1, SSM_DINNER)), full((1, SSM_DINNER))]
    args = [proj3, act.reshape(n, 1, SMALL_W), gda.reshape(n, 1, SMALL_W), dtx.reshape(n, 1, SSM_DINNER),
            dax.reshape(n, 1, SSM_DINNER), gconv_state, sconv_state, s_all, h_all, gconv_w, sconv_w, sconv_b,
            gnorm_w, d_exp, snorm_w]
    aliases = {}
    if s_prev is not None:
        aliases = {len(args): 2, len(args) + 1: 3}
        in_specs += [pl.BlockSpec(memory_space=pl.ANY), pl.BlockSpec(memory_space=pl.ANY)]
        args += [s_prev, h_prev]
    if s_prev is None:
        depth = s_all.shape[0]
        state_specs = [pl.BlockSpec((depth, None, GDN_HEADS, GDN_DK, LANE), lambda i: (0, i, 0, 0, 0)),
                       pl.BlockSpec((depth, None, SSM_DINNER, SSM_DSTATE), lambda i: (0, i, 0, 0))]
    else:
        state_specs = [per_layer_seq((GDN_HEADS, GDN_DK, LANE)), per_layer_seq((SSM_DINNER, SSM_DSTATE))]
    kern = functools.partial(_sample_kernel, n_alias=len(aliases))
    og, yz, s_new, h_new = pl.pallas_call(
        kern,
        out_shape=(jax.ShapeDtypeStruct((n, 1, GDN_VDIM), BF16),
                   jax.ShapeDtypeStruct((n, 1, SSM_DINNER), BF16),
                   jax.ShapeDtypeStruct(s_all.shape, F32),
                   jax.ShapeDtypeStruct(h_all.shape, F32)),
        grid=(n,),
        in_specs=in_specs,
        out_specs=(per_seq((1, GDN_VDIM)), per_seq((1, SSM_DINNER)), state_specs[0], state_specs[1]),
        scratch_shapes=[pltpu.VMEM((LANE, LANE), F32), pltpu.VMEM((LANE, gw), F32)],
        input_output_aliases=aliases,
        compiler_params=pltpu.CompilerParams(
            dimension_semantics=("parallel",), vmem_limit_bytes=VMEM_LIMIT),
        name="sample_mix",
    )(*args)
    return og.reshape(n, GDN_VDIM), yz.reshape(n, SSM_DINNER), s_new, h_new


def _merge_kernel(og_ref, yz_ref, ga_ref, gb_ref, x_ref, wbg_ref, wbs_ref, wout_ref, lng_ref, lnb_ref,
                  rwh_ref, rwl_ref, rb_ref, x1_ref, x1t_ref, route_ref, *, alpha):
    a = _dot(og_ref[...], wbg_ref[...])
    b = _dot(yz_ref[...], wbs_ref[...])
    merged = _sigmoid(ga_ref[...]) * a + _sigmoid(gb_ref[...]) * b
    mix = _dot(_bf(merged), wout_ref[...])
    x1 = _layer_norm(alpha * x_ref[...] + mix, lng_ref[...], lnb_ref[...])
    x1_ref[...] = x1
    _store_token_tiles(x1t_ref, x1)

    xh = _bf(x1)
    xl = _bf(x1 - xh.astype(F32))
    rwh = rwh_ref[...]
    lg = (_dot(xh, rwl_ref[...]) + _dot(xl, rwh)) + _dot(xh, rwh) + rb_ref[...]
    colf = lax.broadcasted_iota(jnp.int32, lg.shape, 1).astype(F32)
    vals, idxs = [], []
    for _ in range(TOP_K):
        m = jnp.max(lg, axis=-1, keepdims=True)
        idx = jnp.min(jnp.where(lg == m, colf, float(LANE)), axis=-1, keepdims=True)
        vals.append(m)
        idxs.append(idx)
        lg = jnp.where(colf == idx, 2.0 * NEG_BIG, lg)
    es = [jnp.exp(v - vals[0]) for v in vals]
    den = es[0] + es[1] + es[2] + es[3]
    route = jnp.zeros_like(lg)
    for kk in range(TOP_K):
        route = jnp.where(colf == float(kk), es[kk] / den, route)
        route = jnp.where(colf == float(TOP_K + kk), idxs[kk], route)
    route_ref[...] = route


def _merge(og, yz, proj, x, wbg, wbs, wout, lng, lnb, rw, rb, alpha, tm):
    nt = x.shape[0]
    ka = (W_BIG - 2 * D_MODEL) // D_MODEL
    kern = functools.partial(_merge_kernel, alpha=alpha)
    rw_hi = _bf(rw)

    def full(shape):
        return pl.BlockSpec(shape, lambda i: (0, 0))

    return pl.pallas_call(
        kern,
        out_shape=(jax.ShapeDtypeStruct((nt, D_MODEL), F32),
                   jax.ShapeDtypeStruct((nt * TOK_ROWS, LANE), F32),
                   jax.ShapeDtypeStruct((nt, LANE), F32)),
        grid=(nt // tm,),
        in_specs=[pl.BlockSpec((tm, GDN_VDIM), lambda i: (i, 0)),
                  pl.BlockSpec((tm, SSM_DINNER), lambda i: (i, 0)),
                  pl.BlockSpec((tm, D_MODEL), lambda i: (i, ka)),
                  pl.BlockSpec((tm, D_MODEL), lambda i: (i, ka + 1)),
                  pl.BlockSpec((tm, D_MODEL), lambda i: (i, 0)),
                  full((GDN_VDIM, D_MODEL)), full((SSM_DINNER, D_MODEL)), full((D_MODEL, D_MODEL)),
                  full((1, D_MODEL)), full((1, D_MODEL)), full((D_MODEL, LANE)), full((D_MODEL, LANE)),
                  full((1, LANE))],
        out_specs=(pl.BlockSpec((tm, D_MODEL), lambda i: (i, 0)),
                   pl.BlockSpec((tm * TOK_ROWS, LANE), lambda i: (i, 0)),
                   pl.BlockSpec((tm, LANE), lambda i: (i, 0))),
        compiler_params=pltpu.CompilerParams(
            dimension_semantics=("parallel",), vmem_limit_bytes=VMEM_LIMIT),
        name="merge_ln_router",
    )(og, yz, proj, proj, x, wbg, wbs, wout, lng, lnb, rw_hi, _bf(rw - rw_hi.astype(F32)), rb)


def _store_token_tiles(ref, val):
    n = val.shape[0]
    for cc in range(TOK_ROWS):
        ref[pl.ds(cc, n, stride=TOK_ROWS), :] = val[:, cc * LANE:(cc + 1) * LANE]


def _load_token_chunk(ref, cc, n):
    return ref[pl.ds(cc, n, stride=TOK_ROWS), :]


def _token_rows(t):
    return pl.ds(pl.multiple_of(t * TOK_ROWS, TOK_ROWS), TOK_ROWS)


def _dispatch_kernel(dest_hbm, x_ref, xb_in, xb_out, idx, sem_idx, sem):
    del xb_in
    i = pl.program_id(0)
    cp = pltpu.make_async_copy(dest_hbm.at[i], idx, sem_idx)
    cp.start()
    cp.wait()

    def issue(r, carry):
        for kk in range(TOP_K):
            pltpu.make_async_copy(x_ref.at[_token_rows(r), :],
                                  xb_out.at[_token_rows(idx[r * TOP_K + kk]), :], sem).start(priority=kk % 2)
        return carry

    lax.fori_loop(0, TOK_TILE, issue, 0, unroll=8)
    for _ in range(TOP_K):
        pltpu.make_async_copy(x_ref, xb_out.at[pl.ds(0, TOK_TILE * TOK_ROWS), :], sem).wait()


def _dispatch(dest2, x1t, xb_zero):
    n_tiles = dest2.shape[0]
    return pl.pallas_call(
        _dispatch_kernel,
        out_shape=jax.ShapeDtypeStruct(xb_zero.shape, F32),
        grid=(n_tiles,),
        in_specs=[pl.BlockSpec(memory_space=pl.ANY),
                  pl.BlockSpec((TOK_TILE * TOK_ROWS, LANE), lambda i: (i, 0)),
                  pl.BlockSpec(memory_space=pl.ANY)],
        out_specs=pl.BlockSpec(memory_space=pl.ANY),
        scratch_shapes=[pltpu.SMEM((TOK_TILE * TOP_K,), jnp.int32),
                        pltpu.SemaphoreType.DMA, pltpu.SemaphoreType.DMA],
        input_output_aliases={2: 0},
        compiler_params=pltpu.CompilerParams(
            dimension_semantics=("arbitrary",), vmem_limit_bytes=VMEM_LIMIT),
        name="moe_dispatch",
    )(dest2, x1t, xb_zero)


def _expert_kernel(be_ref, nu_ref, x_ref, wg_ref, wu_ref, wd_ref, bg_ref, bu_ref, bd_ref, y_ref,
                   wgb, wub, wdb):
    j = pl.program_id(0)
    e = be_ref[j]
    prev = be_ref[jnp.maximum(j - 1, 0)]
    used = j < nu_ref[0]

    @pl.when(used & ((j == 0) | (e != prev)))
    def _():
        wgb[...] = _bf(wg_ref[...])
        wub[...] = _bf(wu_ref[...])
        wdb[...] = _bf(wd_ref[...])

    @pl.when(used)
    def _():
        x = jnp.concatenate([_bf(_load_token_chunk(x_ref, cc, MOE_BLK)) for cc in range(TOK_ROWS)], axis=1)
        gt = _dot(x, wgb[...]) + bg_ref[...]
        up = _dot(x, wub[...]) + bu_ref[...]
        gt = jnp.minimum(gt, SWIGLU_LIMIT)
        up = jnp.clip(up, -SWIGLU_LIMIT, SWIGLU_LIMIT)
        h = (up + 1.0) * (gt * _sigmoid(SWIGLU_ALPHA * gt))
        _store_token_tiles(y_ref, _dot(_bf(h), wdb[...]) + bd_ref[...])

    @pl.when(jnp.logical_not(used))
    def _():
        y_ref[...] = jnp.zeros_like(y_ref)


def _experts(layer, block_e, n_used, xb, wg, wu, wd, bg, bu, bd):
    rows = xb.shape[0]
    blk_rows = MOE_BLK * TOK_ROWS
    nblk = rows // blk_rows
    depth = wg.shape[0]
    d_e = wg.shape[-1]
    wspec_in = pl.BlockSpec((None, None, D_MODEL, d_e), lambda j, be, nu: (layer, be[j], 0, 0))
    wspec_out = pl.BlockSpec((None, None, d_e, D_MODEL), lambda j, be, nu: (layer, be[j], 0, 0))
    bspec_e = pl.BlockSpec((None, None, 1, d_e), lambda j, be, nu: (layer, be[j], 0, 0))
    bspec_d = pl.BlockSpec((None, None, 1, D_MODEL), lambda j, be, nu: (layer, be[j], 0, 0))
    grid_spec = pltpu.PrefetchScalarGridSpec(
        num_scalar_prefetch=2,
        grid=(nblk,),
        in_specs=[pl.BlockSpec((blk_rows, LANE), lambda j, be, nu: (j, 0)),
                  wspec_in, wspec_in, wspec_out, bspec_e, bspec_e, bspec_d],
        out_specs=pl.BlockSpec((blk_rows, LANE), lambda j, be, nu: (j, 0)),
        scratch_shapes=[pltpu.VMEM((D_MODEL, d_e), BF16), pltpu.VMEM((D_MODEL, d_e), BF16),
                        pltpu.VMEM((d_e, D_MODEL), BF16)],
    )
    return pl.pallas_call(
        _expert_kernel,
        out_shape=jax.ShapeDtypeStruct((rows, LANE), F32),
        grid_spec=grid_spec,
        compiler_params=pltpu.CompilerParams(
            dimension_semantics=("arbitrary",), vmem_limit_bytes=VMEM_LIMIT),
        name="moe_experts",
    )(block_e, n_used, xb, wg, wu, wd, bg.reshape(depth, N_EXPERTS, 1, d_e),
      bu.reshape(depth, N_EXPERTS, 1, d_e), bd.reshape(depth, N_EXPERTS, 1, D_MODEL))


def _combine_kernel(dest_hbm, gates_ref, x1_ref, lng_ref, lnb_ref, yb_hbm, y_ref, ybf_ref, idx, buf,
                    sem_idx, sem, *, alpha):
    i = pl.program_id(0)
    cp = pltpu.make_async_copy(dest_hbm.at[i], idx, sem_idx)
    cp.start()
    cp.wait()

    def issue(r, carry):
        for kk in range(TOP_K):
            pltpu.make_async_copy(yb_hbm.at[_token_rows(idx[r * TOP_K + kk]), :],
                                  buf.at[kk, _token_rows(r), :], sem).start(priority=kk % 2)
        return carry

    lax.fori_loop(0, TOK_TILE, issue, 0, unroll=8)
    for kk in range(TOP_K):
        pltpu.make_async_copy(yb_hbm.at[pl.ds(0, TOK_TILE * TOK_ROWS), :], buf.at[kk], sem).wait()

    gates = gates_ref[...]
    chunks = []
    for cc in range(TOK_ROWS):
        acc = gates[:, 0:1] * _load_token_chunk(buf.at[0], cc, TOK_TILE)
        for kk in range(1, TOP_K):
            acc = acc + gates[:, kk:kk + 1] * _load_token_chunk(buf.at[kk], cc, TOK_TILE)
        chunks.append(acc)
    moe = jnp.concatenate(chunks, axis=1)
    y = _layer_norm(alpha * x1_ref[...] + moe, lng_ref[...], lnb_ref[...])
    y_ref[...] = y
    ybf_ref[...] = _bf(y)


def _combine(dest2, route, x1, lng, lnb, yb, alpha):
    nt = x1.shape[0]
    kern = functools.partial(_combine_kernel, alpha=alpha)
    return pl.pallas_call(
        kern,
        out_shape=(jax.ShapeDtypeStruct((nt, D_MODEL), F32),
                   jax.ShapeDtypeStruct((nt, D_MODEL), BF16)),
        grid=(nt // TOK_TILE,),
        in_specs=[pl.BlockSpec(memory_space=pl.ANY),
                  pl.BlockSpec((TOK_TILE, LANE), lambda i: (i, 0)),
                  pl.BlockSpec((TOK_TILE, D_MODEL), lambda i: (i, 0)),
                  pl.BlockSpec((1, D_MODEL), lambda i: (0, 0)),
                  pl.BlockSpec((1, D_MODEL), lambda i: (0, 0)),
                  pl.BlockSpec(memory_space=pl.ANY)],
        out_specs=(pl.BlockSpec((TOK_TILE, D_MODEL), lambda i: (i, 0)),
                   pl.BlockSpec((TOK_TILE, D_MODEL), lambda i: (i, 0))),
        scratch_shapes=[pltpu.SMEM((TOK_TILE * TOP_K,), jnp.int32),
                        pltpu.VMEM((TOP_K, TOK_TILE * TOK_ROWS, LANE), F32),
                        pltpu.SemaphoreType.DMA, pltpu.SemaphoreType.DMA],
        compiler_params=pltpu.CompilerParams(
            dimension_semantics=("arbitrary",), vmem_limit_bytes=VMEM_LIMIT),
        name="moe_combine_ln",
    )(dest2, route, x1, lng, lnb, yb)


def _routing_tables(top_i):
    m = top_i.size
    flat_e = top_i.reshape(-1)
    onehot = (flat_e[:, None] == jnp.arange(N_EXPERTS, dtype=jnp.int32)[None, :]).astype(jnp.int32)
    oh3 = onehot.reshape(m // LANE, LANE, N_EXPERTS)
    tri = jnp.tril(jnp.ones((LANE, LANE), F32))
    within = jnp.einsum("ij,tjk->tik", tri, oh3.astype(F32)).astype(jnp.int32)
    tile_tot = within[:, -1, :]
    tile_off = jnp.cumsum(tile_tot, axis=0) - tile_tot
    csum = (within + tile_off[:, None, :]).reshape(m, N_EXPERTS)
    rank = jnp.sum(onehot * csum, axis=1) - 1
    counts = csum[-1]
    padded = (counts + MOE_BLK - 1) // MOE_BLK * MOE_BLK
    pad_ends = jnp.cumsum(padded)
    pad_starts = pad_ends - padded
    dest = jnp.sum(onehot * pad_starts[None, :], axis=1) + rank
    nblk = m // MOE_BLK + N_EXPERTS
    blk_start = jnp.arange(nblk, dtype=jnp.int32) * MOE_BLK
    block_e = jnp.minimum(jnp.sum((blk_start[:, None] >= pad_ends[None, :]).astype(jnp.int32), axis=1),
                          N_EXPERTS - 1)
    n_used = (pad_ends[-1] // MOE_BLK).astype(jnp.int32).reshape(1)
    return dest.astype(jnp.int32), block_e.astype(jnp.int32), n_used, nblk


def kernel(x_prompt, x_sample, state_gdn, state_gdn_conv, state_ssm, state_ssm_conv, w_in, gdn_conv_w,
           gdn_a_log, gdn_dt_bias, gdn_norm_w, ssm_conv_w, ssm_conv_b, ssm_a_log, ssm_dt_bias, ssm_d,
           ssm_norm_w, w_br_gdn, w_br_ssm, w_out, ln1_g, ln1_b, router_w, router_b, exp_w_gate,
           exp_b_gate, exp_w_up, exp_b_up, exp_w_down, exp_b_down, ln2_g, ln2_b):
    batch, seq, _ = x_prompt.shape
    dec = x_sample.shape[0]
    depth = w_in.shape[0]
    n_p = batch * seq
    nt = n_p + dec
    alpha = (2.0 * depth) ** 0.25
    gw = SSM_DINNER // SSM_GROUPS
    tm = _pick(nt, (384, 256, 128, 64, 32, 16))
    assert seq % CHUNK == 0 and nt % TOK_TILE == 0 and (nt * TOP_K) % MOE_BLK == 0

    x = jnp.concatenate([x_prompt.reshape(n_p, D_MODEL), x_sample.reshape(dec, D_MODEL)], axis=0)
    x_bf = _bf(x)

    o_zg = GDN_CONV_CH
    o_b = o_zg + GDN_VDIM
    o_a = o_b + GDN_HEADS
    o_zs = o_a + GDN_HEADS
    o_x = o_zs + SSM_DINNER
    o_dt = o_x + SSM_CONV_CH
    o_ga = o_dt + SSM_HEADS

    rows = jnp.arange(SMALL_W, dtype=jnp.int32)[:, None]
    lanes = jnp.arange(SSM_DINNER, dtype=jnp.int32)[None, :]
    expand_all = (rows == COL_DT + lanes // SSM_HEADDIM).astype(BF16)
    expand = expand_all.reshape(SMALL_W, SSM_GROUPS, gw).transpose(1, 0, 2)

    outs = {k: [] for k in ("gdn_p", "gconv_p", "gconv_s", "ssm_p", "sconv_p", "sconv_s")}
    ssm_all = state_ssm.reshape(depth, dec, SSM_DINNER, SSM_DSTATE)
    gdn_s = ssm_s = None
    for l in range(depth):
        w = w_in[l]
        w_big = _bf(jnp.concatenate([w[:, :o_b], w[:, o_zs:o_dt], w[:, o_ga:]], axis=1))
        w_small = _bf(jnp.concatenate(
            [w[:, o_b:o_zs], w[:, o_dt:o_ga],
             jnp.zeros((D_MODEL, SMALL_W - 2 * GDN_HEADS - SSM_HEADS), F32)], axis=1))
        zpad = jnp.zeros((SMALL_W - COL_DT - SSM_HEADS,), F32)
        pv = jnp.zeros((SUBLANE, SMALL_W), F32)
        pv = pv.at[0].set(jnp.concatenate([jnp.zeros((COL_A,), F32), gdn_dt_bias[l], ssm_dt_bias[l], zpad]))
        pv = pv.at[1].set(jnp.concatenate([jnp.zeros((COL_A,), F32), gdn_a_log[l], ssm_a_log[l], zpad]))
        d_exp = jnp.repeat(ssm_d[l], SSM_HEADDIM).reshape(1, SSM_DINNER)
        gnw = gdn_norm_w[l].reshape(1, LANE)
        snw = ssm_norm_w[l].reshape(1, SSM_DINNER)
        scb = ssm_conv_b[l].reshape(1, SSM_CONV_CH)

        proj = _matmul(x_bf, w_big, tm, W_BIG // 4)
        small = _matmul(x_bf, w_small, tm, SMALL_W)

        og_p, s_p = _gdn_prompt(proj, small, gdn_conv_w[l], pv, gnw, batch, seq)
        yz_p, h_p = _ssd_prompt(proj, small, ssm_conv_w[l], scb, pv, expand, d_exp, snw, batch, seq)
        proj_s = lax.slice(proj, (n_p, 0), (nt, W_BIG))
        small_s = lax.slice(small, (n_p, 0), (nt, SMALL_W))
        og_s, yz_s, gdn_s, ssm_s = _sample_mix(l, proj_s, small_s, state_gdn_conv[l], state_ssm_conv[l],
                                               state_gdn, ssm_all, gdn_s, ssm_s, gdn_conv_w[l],
                                               ssm_conv_w[l], scb, pv, gnw, expand_all, d_exp, snw)
        og = jnp.concatenate([og_p, og_s], axis=0)
        yz = jnp.concatenate([yz_p, yz_s], axis=0)

        rw = jnp.concatenate([router_w[l], jnp.zeros((D_MODEL, LANE - N_EXPERTS), F32)], axis=1)
        rb = jnp.concatenate([router_b[l], jnp.full((LANE - N_EXPERTS,), NEG_BIG, F32)]).reshape(1, LANE)
        x1, x1t, route = _merge(og, yz, proj, x, _bf(w_br_gdn[l]), _bf(w_br_ssm[l]), _bf(w_out[l]),
                                ln1_g[l].reshape(1, D_MODEL), ln1_b[l].reshape(1, D_MODEL), rw, rb, alpha, tm)

        top_i = route[:, TOP_K:2 * TOP_K].astype(jnp.int32)
        dest, block_e, n_used, nblk = _routing_tables(top_i)
        dest2 = dest.reshape(nt // TOK_TILE, TOK_TILE * TOP_K)
        xb = _dispatch(dest2, x1t, jnp.zeros((nblk * MOE_BLK * TOK_ROWS, LANE), F32))
        yb = _experts(l, block_e, n_used, xb, exp_w_gate, exp_w_up, exp_w_down,
                      exp_b_gate, exp_b_up, exp_b_down)
        x, x_bf = _combine(dest2, route, x1, ln2_g[l].reshape(1, D_MODEL), ln2_b[l].reshape(1, D_MODEL),
                           yb, alpha)

        outs["gdn_p"].append(s_p)
        outs["ssm_p"].append(h_p.reshape(batch, SSM_HEADS, SSM_HEADDIM, SSM_DSTATE))
        tails_g = [lax.slice(proj, (b * seq + seq - (CONV_W - 1), 0), (b * seq + seq, GDN_CONV_CH))
                   for b in range(batch)]
        tails_s = [lax.slice(proj, (b * seq + seq - (CONV_W - 1), o_x - 2 * GDN_HEADS),
                             (b * seq + seq, o_x - 2 * GDN_HEADS + SSM_CONV_CH)) for b in range(batch)]
        outs["gconv_p"].append(jnp.stack(tails_g))
        outs["sconv_p"].append(jnp.stack(tails_s))
        outs["gconv_s"].append(jnp.concatenate(
            [state_gdn_conv[l][:, 1:], proj_s[:, None, :GDN_CONV_CH]], axis=1))
        xbc_off = o_x - 2 * GDN_HEADS
        outs["sconv_s"].append(jnp.concatenate(
            [state_ssm_conv[l][:, 1:], proj_s[:, None, xbc_off:xbc_off + SSM_CONV_CH]], axis=1))

    yp = x[:n_p].reshape(batch, seq, D_MODEL)
    ys = x[n_p:].reshape(dec, 1, D_MODEL)
    return (yp, ys, jnp.stack(outs["gdn_p"]), gdn_s, jnp.stack(outs["gconv_p"]),
            jnp.stack(outs["gconv_s"]), jnp.stack(outs["ssm_p"]),
            ssm_s.reshape(depth, dec, SSM_HEADS, SSM_HEADDIM, SSM_DSTATE),
            jnp.stack(outs["sconv_p"]), jnp.stack(outs["sconv_s"]))
```

```python
import functools

import jax
import jax.numpy as jnp
from jax import lax
from jax.experimental import pallas as pl
from jax.experimental.pallas import tpu as pltpu

F32 = jnp.float32
BF16 = jnp.bfloat16

D_MODEL = 1024
GDN_HEADS = 8
GDN_DK = 128
GDN_VDIM = 1024
GDN_CONV_CH = 3072
SSM_HEADS = 32
SSM_HEADDIM = 64
SSM_GROUPS = 4
SSM_DINNER = 2048
SSM_DSTATE = 128
SSM_CONV_CH = 3072
CONV_W = 4
N_EXPERTS = 32
TOP_K = 4
SWIGLU_ALPHA = 1.702
SWIGLU_LIMIT = 7.0
LN_EPS = 1e-5
RMS_EPS = 1e-6
L2_EPS = 1e-6
NEG_BIG = -1e30

W_BIG = 11264
SMALL_W = 128
COL_B, COL_A, COL_DT = 0, 8, 16

LANE = 128
SUBLANE = 8
CHUNK = 128
GDN_HB = 8
MOE_BLK = 512
TOK_TILE = 128
TOK_ROWS = D_MODEL // LANE
VMEM_LIMIT = 56 * 1024 * 1024


def _pick(n, cands):
    for c in cands:
        if n % c == 0:
            return c
    raise ValueError(f"no tile for {n}")


def _bf(x):
    return x.astype(BF16)


def _dot(a, b, prec=None):
    return jnp.dot(a, b, preferred_element_type=F32, precision=prec)


def _dot_nt(a, b):
    return lax.dot_general(a, b, (((1,), (1,)), ((), ())), preferred_element_type=F32)


def _dot_tn(a, b):
    return lax.dot_general(a, b, (((0,), (0,)), ((), ())), preferred_element_type=F32)


def _split3(x):
    hi = _bf(x)
    r = x - hi.astype(F32)
    mid = _bf(r)
    return hi, mid, _bf(r - mid.astype(F32))


def _dot_sel_rhs(x, sel):
    hi, mid, lo = _split3(x)
    return (_dot(lo, sel) + _dot(mid, sel)) + _dot(hi, sel)


def _dot_sel_lhs(sel, x):
    hi, mid, lo = _split3(x)
    return (_dot(sel, lo) + _dot(sel, mid)) + _dot(sel, hi)


def _sigmoid(x):
    return jax.nn.sigmoid(x)


def _silu(x):
    return x * jax.nn.sigmoid(x)


def _softplus(x):
    return jnp.maximum(x, 0.0) + jnp.log(1.0 + jnp.exp(-jnp.abs(x)))


def _layer_norm(x, g, b):
    mu = jnp.mean(x, axis=-1, keepdims=True)
    xc = x - mu
    var = jnp.mean(xc * xc, axis=-1, keepdims=True)
    return xc * lax.rsqrt(var + LN_EPS) * g + b


def _small_act(raw, pv):
    col = lax.broadcasted_iota(jnp.int32, raw.shape, 1)
    sp = _softplus(raw + pv[0:1, :])
    act = jnp.where(col < COL_A, _sigmoid(raw), sp)
    gda = sp * (-jnp.exp(pv[1:2, :]))
    return act, gda


def _mm_kernel(x_ref, w_ref, o_ref):
    o_ref[...] = _dot(x_ref[...], w_ref[...])


def _matmul(x, w, tm, tn):
    m, k = x.shape
    n = w.shape[1]
    return pl.pallas_call(
        _mm_kernel,
        out_shape=jax.ShapeDtypeStruct((m, n), F32),
        grid=(n // tn, m // tm),
        in_specs=[pl.BlockSpec((tm, k), lambda j, i: (i, 0)),
                  pl.BlockSpec((k, tn), lambda j, i: (0, j))],
        out_specs=pl.BlockSpec((tm, tn), lambda j, i: (i, j)),
        compiler_params=pltpu.CompilerParams(
            dimension_semantics=("parallel", "parallel"), vmem_limit_bytes=VMEM_LIMIT),
        name="in_proj",
    )(x, w)


def _tri_inv_all(mats, ii, jj, c):
    eye = (ii == jj).astype(F32)
    pair = (ii >> 1) == (jj >> 1)
    ts = [eye - jnp.where(pair, a, 0.0) for a in mats]
    abs_ = [_bf(a) for a in mats]
    s = 1
    while (2 << s) <= c:
        same_outer = (ii >> (s + 1)) == (jj >> (s + 1))
        same_inner = (ii >> s) == (jj >> s)
        off = _bf((same_outer & jnp.logical_not(same_inner)).astype(F32))
        tbs = [_bf(t) for t in ts]
        tes = [_dot(tb, ab * off) for tb, ab in zip(tbs, abs_)]
        ts = [t - _dot(_bf(te), tb) for t, te, tb in zip(ts, tes, tbs)]
        s += 1
    return ts


def _gdn_kernel(q_ref, k_ref, v_ref, zg_ref, small_ref, cwq_ref, cwk_ref, cwv_ref, pv_ref, nw_ref,
                o_ref, sfin_ref, xf, tail, s_scr):
    assert GDN_HB == GDN_HEADS
    c_len = q_ref.shape[0]
    c = pl.program_id(2)
    nc = pl.num_programs(2)

    @pl.when(c == 0)
    def _():
        tail[...] = jnp.zeros_like(tail)
        s_scr[...] = jnp.zeros_like(s_scr)

    for p, r in enumerate((q_ref, k_ref, v_ref)):
        xf[p, 0:SUBLANE, :] = tail[p]
        xf[p, SUBLANE:SUBLANE + c_len, :] = r[...]
        tail[p] = r[c_len - SUBLANE:c_len, :]
    cws = (cwq_ref, cwk_ref, cwv_ref)

    def conv(p, hs):
        acc = None
        for i in range(CONV_W):
            term = cws[p][i:i + 1, hs] * xf[p, pl.ds(SUBLANE - (CONV_W - 1) + i, c_len), hs]
            acc = term if acc is None else acc + term
        return _silu(acc)

    act, gda = _small_act(small_ref[...], pv_ref[...])
    ii = lax.broadcasted_iota(jnp.int32, (c_len, c_len), 0)
    jj = lax.broadcasted_iota(jnp.int32, (c_len, c_len), 1)
    incl = ii >= jj
    strict = ii > jj
    gcum = _dot_sel_lhs(_bf(incl.astype(F32)), gda)
    gcum_t = gcum.T
    nw = nw_ref[...]

    heads = range(GDN_HB)
    hsl = [slice(hh * LANE, (hh + 1) * LANE) for hh in heads]
    betas = [act[:, COL_B + hh:COL_B + hh + 1] for hh in heads]
    gcs = [gcum[:, COL_A + hh:COL_A + hh + 1] for hh in heads]
    grs = [gcum_t[COL_A + hh:COL_A + hh + 1, :] for hh in heads]
    ks = []
    for hh in heads:
        k = conv(1, hsl[hh])
        ks.append(k * lax.rsqrt(jnp.sum(k * k, axis=-1, keepdims=True) + L2_EPS))
    kbs = [ks[hh] * betas[hh] for hh in heads]
    kbfs = [_bf(k) for k in ks]
    kks = [_dot_nt(_bf(kbs[hh]), kbfs[hh]) for hh in heads]
    gams = [jnp.exp(jnp.where(incl, gcs[hh] - grs[hh], NEG_BIG)) for hh in heads]
    amats = [jnp.where(strict, kks[hh] * gams[hh], 0.0) for hh in heads]
    qs = []
    for hh in heads:
        q = conv(0, hsl[hh])
        qs.append(q * lax.rsqrt(jnp.sum(q * q, axis=-1, keepdims=True) + L2_EPS) * (GDN_DK ** -0.5))
    qks = [_dot_nt(_bf(qs[hh]), kbfs[hh]) * gams[hh] for hh in heads]
    egs = [jnp.exp(gcs[hh]) for hh in heads]
    rhs = [_bf(jnp.concatenate([conv(2, hsl[hh]) * betas[hh], kbs[hh] * egs[hh]], axis=1)) for hh in heads]
    ts = _tri_inv_all(amats, ii, jj, c_len)
    uws = [_dot(_bf(ts[hh]), rhs[hh]) for hh in heads]
    glasts = [gcs[hh][c_len - 1:c_len, :] for hh in heads]
    s_olds = [s_scr[hh] for hh in heads]
    sbs = [_bf(s) for s in s_olds]
    v_news = [uws[hh][:, :LANE] - _dot(_bf(uws[hh][:, LANE:]), sbs[hh]) for hh in heads]
    vnbs = [_bf(v) for v in v_news]
    os_ = [_dot(_bf(qs[hh] * egs[hh]), sbs[hh]) + _dot(_bf(qks[hh]), vnbs[hh]) for hh in heads]
    for hh in heads:
        kdec = ks[hh] * jnp.exp(glasts[hh] - gcs[hh])
        s_scr[hh] = s_olds[hh] * jnp.exp(glasts[hh]) + _dot_tn(_bf(kdec), vnbs[hh])
    for hh in heads:
        o = os_[hh]
        o = (o * lax.rsqrt(jnp.mean(o * o, axis=-1, keepdims=True) + RMS_EPS) * nw
             * _silu(zg_ref[:, hsl[hh]]))
        o_ref[:, hsl[hh]] = _bf(o)

    @pl.when(c == nc - 1)
    def _():
        sfin_ref[0] = s_scr[...]


def _gdn_prompt(proj, small, conv_w, pv, norm_w, batch, seq):
    nc = seq // CHUNK
    hbw = GDN_HB * LANE
    ngrp = GDN_HEADS // GDN_HB
    kq, kk, kv, kz = 0, GDN_VDIM // hbw, 2 * GDN_VDIM // hbw, 3 * GDN_VDIM // hbw

    def row(b, hg, c):
        return b * nc + c

    def pspec(off):
        return pl.BlockSpec((CHUNK, hbw), lambda b, hg, c: (row(b, hg, c), off + hg))

    def wspec(off):
        return pl.BlockSpec((CONV_W, hbw), lambda b, hg, c: (0, off + hg))

    return pl.pallas_call(
        _gdn_kernel,
        out_shape=(jax.ShapeDtypeStruct((batch * seq, GDN_VDIM), BF16),
                   jax.ShapeDtypeStruct((batch, GDN_HEADS, GDN_DK, LANE), F32)),
        grid=(batch, ngrp, nc),
        in_specs=[pspec(kq), pspec(kk), pspec(kv), pspec(kz),
                  pl.BlockSpec((CHUNK, SMALL_W), lambda b, hg, c: (row(b, hg, c), 0)),
                  wspec(kq), wspec(kk), wspec(kv),
                  pl.BlockSpec((SUBLANE, SMALL_W), lambda b, hg, c: (0, 0)),
                  pl.BlockSpec((1, LANE), lambda b, hg, c: (0, 0))],
        out_specs=(pl.BlockSpec((CHUNK, hbw), lambda b, hg, c: (row(b, hg, c), hg)),
                   pl.BlockSpec((1, GDN_HB, GDN_DK, LANE), lambda b, hg, c: (b, hg, 0, 0))),
        scratch_shapes=[pltpu.VMEM((3, CHUNK + SUBLANE, hbw), F32),
                        pltpu.VMEM((3, SUBLANE, hbw), F32),
                        pltpu.VMEM((GDN_HB, GDN_DK, LANE), F32)],
        compiler_params=pltpu.CompilerParams(
            dimension_semantics=("parallel", "parallel", "arbitrary"), vmem_limit_bytes=VMEM_LIMIT),
        name="gdn_prompt",
    )(proj, proj, proj, proj, small, conv_w, conv_w, conv_w, pv, norm_w)


def _ssd_kernel(xs_ref, b_ref, c_ref, zs_ref, small_ref, cwx_ref, cwb_ref, cwc_ref, cbx_ref, cbb_ref,
                cbc_ref, pv_ref, ex_ref, dexp_ref, nw_ref, yz_ref, hfin_ref, xf, tail, ht, ydiag):
    c_len = xs_ref.shape[0]
    gw = SSM_DINNER // SSM_GROUPS
    hpg = gw // SSM_HEADDIM
    gn = SSM_GROUPS * SSM_DSTATE
    c = pl.program_id(1)
    nc = pl.num_programs(1)

    @pl.when(c == 0)
    def _():
        tail[...] = jnp.zeros_like(tail)
        ht[...] = jnp.zeros_like(ht)

    parts = ((xs_ref, cwx_ref, cbx_ref, 0, SSM_DINNER), (b_ref, cwb_ref, cbb_ref, SSM_DINNER, gn),
             (c_ref, cwc_ref, cbc_ref, SSM_DINNER + gn, gn))
    convs = []
    for r, cw, cb, off, wd in parts:
        sl = slice(off, off + wd)
        xf[0:SUBLANE, sl] = tail[:, sl]
        xf[SUBLANE:SUBLANE + c_len, sl] = r[...]
        tail[:, sl] = r[c_len - SUBLANE:c_len, :]
        acc = cb[...]
        for i in range(CONV_W):
            acc = acc + cw[i:i + 1, :] * xf[pl.ds(SUBLANE - (CONV_W - 1) + i, c_len), sl]
        convs.append(_silu(acc))
    xs, bm_all, cm_all = convs

    act, gda = _small_act(small_ref[...], pv_ref[...])
    ii = lax.broadcasted_iota(jnp.int32, (c_len, c_len), 0)
    jj = lax.broadcasted_iota(jnp.int32, (c_len, c_len), 1)
    incl = ii >= jj
    acs = _dot_sel_lhs(_bf(incl.astype(F32)), gda)
    acs_t = acs.T
    ex = ex_ref[...]
    dt_x = _dot_sel_rhs(act, ex)
    acs_x = _dot_sel_rhs(acs, ex)
    last = acs_x[c_len - 1:c_len, :]
    xdt = xs * dt_x
    xdec = _bf(xdt * jnp.exp(last - acs_x))
    lane = lax.broadcasted_iota(jnp.int32, (c_len, LANE), 1)
    lo_half = lane < SSM_HEADDIM

    groups = range(SSM_GROUPS)
    bms = [_bf(bm_all[:, g * SSM_DSTATE:(g + 1) * SSM_DSTATE]) for g in groups]
    cms = [_bf(cm_all[:, g * SSM_DSTATE:(g + 1) * SSM_DSTATE]) for g in groups]
    cbs = [_dot_nt(cms[g], bms[g]) for g in groups]
    h_olds = [ht[:, g * gw:(g + 1) * gw] for g in groups]
    y_offs = [_dot(cms[g], _bf(h_olds[g])) for g in groups]
    for g in groups:
        for pr in range(hpg // 2):
            ps = slice(g * gw + pr * LANE, g * gw + (pr + 1) * LANE)
            xpair = xdt[:, ps]
            acc = None
            for half in range(2):
                head = g * hpg + pr * 2 + half
                ac = acs[:, COL_DT + head:COL_DT + head + 1]
                ar = acs_t[COL_DT + head:COL_DT + head + 1, :]
                sc = cbs[g] * jnp.exp(jnp.where(incl, ac - ar, NEG_BIG))
                keep = lo_half if half == 0 else jnp.logical_not(lo_half)
                term = _dot(_bf(sc), _bf(jnp.where(keep, xpair, 0.0)))
                acc = term if acc is None else acc + term
            ydiag[:, ps] = acc
    for g in groups:
        gs = slice(g * gw, (g + 1) * gw)
        ht[:, gs] = h_olds[g] * jnp.exp(last[:, gs]) + _dot_tn(bms[g], xdec[:, gs])
    y = ydiag[...] + jnp.concatenate(y_offs, axis=1) * jnp.exp(acs_x) + dexp_ref[...] * xs
    yz = y * _silu(zs_ref[...])
    nw = nw_ref[...]
    for g in groups:
        gs = slice(g * gw, (g + 1) * gw)
        yg = yz[:, gs]
        yz_ref[:, gs] = _bf(yg * lax.rsqrt(jnp.mean(yg * yg, axis=-1, keepdims=True) + RMS_EPS) * nw[:, gs])

    @pl.when(c == nc - 1)
    def _():
        hfin_ref[0] = ht[...].T


def _ssd_prompt(proj, small, conv_w, conv_b, pv, expand, d_exp, norm_w, batch, seq):
    nc = seq // CHUNK
    gn = SSM_GROUPS * SSM_DSTATE
    x_off = (GDN_CONV_CH + GDN_VDIM + SSM_DINNER)
    z_off = GDN_CONV_CH + GDN_VDIM
    kx = x_off // SSM_DINNER
    kb = (x_off + SSM_DINNER) // gn
    kz = z_off // SSM_DINNER
    wb = SSM_DINNER // gn

    def row(b, c):
        return b * nc + c

    in_specs = [
        pl.BlockSpec((CHUNK, SSM_DINNER), lambda b, c: (row(b, c), kx)),
        pl.BlockSpec((CHUNK, gn), lambda b, c: (row(b, c), kb)),
        pl.BlockSpec((CHUNK, gn), lambda b, c: (row(b, c), kb + 1)),
        pl.BlockSpec((CHUNK, SSM_DINNER), lambda b, c: (row(b, c), kz)),
        pl.BlockSpec((CHUNK, SMALL_W), lambda b, c: (row(b, c), 0)),
        pl.BlockSpec((CONV_W, SSM_DINNER), lambda b, c: (0, 0)),
        pl.BlockSpec((CONV_W, gn), lambda b, c: (0, wb)),
        pl.BlockSpec((CONV_W, gn), lambda b, c: (0, wb + 1)),
        pl.BlockSpec((1, SSM_DINNER), lambda b, c: (0, 0)),
        pl.BlockSpec((1, gn), lambda b, c: (0, wb)),
        pl.BlockSpec((1, gn), lambda b, c: (0, wb + 1)),
        pl.BlockSpec((SUBLANE, SMALL_W), lambda b, c: (0, 0)),
        pl.BlockSpec((SMALL_W, SSM_DINNER), lambda b, c: (0, 0)),
        pl.BlockSpec((1, SSM_DINNER), lambda b, c: (0, 0)),
        pl.BlockSpec((1, SSM_DINNER), lambda b, c: (0, 0)),
    ]
    return pl.pallas_call(
        _ssd_kernel,
        out_shape=(jax.ShapeDtypeStruct((batch * seq, SSM_DINNER), BF16),
                   jax.ShapeDtypeStruct((batch, SSM_DINNER, SSM_DSTATE), F32)),
        grid=(batch, nc),
        in_specs=in_specs,
        out_specs=(pl.BlockSpec((CHUNK, SSM_DINNER), lambda b, c: (row(b, c), 0)),
                   pl.BlockSpec((1, SSM_DINNER, SSM_DSTATE), lambda b, c: (b, 0, 0))),
        scratch_shapes=[pltpu.VMEM((CHUNK + SUBLANE, SSM_CONV_CH), F32),
                        pltpu.VMEM((SUBLANE, SSM_CONV_CH), F32),
                        pltpu.VMEM((SSM_DSTATE, SSM_DINNER), F32),
                        pltpu.VMEM((CHUNK, SSM_DINNER), F32)],
        compiler_params=pltpu.CompilerParams(
            dimension_semantics=("parallel", "arbitrary"), vmem_limit_bytes=VMEM_LIMIT),
        name="ssd_prompt",
    )(proj, proj, proj, proj, small, conv_w, conv_w, conv_w, conv_b, conv_b, conv_b, pv, expand,
      d_exp, norm_w)


def _sample_prep_kernel(small_ref, pv_ref, ex_ref, act_ref, gda_ref, dtx_ref, dax_ref):
    act, gda = _small_act(small_ref[...], pv_ref[...])
    act_ref[...] = act
    gda_ref[...] = gda
    ex = ex_ref[...]
    dtx_ref[...] = _dot_sel_rhs(act, ex)
    dax_ref[...] = _dot_sel_rhs(gda, ex)


def _sample_prep(small_s, pv, expand_all):
    n = small_s.shape[0]
    return pl.pallas_call(
        _sample_prep_kernel,
        out_shape=(jax.ShapeDtypeStruct((n, SMALL_W), F32), jax.ShapeDtypeStruct((n, SMALL_W), F32),
                   jax.ShapeDtypeStruct((n, SSM_DINNER), F32), jax.ShapeDtypeStruct((n, SSM_DINNER), F32)),
        name="sample_prep",
    )(small_s, pv, expand_all)


def _sample_kernel(proj_ref, act_ref, gda_ref, dtx_ref, dax_ref, gcs_ref, scs_ref, s0_ref, h0_ref, gcw_ref,
                   scw_ref, scb_ref, gnw_ref, dexp_ref, snw_ref, *rest, n_alias):
    og_ref, yz_ref, s_ref, h_ref, stk, stk2 = rest[n_alias:]
    if n_alias == 0:
        if s_ref.shape[0] > 1:
            s_ref[1:] = jnp.zeros((s_ref.shape[0] - 1,) + s_ref.shape[1:], F32)
            h_ref[1:] = jnp.zeros((h_ref.shape[0] - 1,) + h_ref.shape[1:], F32)
        s_ref = s_ref.at[0]
        h_ref = h_ref.at[0]
    gw = SSM_DINNER // SSM_GROUPS
    z_off = GDN_CONV_CH + GDN_VDIM
    x_off = z_off + SSM_DINNER

    def conv1(state_ref, w_ref, new_row):
        acc = w_ref[CONV_W - 1:CONV_W, :] * new_row
        for i in range(CONV_W - 1):
            acc = acc + w_ref[i:i + 1, :] * state_ref[i:i + 1, :]
        return acc

    act = act_ref[...]
    gda = gda_ref[...]

    qkv = _silu(conv1(gcs_ref, gcw_ref, proj_ref[:, 0:GDN_CONV_CH]))
    stk[...] = jnp.zeros_like(stk)
    qs, vs = [], []
    for h in range(GDN_HEADS):
        q = qkv[:, h * LANE:(h + 1) * LANE]
        k = qkv[:, GDN_VDIM + h * LANE:GDN_VDIM + (h + 1) * LANE]
        q = q * lax.rsqrt(jnp.sum(q * q, axis=-1, keepdims=True) + L2_EPS) * (GDN_DK ** -0.5)
        k = k * lax.rsqrt(jnp.sum(k * k, axis=-1, keepdims=True) + L2_EPS)
        stk[h:h + 1, :] = k
        stk[GDN_HEADS + h:GDN_HEADS + h + 1, :] = q
        vs.append(qkv[:, 2 * GDN_VDIM + h * LANE:2 * GDN_VDIM + (h + 1) * LANE])
    cols = stk[...].T
    gnw = gnw_ref[...]
    for h in range(GDN_HEADS):
        kc = cols[:, h:h + 1]
        qc = cols[:, GDN_HEADS + h:GDN_HEADS + h + 1]
        beta = act[:, COL_B + h:COL_B + h + 1]
        gh = gda[:, COL_A + h:COL_A + h + 1]
        sd = s0_ref[h] * jnp.exp(gh)
        v_old = jnp.sum(sd * kc, axis=0, keepdims=True)
        delta = (vs[h] - v_old) * beta
        s_new = sd + kc * delta
        s_ref[h] = s_new
        o = jnp.sum(s_new * qc, axis=0, keepdims=True)
        zg = proj_ref[:, GDN_CONV_CH + h * LANE:GDN_CONV_CH + (h + 1) * LANE]
        o = o * lax.rsqrt(jnp.mean(o * o, axis=-1, keepdims=True) + RMS_EPS) * gnw * _silu(zg)
        og_ref[:, h * LANE:(h + 1) * LANE] = _bf(o)

    xbc = _silu(conv1(scs_ref, scw_ref, proj_ref[:, x_off:x_off + SSM_CONV_CH]) + scb_ref[...])
    dexp = dexp_ref[...]
    stk2[...] = jnp.zeros_like(stk2)
    for g in range(SSM_GROUPS):
        gs = slice(g * gw, (g + 1) * gw)
        dt_x = dtx_ref[:, gs]
        da_x = dax_ref[:, gs]
        xs = xbc[:, gs]
        bm = xbc[:, SSM_DINNER + g * SSM_DSTATE:SSM_DINNER + (g + 1) * SSM_DSTATE]
        cm = xbc[:, SSM_DINNER + SSM_GROUPS * SSM_DSTATE + g * SSM_DSTATE:
                 SSM_DINNER + SSM_GROUPS * SSM_DSTATE + (g + 1) * SSM_DSTATE]
        stk2[0:1, :] = jnp.exp(da_x)
        stk2[1:2, :] = xs * dt_x
        cols2 = stk2[...].T
        h_new = h0_ref[gs, :] * cols2[:, 0:1] + cols2[:, 1:2] * bm
        h_ref[gs, :] = h_new
        cm16 = jnp.broadcast_to(cm, (2 * SUBLANE, SSM_DSTATE))
        y = _dot_nt(_bf(cm16), _bf(h_new))[0:1, :] + dexp[:, gs] * xs
        yz = y * _silu(proj_ref[:, z_off + g * gw:z_off + (g + 1) * gw])
        yz = yz * lax.rsqrt(jnp.mean(yz * yz, axis=-1, keepdims=True) + RMS_EPS) * snw_ref[:, gs]
        yz_ref[:, gs] = _bf(yz)


def _sample_mix(layer, proj_s, small_s, gconv_state, sconv_state, s_all, h_all, s_prev, h_prev, gconv_w,
                sconv_w, sconv_b, pv, gnorm_w, expand_all, d_exp, snorm_w):
    n = proj_s.shape[0]
    gw = SSM_DINNER // SSM_GROUPS
    proj3 = proj_s.reshape(n, 1, W_BIG)
    act, gda, dtx, dax = _sample_prep(small_s, pv, expand_all)

    def full(shape):
        nd = len(shape)
        return pl.BlockSpec(shape, lambda i: (0,) * nd)

    def per_seq(shape):
        nd = len(shape)
        return pl.BlockSpec((None,) + shape, lambda i: (i,) + (0,) * nd)

    def per_layer_seq(shape):
        nd = len(shape)
        return pl.BlockSpec((None, None) + shape, lambda i: (layer, i) + (0,) * nd)

    in_specs = [per_seq((1, W_BIG)), per_seq((1, SMALL_W)), per_seq((1, SMALL_W)),
                per_seq((1, SSM_DINNER)), per_seq((1, SSM_DINNER)),
                per_seq((CONV_W - 1, GDN_CONV_CH)), per_seq((CONV_W - 1, SSM_CONV_CH)),
                per_layer_seq((GDN_HEADS, GDN_DK, LANE)), per_layer_seq((SSM_DINNER, SSM_DSTATE)),
                full((CONV_W, GDN_CONV_CH)), full((CONV_W, SSM_CONV_CH)), full((1, SSM_CONV_CH)),
                full((1, LANE)), full((1, SSM_DINNER)), full((1, SSM_DINNER))]
    args = [proj3, act.reshape(n, 1, SMALL_W), gda.reshape(n, 1, SMALL_W), dtx.reshape(n, 1, SSM_DINNER),
            dax.reshape(n, 1, SSM_DINNER), gconv_state, sconv_state, s_all, h_all, gconv_w, sconv_w, sconv_b,
            gnorm_w, d_exp, snorm_w]
    aliases = {}
    if s_prev is not None:
        aliases = {len(args): 2, len(args) + 1: 3}
        in_specs += [pl.BlockSpec(memory_space=pl.ANY), pl.BlockSpec(memory_space=pl.ANY)]
        args += [s_prev, h_prev]
    if s_prev is None:
        depth = s_all.shape[0]
        state_specs = [pl.BlockSpec((depth, None, GDN_HEADS, GDN_DK, LANE), lambda i: (0, i, 0, 0, 0)),
                       pl.BlockSpec((depth, None, SSM_DINNER, SSM_DSTATE), lambda i: (0, i, 0, 0))]
    else:
        state_specs = [per_layer_seq((GDN_HEADS, GDN_DK, LANE)), per_layer_seq((SSM_DINNER, SSM_DSTATE))]
    kern = functools.partial(_sample_kernel, n_alias=len(aliases))
    og, yz, s_new, h_new = pl.pallas_call(
        kern,
        out_shape=(jax.ShapeDtypeStruct((n, 1, GDN_VDIM), BF16),
                   jax.ShapeDtypeStruct((n, 1, SSM_DINNER), BF16),
                   jax.ShapeDtypeStruct(s_all.shape, F32),
                   jax.ShapeDtypeStruct(h_all.shape, F32)),
        grid=(n,),
        in_specs=in_specs,
        out_specs=(per_seq((1, GDN_VDIM)), per_seq((1, SSM_DINNER)), state_specs[0], state_specs[1]),
        scratch_shapes=[pltpu.VMEM((LANE, LANE), F32), pltpu.VMEM((LANE, gw), F32)],
        input_output_aliases=aliases,
        compiler_params=pltpu.CompilerParams(
            dimension_semantics=("parallel",), vmem_limit_bytes=VMEM_LIMIT),
        name="sample_mix",
    )(*args)
    return og.reshape(n, GDN_VDIM), yz.reshape(n, SSM_DINNER), s_new, h_new


def _merge_kernel(og_ref, yz_ref, ga_ref, gb_ref, x_ref, wbg_ref, wbs_ref, wout_ref, lng_ref, lnb_ref,
                  rwh_ref, rwl_ref, rb_ref, x1_ref, x1t_ref, route_ref, *, alpha):
    a = _dot(og_ref[...], wbg_ref[...])
    b = _dot(yz_ref[...], wbs_ref[...])
    merged = _sigmoid(ga_ref[...]) * a + _sigmoid(gb_ref[...]) * b
    mix = _dot(_bf(merged), wout_ref[...])
    x1 = _layer_norm(alpha * x_ref[...] + mix, lng_ref[...], lnb_ref[...])
    x1_ref[...] = x1
    _store_token_tiles(x1t_ref, x1)

    xh = _bf(x1)
    xl = _bf(x1 - xh.astype(F32))
    rwh = rwh_ref[...]
    lg = (_dot(xh, rwl_ref[...]) + _dot(xl, rwh)) + _dot(xh, rwh) + rb_ref[...]
    colf = lax.broadcasted_iota(jnp.int32, lg.shape, 1).astype(F32)
    vals, idxs = [], []
    for _ in range(TOP_K):
        m = jnp.max(lg, axis=-1, keepdims=True)
        idx = jnp.min(jnp.where(lg == m, colf, float(LANE)), axis=-1, keepdims=True)
        vals.append(m)
        idxs.append(idx)
        lg = jnp.where(colf == idx, 2.0 * NEG_BIG, lg)
    es = [jnp.exp(v - vals[0]) for v in vals]
    den = es[0] + es[1] + es[2] + es[3]
    route = jnp.zeros_like(lg)
    for kk in range(TOP_K):
        route = jnp.where(colf == float(kk), es[kk] / den, route)
        route = jnp.where(colf == float(TOP_K + kk), idxs[kk], route)
    route_ref[...] = route


def _merge(og, yz, proj, x, wbg, wbs, wout, lng, lnb, rw, rb, alpha, tm):
    nt = x.shape[0]
    ka = (W_BIG - 2 * D_MODEL) // D_MODEL
    kern = functools.partial(_merge_kernel, alpha=alpha)
    rw_hi = _bf(rw)

    def full(shape):
        return pl.BlockSpec(shape, lambda i: (0, 0))

    return pl.pallas_call(
        kern,
        out_shape=(jax.ShapeDtypeStruct((nt, D_MODEL), F32),
                   jax.ShapeDtypeStruct((nt * TOK_ROWS, LANE), F32),
                   jax.ShapeDtypeStruct((nt, LANE), F32)),
        grid=(nt // tm,),
        in_specs=[pl.BlockSpec((tm, GDN_VDIM), lambda i: (i, 0)),
                  pl.BlockSpec((tm, SSM_DINNER), lambda i: (i, 0)),
                  pl.BlockSpec((tm, D_MODEL), lambda i: (i, ka)),
                  pl.BlockSpec((tm, D_MODEL), lambda i: (i, ka + 1)),
                  pl.BlockSpec((tm, D_MODEL), lambda i: (i, 0)),
                  full((GDN_VDIM, D_MODEL)), full((SSM_DINNER, D_MODEL)), full((D_MODEL, D_MODEL)),
                  full((1, D_MODEL)), full((1, D_MODEL)), full((D_MODEL, LANE)), full((D_MODEL, LANE)),
                  full((1, LANE))],
        out_specs=(pl.BlockSpec((tm, D_MODEL), lambda i: (i, 0)),
                   pl.BlockSpec((tm * TOK_ROWS, LANE), lambda i: (i, 0)),
                   pl.BlockSpec((tm, LANE), lambda i: (i, 0))),
        compiler_params=pltpu.CompilerParams(
            dimension_semantics=("parallel",), vmem_limit_bytes=VMEM_LIMIT),
        name="merge_ln_router",
    )(og, yz, proj, proj, x, wbg, wbs, wout, lng, lnb, rw_hi, _bf(rw - rw_hi.astype(F32)), rb)


def _store_token_tiles(ref, val):
    n = val.shape[0]
    for cc in range(TOK_ROWS):
        ref[pl.ds(cc, n, stride=TOK_ROWS), :] = val[:, cc * LANE:(cc + 1) * LANE]


def _load_token_chunk(ref, cc, n):
    return ref[pl.ds(cc, n, stride=TOK_ROWS), :]


def _token_rows(t):
    return pl.ds(pl.multiple_of(t * TOK_ROWS, TOK_ROWS), TOK_ROWS)


def _dispatch_kernel(dest_hbm, x_ref, xb_in, xb_out, idx, sem_idx, sem):
    del xb_in
    i = pl.program_id(0)
    cp = pltpu.make_async_copy(dest_hbm.at[i], idx, sem_idx)
    cp.start()
    cp.wait()

    def issue(r, carry):
        for kk in range(TOP_K):
            pltpu.make_async_copy(x_ref.at[_token_rows(r), :],
                                  xb_out.at[_token_rows(idx[r * TOP_K + kk]), :], sem).start(priority=kk % 2)
        return carry

    lax.fori_loop(0, TOK_TILE, issue, 0, unroll=8)
    for _ in range(TOP_K):
        pltpu.make_async_copy(x_ref, xb_out.at[pl.ds(0, TOK_TILE * TOK_ROWS), :], sem).wait()


def _dispatch(dest2, x1t, xb_zero):
    n_tiles = dest2.shape[0]
    return pl.pallas_call(
        _dispatch_kernel,
        out_shape=jax.ShapeDtypeStruct(xb_zero.shape, F32),
        grid=(n_tiles,),
        in_specs=[pl.BlockSpec(memory_space=pl.ANY),
                  pl.BlockSpec((TOK_TILE * TOK_ROWS, LANE), lambda i: (i, 0)),
                  pl.BlockSpec(memory_space=pl.ANY)],
        out_specs=pl.BlockSpec(memory_space=pl.ANY),
        scratch_shapes=[pltpu.SMEM((TOK_TILE * TOP_K,), jnp.int32),
                        pltpu.SemaphoreType.DMA, pltpu.SemaphoreType.DMA],
        input_output_aliases={2: 0},
        compiler_params=pltpu.CompilerParams(
            dimension_semantics=("arbitrary",), vmem_limit_bytes=VMEM_LIMIT),
        name="moe_dispatch",
    )(dest2, x1t, xb_zero)


def _expert_kernel(be_ref, nu_ref, x_ref, wg_ref, wu_ref, wd_ref, bg_ref, bu_ref, bd_ref, y_ref,
                   wgb, wub, wdb):
    j = pl.program_id(0)
    e = be_ref[j]
    prev = be_ref[jnp.maximum(j - 1, 0)]
    used = j < nu_ref[0]

    @pl.when(used & ((j == 0) | (e != prev)))
    def _():
        wgb[...] = _bf(wg_ref[...])
        wub[...] = _bf(wu_ref[...])
        wdb[...] = _bf(wd_ref[...])

    @pl.when(used)
    def _():
        x = jnp.concatenate([_bf(_load_token_chunk(x_ref, cc, MOE_BLK)) for cc in range(TOK_ROWS)], axis=1)
        gt = _dot(x, wgb[...]) + bg_ref[...]
        up = _dot(x, wub[...]) + bu_ref[...]
        gt = jnp.minimum(gt, SWIGLU_LIMIT)
        up = jnp.clip(up, -SWIGLU_LIMIT, SWIGLU_LIMIT)
        h = (up + 1.0) * (gt * _sigmoid(SWIGLU_ALPHA * gt))
        _store_token_tiles(y_ref, _dot(_bf(h), wdb[...]) + bd_ref[...])

    @pl.when(jnp.logical_not(used))
    def _():
        y_ref[...] = jnp.zeros_like(y_ref)


def _experts(layer, block_e, n_used, xb, wg, wu, wd, bg, bu, bd):
    rows = xb.shape[0]
    blk_rows = MOE_BLK * TOK_ROWS
    nblk = rows // blk_rows
    depth = wg.shape[0]
    d_e = wg.shape[-1]
    wspec_in = pl.BlockSpec((None, None, D_MODEL, d_e), lambda j, be, nu: (layer, be[j], 0, 0))
    wspec_out = pl.BlockSpec((None, None, d_e, D_MODEL), lambda j, be, nu: (layer, be[j], 0, 0))
    bspec_e = pl.BlockSpec((None, None, 1, d_e), lambda j, be, nu: (layer, be[j], 0, 0))
    bspec_d = pl.BlockSpec((None, None, 1, D_MODEL), lambda j, be, nu: (layer, be[j], 0, 0))
    grid_spec = pltpu.PrefetchScalarGridSpec(
        num_scalar_prefetch=2,
        grid=(nblk,),
        in_specs=[pl.BlockSpec((blk_rows, LANE), lambda j, be, nu: (j, 0)),
                  wspec_in, wspec_in, wspec_out, bspec_e, bspec_e, bspec_d],
        out_specs=pl.BlockSpec((blk_rows, LANE), lambda j, be, nu: (j, 0)),
        scratch_shapes=[pltpu.VMEM((D_MODEL, d_e), BF16), pltpu.VMEM((D_MODEL, d_e), BF16),
                        pltpu.VMEM((d_e, D_MODEL), BF16)],
    )
    return pl.pallas_call(
        _expert_kernel,
        out_shape=jax.ShapeDtypeStruct((rows, LANE), F32),
        grid_spec=grid_spec,
        compiler_params=pltpu.CompilerParams(
            dimension_semantics=("arbitrary",), vmem_limit_bytes=VMEM_LIMIT),
        name="moe_experts",
    )(block_e, n_used, xb, wg, wu, wd, bg.reshape(depth, N_EXPERTS, 1, d_e),
      bu.reshape(depth, N_EXPERTS, 1, d_e), bd.reshape(depth, N_EXPERTS, 1, D_MODEL))


def _combine_kernel(dest_hbm, gates_ref, x1_ref, lng_ref, lnb_ref, yb_hbm, y_ref, ybf_ref, idx, buf,
                    sem_idx, sem, *, alpha):
    i = pl.program_id(0)
    n = pl.num_programs(0)
    slot = i % 2
    nxt = 1 - slot

    def idx_copy(tile, s):
        return pltpu.make_async_copy(dest_hbm.at[tile], idx.at[s], sem_idx.at[s])

    def issue_gathers(s):
        def issue(r, carry):
            for kk in range(TOP_K):
                pltpu.make_async_copy(yb_hbm.at[_token_rows(idx[s, r * TOP_K + kk]), :],
                                      buf.at[s, kk, _token_rows(r), :], sem.at[s]).start(priority=kk % 2)
            return carry

        lax.fori_loop(0, TOK_TILE, issue, 0, unroll=8)

    @pl.when(i == 0)
    def _():
        first = idx_copy(0, 0)
        first.start()
        first.wait()
        issue_gathers(0)

        @pl.when(n > 1)
        def _():
            idx_copy(1, 1).start()

    @pl.when(i + 1 < n)
    def _():
        idx_copy(i + 1, nxt).wait()
        issue_gathers(nxt)

    @pl.when(i + 2 < n)
    def _():
        idx_copy(i + 2, slot).start()

    for kk in range(TOP_K):
        pltpu.make_async_copy(yb_hbm.at[pl.ds(0, TOK_TILE * TOK_ROWS), :], buf.at[slot, kk], sem.at[slot]).wait()

    gates = gates_ref[...]
    chunks = []
    for cc in range(TOK_ROWS):
        acc = gates[:, 0:1] * _load_token_chunk(buf.at[slot, 0], cc, TOK_TILE)
        for kk in range(1, TOP_K):
            acc = acc + gates[:, kk:kk + 1] * _load_token_chunk(buf.at[slot, kk], cc, TOK_TILE)
        chunks.append(acc)
    moe = jnp.concatenate(chunks, axis=1)
    y = _layer_norm(alpha * x1_ref[...] + moe, lng_ref[...], lnb_ref[...])
    y_ref[...] = y
    ybf_ref[...] = _bf(y)


def _combine(dest2, route, x1, lng, lnb, yb, alpha):
    nt = x1.shape[0]
    kern = functools.partial(_combine_kernel, alpha=alpha)
    return pl.pallas_call(
        kern,
        out_shape=(jax.ShapeDtypeStruct((nt, D_MODEL), F32),
                   jax.ShapeDtypeStruct((nt, D_MODEL), BF16)),
        grid=(nt // TOK_TILE,),
        in_specs=[pl.BlockSpec(memory_space=pl.ANY),
                  pl.BlockSpec((TOK_TILE, LANE), lambda i: (i, 0)),
                  pl.BlockSpec((TOK_TILE, D_MODEL), lambda i: (i, 0)),
                  pl.BlockSpec((1, D_MODEL), lambda i: (0, 0)),
                  pl.BlockSpec((1, D_MODEL), lambda i: (0, 0)),
                  pl.BlockSpec(memory_space=pl.ANY)],
        out_specs=(pl.BlockSpec((TOK_TILE, D_MODEL), lambda i: (i, 0)),
                   pl.BlockSpec((TOK_TILE, D_MODEL), lambda i: (i, 0))),
        scratch_shapes=[pltpu.SMEM((2, TOK_TILE * TOP_K), jnp.int32),
                        pltpu.VMEM((2, TOP_K, TOK_TILE * TOK_ROWS, LANE), F32),
                        pltpu.SemaphoreType.DMA((2,)), pltpu.SemaphoreType.DMA((2,))],
        compiler_params=pltpu.CompilerParams(
            dimension_semantics=("arbitrary",), vmem_limit_bytes=VMEM_LIMIT),
        name="moe_combine_ln",
    )(dest2, route, x1, lng, lnb, yb)


def _routing_tables(top_i):
    m = top_i.size
    flat_e = top_i.reshape(-1)
    onehot = (flat_e[:, None] == jnp.arange(N_EXPERTS, dtype=jnp.int32)[None, :]).astype(jnp.int32)
    oh3 = onehot.reshape(m // LANE, LANE, N_EXPERTS)
    tri = jnp.tril(jnp.ones((LANE, LANE), F32))
    within = jnp.einsum("ij,tjk->tik", tri, oh3.astype(F32)).astype(jnp.int32)
    tile_tot = within[:, -1, :]
    tile_off = jnp.cumsum(tile_tot, axis=0) - tile_tot
    csum = (within + tile_off[:, None, :]).reshape(m, N_EXPERTS)
    rank = jnp.sum(onehot * csum, axis=1) - 1
    counts = csum[-1]
    padded = (counts + MOE_BLK - 1) // MOE_BLK * MOE_BLK
    pad_ends = jnp.cumsum(padded)
    pad_starts = pad_ends - padded
    dest = jnp.sum(onehot * pad_starts[None, :], axis=1) + rank
    nblk = m // MOE_BLK + N_EXPERTS
    blk_start = jnp.arange(nblk, dtype=jnp.int32) * MOE_BLK
    block_e = jnp.minimum(jnp.sum((blk_start[:, None] >= pad_ends[None, :]).astype(jnp.int32), axis=1),
                          N_EXPERTS - 1)
    n_used = (pad_ends[-1] // MOE_BLK).astype(jnp.int32).reshape(1)
    return dest.astype(jnp.int32), block_e.astype(jnp.int32), n_used, nblk


def kernel(x_prompt, x_sample, state_gdn, state_gdn_conv, state_ssm, state_ssm_conv, w_in, gdn_conv_w,
           gdn_a_log, gdn_dt_bias, gdn_norm_w, ssm_conv_w, ssm_conv_b, ssm_a_log, ssm_dt_bias, ssm_d,
           ssm_norm_w, w_br_gdn, w_br_ssm, w_out, ln1_g, ln1_b, router_w, router_b, exp_w_gate,
           exp_b_gate, exp_w_up, exp_b_up, exp_w_down, exp_b_down, ln2_g, ln2_b):
    batch, seq, _ = x_prompt.shape
    dec = x_sample.shape[0]
    depth = w_in.shape[0]
    n_p = batch * seq
    nt = n_p + dec
    alpha = (2.0 * depth) ** 0.25
    gw = SSM_DINNER // SSM_GROUPS
    tm = _pick(nt, (384, 256, 128, 64, 32, 16))
    assert seq % CHUNK == 0 and nt % TOK_TILE == 0 and (nt * TOP_K) % MOE_BLK == 0

    x = jnp.concatenate([x_prompt.reshape(n_p, D_MODEL), x_sample.reshape(dec, D_MODEL)], axis=0)
    x_bf = _bf(x)

    o_zg = GDN_CONV_CH
    o_b = o_zg + GDN_VDIM
    o_a = o_b + GDN_HEADS
    o_zs = o_a + GDN_HEADS
    o_x = o_zs + SSM_DINNER
    o_dt = o_x + SSM_CONV_CH
    o_ga = o_dt + SSM_HEADS

    rows = jnp.arange(SMALL_W, dtype=jnp.int32)[:, None]
    lanes = jnp.arange(SSM_DINNER, dtype=jnp.int32)[None, :]
    expand_all = (rows == COL_DT + lanes // SSM_HEADDIM).astype(BF16)

    outs = {k: [] for k in ("gdn_p", "gconv_p", "gconv_s", "ssm_p", "sconv_p", "sconv_s")}
    ssm_all = state_ssm.reshape(depth, dec, SSM_DINNER, SSM_DSTATE)
    gdn_s = ssm_s = None
    for l in range(depth):
        w = w_in[l]
        w_big = _bf(jnp.concatenate([w[:, :o_b], w[:, o_zs:o_dt], w[:, o_ga:]], axis=1))
        w_small = _bf(jnp.concatenate(
            [w[:, o_b:o_zs], w[:, o_dt:o_ga],
             jnp.zeros((D_MODEL, SMALL_W - 2 * GDN_HEADS - SSM_HEADS), F32)], axis=1))
        zpad = jnp.zeros((SMALL_W - COL_DT - SSM_HEADS,), F32)
        pv = jnp.zeros((SUBLANE, SMALL_W), F32)
        pv = pv.at[0].set(jnp.concatenate([jnp.zeros((COL_A,), F32), gdn_dt_bias[l], ssm_dt_bias[l], zpad]))
        pv = pv.at[1].set(jnp.concatenate([jnp.zeros((COL_A,), F32), gdn_a_log[l], ssm_a_log[l], zpad]))
        d_exp = jnp.repeat(ssm_d[l], SSM_HEADDIM).reshape(1, SSM_DINNER)
        gnw = gdn_norm_w[l].reshape(1, LANE)
        snw = ssm_norm_w[l].reshape(1, SSM_DINNER)
        scb = ssm_conv_b[l].reshape(1, SSM_CONV_CH)

        proj = _matmul(x_bf, w_big, tm, W_BIG // 4)
        small = _matmul(x_bf, w_small, tm, SMALL_W)

        og_p, s_p = _gdn_prompt(proj, small, gdn_conv_w[l], pv, gnw, batch, seq)
        yz_p, h_p = _ssd_prompt(proj, small, ssm_conv_w[l], scb, pv, expand_all, d_exp, snw, batch, seq)
        proj_s = lax.slice(proj, (n_p, 0), (nt, W_BIG))
        small_s = lax.slice(small, (n_p, 0), (nt, SMALL_W))
        og_s, yz_s, gdn_s, ssm_s = _sample_mix(l, proj_s, small_s, state_gdn_conv[l], state_ssm_conv[l],
                                               state_gdn, ssm_all, gdn_s, ssm_s, gdn_conv_w[l],
                                               ssm_conv_w[l], scb, pv, gnw, expand_all, d_exp, snw)
        og = jnp.concatenate([og_p, og_s], axis=0)
        yz = jnp.concatenate([yz_p, yz_s], axis=0)

        rw = jnp.concatenate([router_w[l], jnp.zeros((D_MODEL, LANE - N_EXPERTS), F32)], axis=1)
        rb = jnp.concatenate([router_b[l], jnp.full((LANE - N_EXPERTS,), NEG_BIG, F32)]).reshape(1, LANE)
        x1, x1t, route = _merge(og, yz, proj, x, _bf(w_br_gdn[l]), _bf(w_br_ssm[l]), _bf(w_out[l]),
                                ln1_g[l].reshape(1, D_MODEL), ln1_b[l].reshape(1, D_MODEL), rw, rb, alpha, tm)

        top_i = route[:, TOP_K:2 * TOP_K].astype(jnp.int32)
        dest, block_e, n_used, nblk = _routing_tables(top_i)
        dest2 = dest.reshape(nt // TOK_TILE, TOK_TILE * TOP_K)
        xb = _dispatch(dest2, x1t, jnp.zeros((nblk * MOE_BLK * TOK_ROWS, LANE), F32))
        yb = _experts(l, block_e, n_used, xb, exp_w_gate, exp_w_up, exp_w_down,
                      exp_b_gate, exp_b_up, exp_b_down)
        x, x_bf = _combine(dest2, route, x1, ln2_g[l].reshape(1, D_MODEL), ln2_b[l].reshape(1, D_MODEL),
                           yb, alpha)

        outs["gdn_p"].append(s_p)
        outs["ssm_p"].append(h_p.reshape(batch, SSM_HEADS, SSM_HEADDIM, SSM_DSTATE))
        tails_g = [lax.slice(proj, (b * seq + seq - (CONV_W - 1), 0), (b * seq + seq, GDN_CONV_CH))
                   for b in range(batch)]
        tails_s = [lax.slice(proj, (b * seq + seq - (CONV_W - 1), o_x - 2 * GDN_HEADS),
                             (b * seq + seq, o_x - 2 * GDN_HEADS + SSM_CONV_CH)) for b in range(batch)]
        outs["gconv_p"].append(jnp.stack(tails_g))
        outs["sconv_p"].append(jnp.stack(tails_s))
        outs["gconv_s"].append(jnp.concatenate(
            [state_gdn_conv[l][:, 1:], proj_s[:, None, :GDN_CONV_CH]], axis=1))
        xbc_off = o_x - 2 * GDN_HEADS
        outs["sconv_s"].append(jnp.concatenate(
            [state_ssm_conv[l][:, 1:], proj_s[:, None, xbc_off:xbc_off + SSM_CONV_CH]], axis=1))

    yp = x[:n_p].reshape(batch, seq, D_MODEL)
    ys = x[n_p:].reshape(dec, 1, D_MODEL)
    return (yp, ys, jnp.stack(outs["gdn_p"]), gdn_s, jnp.stack(outs["gconv_p"]),
            jnp.stack(outs["gconv_s"]), jnp.stack(outs["ssm_p"]),
            ssm_s.reshape(depth, dec, SSM_HEADS, SSM_HEADDIM, SSM_DSTATE),
            jnp.stack(outs["sconv_p"]), jnp.stack(outs["sconv_s"]))
```

```python
import functools

import jax
import jax.numpy as jnp
from jax import lax
from jax.experimental import pallas as pl
from jax.experimental.pallas import tpu as pltpu

F32 = jnp.float32
BF16 = jnp.bfloat16

D_MODEL = 1024
GDN_HEADS = 8
GDN_DK = 128
GDN_VDIM = 1024
GDN_CONV_CH = 3072
SSM_HEADS = 32
SSM_HEADDIM = 64
SSM_GROUPS = 4
SSM_DINNER = 2048
SSM_DSTATE = 128
SSM_CONV_CH = 3072
CONV_W = 4
N_EXPERTS = 32
TOP_K = 4
SWIGLU_ALPHA = 1.702
SWIGLU_LIMIT = 7.0
LN_EPS = 1e-5
RMS_EPS = 1e-6
L2_EPS = 1e-6
NEG_BIG = -1e30

W_BIG = 11264
SMALL_W = 128
COL_B, COL_A, COL_DT = 0, 8, 16

LANE = 128
SUBLANE = 8
CHUNK = 128
GDN_HB = 8
MOE_BLK = 512
TOK_TILE = 128
TOK_ROWS = D_MODEL // LANE
VMEM_LIMIT = 56 * 1024 * 1024


def _pick(n, cands):
    for c in cands:
        if n % c == 0:
            return c
    raise ValueError(f"no tile for {n}")


def _bf(x):
    return x.astype(BF16)


def _dot(a, b, prec=None):
    return jnp.dot(a, b, preferred_element_type=F32, precision=prec)


def _dot_nt(a, b):
    return lax.dot_general(a, b, (((1,), (1,)), ((), ())), preferred_element_type=F32)


def _dot_tn(a, b):
    return lax.dot_general(a, b, (((0,), (0,)), ((), ())), preferred_element_type=F32)


def _split3(x):
    hi = _bf(x)
    r = x - hi.astype(F32)
    mid = _bf(r)
    return hi, mid, _bf(r - mid.astype(F32))


def _dot_sel_rhs(x, sel):
    hi, mid, lo = _split3(x)
    return (_dot(lo, sel) + _dot(mid, sel)) + _dot(hi, sel)


def _dot_sel_lhs(sel, x):
    hi, mid, lo = _split3(x)
    return (_dot(sel, lo) + _dot(sel, mid)) + _dot(sel, hi)


def _sigmoid(x):
    return jax.nn.sigmoid(x)


def _silu(x):
    return x * jax.nn.sigmoid(x)


def _softplus(x):
    return jnp.maximum(x, 0.0) + jnp.log(1.0 + jnp.exp(-jnp.abs(x)))


def _layer_norm(x, g, b):
    mu = jnp.mean(x, axis=-1, keepdims=True)
    xc = x - mu
    var = jnp.mean(xc * xc, axis=-1, keepdims=True)
    return xc * lax.rsqrt(var + LN_EPS) * g + b


def _small_act(raw, pv):
    col = lax.broadcasted_iota(jnp.int32, raw.shape, 1)
    sp = _softplus(raw + pv[0:1, :])
    act = jnp.where(col < COL_A, _sigmoid(raw), sp)
    gda = sp * (-jnp.exp(pv[1:2, :]))
    return act, gda


def _mm_kernel(x_ref, w_ref, o_ref):
    o_ref[...] = _dot(x_ref[...], w_ref[...])


def _matmul(x, w, tm, tn):
    m, k = x.shape
    n = w.shape[1]
    return pl.pallas_call(
        _mm_kernel,
        out_shape=jax.ShapeDtypeStruct((m, n), F32),
        grid=(n // tn, m // tm),
        in_specs=[pl.BlockSpec((tm, k), lambda j, i: (i, 0)),
                  pl.BlockSpec((k, tn), lambda j, i: (0, j))],
        out_specs=pl.BlockSpec((tm, tn), lambda j, i: (i, j)),
        compiler_params=pltpu.CompilerParams(
            dimension_semantics=("parallel", "parallel"), vmem_limit_bytes=VMEM_LIMIT),
        name="in_proj",
    )(x, w)


def _tri_inv_all(mats, ii, jj, c):
    eye = (ii == jj).astype(F32)
    pair = (ii >> 1) == (jj >> 1)
    ts = [eye - jnp.where(pair, a, 0.0) for a in mats]
    abs_ = [_bf(a) for a in mats]
    s = 1
    while (2 << s) <= c:
        same_outer = (ii >> (s + 1)) == (jj >> (s + 1))
        same_inner = (ii >> s) == (jj >> s)
        off = _bf((same_outer & jnp.logical_not(same_inner)).astype(F32))
        tbs = [_bf(t) for t in ts]
        tes = [_dot(tb, ab * off) for tb, ab in zip(tbs, abs_)]
        ts = [t - _dot(_bf(te), tb) for t, te, tb in zip(ts, tes, tbs)]
        s += 1
    return ts


def _gdn_kernel(q_ref, k_ref, v_ref, zg_ref, small_ref, cwq_ref, cwk_ref, cwv_ref, pv_ref, nw_ref,
                o_ref, sfin_ref, xf, tail, s_scr):
    assert GDN_HB == GDN_HEADS
    c_len = q_ref.shape[0]
    c = pl.program_id(2)
    nc = pl.num_programs(2)

    @pl.when(c == 0)
    def _():
        tail[...] = jnp.zeros_like(tail)
        s_scr[...] = jnp.zeros_like(s_scr)

    for p, r in enumerate((q_ref, k_ref, v_ref)):
        xf[p, 0:SUBLANE, :] = tail[p]
        xf[p, SUBLANE:SUBLANE + c_len, :] = r[...]
        tail[p] = r[c_len - SUBLANE:c_len, :]
    cws = (cwq_ref, cwk_ref, cwv_ref)

    def conv(p, hs):
        acc = None
        for i in range(CONV_W):
            term = cws[p][i:i + 1, hs] * xf[p, pl.ds(SUBLANE - (CONV_W - 1) + i, c_len), hs]
            acc = term if acc is None else acc + term
        return _silu(acc)

    act, gda = _small_act(small_ref[...], pv_ref[...])
    ii = lax.broadcasted_iota(jnp.int32, (c_len, c_len), 0)
    jj = lax.broadcasted_iota(jnp.int32, (c_len, c_len), 1)
    incl = ii >= jj
    strict = ii > jj
    gcum = _dot_sel_lhs(_bf(incl.astype(F32)), gda)
    gcum_t = gcum.T
    nw = nw_ref[...]

    heads = range(GDN_HB)
    hsl = [slice(hh * LANE, (hh + 1) * LANE) for hh in heads]
    betas = [act[:, COL_B + hh:COL_B + hh + 1] for hh in heads]
    gcs = [gcum[:, COL_A + hh:COL_A + hh + 1] for hh in heads]
    grs = [gcum_t[COL_A + hh:COL_A + hh + 1, :] for hh in heads]
    ks = []
    for hh in heads:
        k = conv(1, hsl[hh])
        ks.append(k * lax.rsqrt(jnp.sum(k * k, axis=-1, keepdims=True) + L2_EPS))
    kbs = [ks[hh] * betas[hh] for hh in heads]
    kbfs = [_bf(k) for k in ks]
    kks = [_dot_nt(_bf(kbs[hh]), kbfs[hh]) for hh in heads]
    gams = [jnp.exp(jnp.where(incl, gcs[hh] - grs[hh], NEG_BIG)) for hh in heads]
    amats = [jnp.where(strict, kks[hh] * gams[hh], 0.0) for hh in heads]
    qs = []
    for hh in heads:
        q = conv(0, hsl[hh])
        qs.append(q * lax.rsqrt(jnp.sum(q * q, axis=-1, keepdims=True) + L2_EPS) * (GDN_DK ** -0.5))
    qks = [_dot_nt(_bf(qs[hh]), kbfs[hh]) * gams[hh] for hh in heads]
    egs = [jnp.exp(gcs[hh]) for hh in heads]
    rhs = [_bf(jnp.concatenate([conv(2, hsl[hh]) * betas[hh], kbs[hh] * egs[hh]], axis=1)) for hh in heads]
    ts = _tri_inv_all(amats, ii, jj, c_len)
    uws = [_dot(_bf(ts[hh]), rhs[hh]) for hh in heads]
    glasts = [gcs[hh][c_len - 1:c_len, :] for hh in heads]
    s_olds = [s_scr[hh] for hh in heads]
    sbs = [_bf(s) for s in s_olds]
    v_news = [uws[hh][:, :LANE] - _dot(_bf(uws[hh][:, LANE:]), sbs[hh]) for hh in heads]
    vnbs = [_bf(v) for v in v_news]
    os_ = [_dot(_bf(qs[hh] * egs[hh]), sbs[hh]) + _dot(_bf(qks[hh]), vnbs[hh]) for hh in heads]
    for hh in heads:
        kdec = ks[hh] * jnp.exp(glasts[hh] - gcs[hh])
        s_scr[hh] = s_olds[hh] * jnp.exp(glasts[hh]) + _dot_tn(_bf(kdec), vnbs[hh])
    for hh in heads:
        o = os_[hh]
        o = (o * lax.rsqrt(jnp.mean(o * o, axis=-1, keepdims=True) + RMS_EPS) * nw
             * _silu(zg_ref[:, hsl[hh]]))
        o_ref[:, hsl[hh]] = _bf(o)

    @pl.when(c == nc - 1)
    def _():
        sfin_ref[0] = s_scr[...]


def _gdn_prompt(proj, small, conv_w, pv, norm_w, batch, seq):
    nc = seq // CHUNK
    hbw = GDN_HB * LANE
    ngrp = GDN_HEADS // GDN_HB
    kq, kk, kv, kz = 0, GDN_VDIM // hbw, 2 * GDN_VDIM // hbw, 3 * GDN_VDIM // hbw

    def row(b, hg, c):
        return b * nc + c

    def pspec(off):
        return pl.BlockSpec((CHUNK, hbw), lambda b, hg, c: (row(b, hg, c), off + hg))

    def wspec(off):
        return pl.BlockSpec((CONV_W, hbw), lambda b, hg, c: (0, off + hg))

    return pl.pallas_call(
        _gdn_kernel,
        out_shape=(jax.ShapeDtypeStruct((batch * seq, GDN_VDIM), BF16),
                   jax.ShapeDtypeStruct((batch, GDN_HEADS, GDN_DK, LANE), F32)),
        grid=(batch, ngrp, nc),
        in_specs=[pspec(kq), pspec(kk), pspec(kv), pspec(kz),
                  pl.BlockSpec((CHUNK, SMALL_W), lambda b, hg, c: (row(b, hg, c), 0)),
                  wspec(kq), wspec(kk), wspec(kv),
                  pl.BlockSpec((SUBLANE, SMALL_W), lambda b, hg, c: (0, 0)),
                  pl.BlockSpec((1, LANE), lambda b, hg, c: (0, 0))],
        out_specs=(pl.BlockSpec((CHUNK, hbw), lambda b, hg, c: (row(b, hg, c), hg)),
                   pl.BlockSpec((1, GDN_HB, GDN_DK, LANE), lambda b, hg, c: (b, hg, 0, 0))),
        scratch_shapes=[pltpu.VMEM((3, CHUNK + SUBLANE, hbw), F32),
                        pltpu.VMEM((3, SUBLANE, hbw), F32),
                        pltpu.VMEM((GDN_HB, GDN_DK, LANE), F32)],
        compiler_params=pltpu.CompilerParams(
            dimension_semantics=("parallel", "parallel", "arbitrary"), vmem_limit_bytes=VMEM_LIMIT),
        name="gdn_prompt",
    )(proj, proj, proj, proj, small, conv_w, conv_w, conv_w, pv, norm_w)


def _ssd_kernel(xs_ref, b_ref, c_ref, zs_ref, small_ref, cwx_ref, cwb_ref, cwc_ref, cbx_ref, cbb_ref,
                cbc_ref, pv_ref, ex_ref, dexp_ref, nw_ref, yz_ref, hfin_ref, xf, tail, ht, ydiag):
    c_len = xs_ref.shape[0]
    gw = SSM_DINNER // SSM_GROUPS
    hpg = gw // SSM_HEADDIM
    gn = SSM_GROUPS * SSM_DSTATE
    c = pl.program_id(1)
    nc = pl.num_programs(1)

    @pl.when(c == 0)
    def _():
        tail[...] = jnp.zeros_like(tail)
        ht[...] = jnp.zeros_like(ht)

    parts = ((xs_ref, cwx_ref, cbx_ref, 0, SSM_DINNER), (b_ref, cwb_ref, cbb_ref, SSM_DINNER, gn),
             (c_ref, cwc_ref, cbc_ref, SSM_DINNER + gn, gn))
    convs = []
    for r, cw, cb, off, wd in parts:
        sl = slice(off, off + wd)
        xf[0:SUBLANE, sl] = tail[:, sl]
        xf[SUBLANE:SUBLANE + c_len, sl] = r[...]
        tail[:, sl] = r[c_len - SUBLANE:c_len, :]
        acc = cb[...]
        for i in range(CONV_W):
            acc = acc + cw[i:i + 1, :] * xf[pl.ds(SUBLANE - (CONV_W - 1) + i, c_len), sl]
        convs.append(_silu(acc))
    xs, bm_all, cm_all = convs

    act, gda = _small_act(small_ref[...], pv_ref[...])
    ii = lax.broadcasted_iota(jnp.int32, (c_len, c_len), 0)
    jj = lax.broadcasted_iota(jnp.int32, (c_len, c_len), 1)
    incl = ii >= jj
    acs = _dot_sel_lhs(_bf(incl.astype(F32)), gda)
    acs_t = acs.T
    ex = ex_ref[...]
    dt_x = _dot_sel_rhs(act, ex)
    acs_x = _dot_sel_rhs(acs, ex)
    last = acs_x[c_len - 1:c_len, :]
    xdt = xs * dt_x
    xdec = _bf(xdt * jnp.exp(last - acs_x))
    lane = lax.broadcasted_iota(jnp.int32, (c_len, LANE), 1)
    lo_half = lane < SSM_HEADDIM

    groups = range(SSM_GROUPS)
    bms = [_bf(bm_all[:, g * SSM_DSTATE:(g + 1) * SSM_DSTATE]) for g in groups]
    cms = [_bf(cm_all[:, g * SSM_DSTATE:(g + 1) * SSM_DSTATE]) for g in groups]
    cbs = [_dot_nt(cms[g], bms[g]) for g in groups]
    h_olds = [ht[:, g * gw:(g + 1) * gw] for g in groups]
    y_offs = [_dot(cms[g], _bf(h_olds[g])) for g in groups]
    for g in groups:
        for pr in range(hpg // 2):
            ps = slice(g * gw + pr * LANE, g * gw + (pr + 1) * LANE)
            xpair = xdt[:, ps]
            acc = None
            for half in range(2):
                head = g * hpg + pr * 2 + half
                ac = acs[:, COL_DT + head:COL_DT + head + 1]
                ar = acs_t[COL_DT + head:COL_DT + head + 1, :]
                sc = cbs[g] * jnp.exp(jnp.where(incl, ac - ar, NEG_BIG))
                keep = lo_half if half == 0 else jnp.logical_not(lo_half)
                term = _dot(_bf(sc), _bf(jnp.where(keep, xpair, 0.0)))
                acc = term if acc is None else acc + term
            ydiag[:, ps] = acc
    for g in groups:
        gs = slice(g * gw, (g + 1) * gw)
        ht[:, gs] = h_olds[g] * jnp.exp(last[:, gs]) + _dot_tn(bms[g], xdec[:, gs])
    y = ydiag[...] + jnp.concatenate(y_offs, axis=1) * jnp.exp(acs_x) + dexp_ref[...] * xs
    yz = y * _silu(zs_ref[...])
    nw = nw_ref[...]
    for g in groups:
        gs = slice(g * gw, (g + 1) * gw)
        yg = yz[:, gs]
        yz_ref[:, gs] = _bf(yg * lax.rsqrt(jnp.mean(yg * yg, axis=-1, keepdims=True) + RMS_EPS) * nw[:, gs])

    @pl.when(c == nc - 1)
    def _():
        hfin_ref[0] = ht[...].T


def _ssd_prompt(proj, small, conv_w, conv_b, pv, expand, d_exp, norm_w, batch, seq):
    nc = seq // CHUNK
    gn = SSM_GROUPS * SSM_DSTATE
    x_off = (GDN_CONV_CH + GDN_VDIM + SSM_DINNER)
    z_off = GDN_CONV_CH + GDN_VDIM
    kx = x_off // SSM_DINNER
    kb = (x_off + SSM_DINNER) // gn
    kz = z_off // SSM_DINNER
    wb = SSM_DINNER // gn

    def row(b, c):
        return b * nc + c

    in_specs = [
        pl.BlockSpec((CHUNK, SSM_DINNER), lambda b, c: (row(b, c), kx)),
        pl.BlockSpec((CHUNK, gn), lambda b, c: (row(b, c), kb)),
        pl.BlockSpec((CHUNK, gn), lambda b, c: (row(b, c), kb + 1)),
        pl.BlockSpec((CHUNK, SSM_DINNER), lambda b, c: (row(b, c), kz)),
        pl.BlockSpec((CHUNK, SMALL_W), lambda b, c: (row(b, c), 0)),
        pl.BlockSpec((CONV_W, SSM_DINNER), lambda b, c: (0, 0)),
        pl.BlockSpec((CONV_W, gn), lambda b, c: (0, wb)),
        pl.BlockSpec((CONV_W, gn), lambda b, c: (0, wb + 1)),
        pl.BlockSpec((1, SSM_DINNER), lambda b, c: (0, 0)),
        pl.BlockSpec((1, gn), lambda b, c: (0, wb)),
        pl.BlockSpec((1, gn), lambda b, c: (0, wb + 1)),
        pl.BlockSpec((SUBLANE, SMALL_W), lambda b, c: (0, 0)),
        pl.BlockSpec((SMALL_W, SSM_DINNER), lambda b, c: (0, 0)),
        pl.BlockSpec((1, SSM_DINNER), lambda b, c: (0, 0)),
        pl.BlockSpec((1, SSM_DINNER), lambda b, c: (0, 0)),
    ]
    return pl.pallas_call(
        _ssd_kernel,
        out_shape=(jax.ShapeDtypeStruct((batch * seq, SSM_DINNER), BF16),
                   jax.ShapeDtypeStruct((batch, SSM_DINNER, SSM_DSTATE), F32)),
        grid=(batch, nc),
        in_specs=in_specs,
        out_specs=(pl.BlockSpec((CHUNK, SSM_DINNER), lambda b, c: (row(b, c), 0)),
                   pl.BlockSpec((1, SSM_DINNER, SSM_DSTATE), lambda b, c: (b, 0, 0))),
        scratch_shapes=[pltpu.VMEM((CHUNK + SUBLANE, SSM_CONV_CH), F32),
                        pltpu.VMEM((SUBLANE, SSM_CONV_CH), F32),
                        pltpu.VMEM((SSM_DSTATE, SSM_DINNER), F32),
                        pltpu.VMEM((CHUNK, SSM_DINNER), F32)],
        compiler_params=pltpu.CompilerParams(
            dimension_semantics=("parallel", "arbitrary"), vmem_limit_bytes=VMEM_LIMIT),
        name="ssd_prompt",
    )(proj, proj, proj, proj, small, conv_w, conv_w, conv_w, conv_b, conv_b, conv_b, pv, expand,
      d_exp, norm_w)


def _sample_prep_kernel(small_ref, pv_ref, ex_ref, act_ref, gda_ref, dtx_ref, dax_ref):
    act, gda = _small_act(small_ref[...], pv_ref[...])
    act_ref[...] = act
    gda_ref[...] = gda
    ex = ex_ref[...]
    dtx_ref[...] = _dot_sel_rhs(act, ex)
    dax_ref[...] = _dot_sel_rhs(gda, ex)


def _sample_prep(small_s, pv, expand_all):
    n = small_s.shape[0]
    return pl.pallas_call(
        _sample_prep_kernel,
        out_shape=(jax.ShapeDtypeStruct((n, SMALL_W), F32), jax.ShapeDtypeStruct((n, SMALL_W), F32),
                   jax.ShapeDtypeStruct((n, SSM_DINNER), F32), jax.ShapeDtypeStruct((n, SSM_DINNER), F32)),
        name="sample_prep",
    )(small_s, pv, expand_all)


def _sample_kernel(proj_ref, act_ref, gda_ref, dtx_ref, dax_ref, gcs_ref, scs_ref, s0_ref, h0_ref, gcw_ref,
                   scw_ref, scb_ref, gnw_ref, dexp_ref, snw_ref, *rest, n_alias):
    og_ref, yz_ref, s_ref, h_ref, stk, stk2 = rest[n_alias:]
    if n_alias == 0:
        if s_ref.shape[0] > 1:
            s_ref[1:] = jnp.zeros((s_ref.shape[0] - 1,) + s_ref.shape[1:], F32)
            h_ref[1:] = jnp.zeros((h_ref.shape[0] - 1,) + h_ref.shape[1:], F32)
        s_ref = s_ref.at[0]
        h_ref = h_ref.at[0]
    gw = SSM_DINNER // SSM_GROUPS
    z_off = GDN_CONV_CH + GDN_VDIM
    x_off = z_off + SSM_DINNER

    def conv1(state_ref, w_ref, new_row):
        acc = w_ref[CONV_W - 1:CONV_W, :] * new_row
        for i in range(CONV_W - 1):
            acc = acc + w_ref[i:i + 1, :] * state_ref[i:i + 1, :]
        return acc

    act = act_ref[...]
    gda = gda_ref[...]

    qkv = _silu(conv1(gcs_ref, gcw_ref, proj_ref[:, 0:GDN_CONV_CH]))
    stk[...] = jnp.zeros_like(stk)
    qs, vs = [], []
    for h in range(GDN_HEADS):
        q = qkv[:, h * LANE:(h + 1) * LANE]
        k = qkv[:, GDN_VDIM + h * LANE:GDN_VDIM + (h + 1) * LANE]
        q = q * lax.rsqrt(jnp.sum(q * q, axis=-1, keepdims=True) + L2_EPS) * (GDN_DK ** -0.5)
        k = k * lax.rsqrt(jnp.sum(k * k, axis=-1, keepdims=True) + L2_EPS)
        stk[h:h + 1, :] = k
        stk[GDN_HEADS + h:GDN_HEADS + h + 1, :] = q
        vs.append(qkv[:, 2 * GDN_VDIM + h * LANE:2 * GDN_VDIM + (h + 1) * LANE])
    cols = stk[...].T
    gnw = gnw_ref[...]
    for h in range(GDN_HEADS):
        kc = cols[:, h:h + 1]
        qc = cols[:, GDN_HEADS + h:GDN_HEADS + h + 1]
        beta = act[:, COL_B + h:COL_B + h + 1]
        gh = gda[:, COL_A + h:COL_A + h + 1]
        sd = s0_ref[h] * jnp.exp(gh)
        v_old = jnp.sum(sd * kc, axis=0, keepdims=True)
        delta = (vs[h] - v_old) * beta
        s_new = sd + kc * delta
        s_ref[h] = s_new
        o = jnp.sum(s_new * qc, axis=0, keepdims=True)
        zg = proj_ref[:, GDN_CONV_CH + h * LANE:GDN_CONV_CH + (h + 1) * LANE]
        o = o * lax.rsqrt(jnp.mean(o * o, axis=-1, keepdims=True) + RMS_EPS) * gnw * _silu(zg)
        og_ref[:, h * LANE:(h + 1) * LANE] = _bf(o)

    xbc = _silu(conv1(scs_ref, scw_ref, proj_ref[:, x_off:x_off + SSM_CONV_CH]) + scb_ref[...])
    dexp = dexp_ref[...]
    stk2[...] = jnp.zeros_like(stk2)
    for g in range(SSM_GROUPS):
        gs = slice(g * gw, (g + 1) * gw)
        dt_x = dtx_ref[:, gs]
        da_x = dax_ref[:, gs]
        xs = xbc[:, gs]
        bm = xbc[:, SSM_DINNER + g * SSM_DSTATE:SSM_DINNER + (g + 1) * SSM_DSTATE]
        cm = xbc[:, SSM_DINNER + SSM_GROUPS * SSM_DSTATE + g * SSM_DSTATE:
                 SSM_DINNER + SSM_GROUPS * SSM_DSTATE + (g + 1) * SSM_DSTATE]
        stk2[0:1, :] = jnp.exp(da_x)
        stk2[1:2, :] = xs * dt_x
        cols2 = stk2[...].T
        h_new = h0_ref[gs, :] * cols2[:, 0:1] + cols2[:, 1:2] * bm
        h_ref[gs, :] = h_new
        cm16 = jnp.broadcast_to(cm, (2 * SUBLANE, SSM_DSTATE))
        y = _dot_nt(_bf(cm16), _bf(h_new))[0:1, :] + dexp[:, gs] * xs
        yz = y * _silu(proj_ref[:, z_off + g * gw:z_off + (g + 1) * gw])
        yz = yz * lax.rsqrt(jnp.mean(yz * yz, axis=-1, keepdims=True) + RMS_EPS) * snw_ref[:, gs]
        yz_ref[:, gs] = _bf(yz)


def _sample_mix(layer, proj_s, small_s, gconv_state, sconv_state, s_all, h_all, s_prev, h_prev, gconv_w,
                sconv_w, sconv_b, pv, gnorm_w, expand_all, d_exp, snorm_w):
    n = proj_s.shape[0]
    gw = SSM_DINNER // SSM_GROUPS
    proj3 = proj_s.reshape(n, 1, W_BIG)
    act, gda, dtx, dax = _sample_prep(small_s, pv, expand_all)

    def full(shape):
        nd = len(shape)
        return pl.BlockSpec(shape, lambda i: (0,) * nd)

    def per_seq(shape):
        nd = len(shape)
        return pl.BlockSpec((None,) + shape, lambda i: (i,) + (0,) * nd)

    def per_layer_seq(shape):
        nd = len(shape)
        return pl.BlockSpec((None, None) + shape, lambda i: (layer, i) + (0,) * nd)

    in_specs = [per_seq((1, W_BIG)), per_seq((1, SMALL_W)), per_seq((1, SMALL_W)),
                per_seq((1, SSM_DINNER)), per_seq((1, SSM_DINNER)),
                per_seq((CONV_W - 1, GDN_CONV_CH)), per_seq((CONV_W - 1, SSM_CONV_CH)),
                per_layer_seq((GDN_HEADS, GDN_DK, LANE)), per_layer_seq((SSM_DINNER, SSM_DSTATE)),
                full((CONV_W, GDN_CONV_CH)), full((CONV_W, SSM_CONV_CH)), full((1, SSM_CONV_CH)),
                full((1, LANE)), full((1, SSM_DINNER)), full((1, SSM_DINNER))]
    args = [proj3, act.reshape(n, 1, SMALL_W), gda.reshape(n, 1, SMALL_W), dtx.reshape(n, 1, SSM_DINNER),
            dax.reshape(n, 1, SSM_DINNER), gconv_state, sconv_state, s_all, h_all, gconv_w, sconv_w, sconv_b,
            gnorm_w, d_exp, snorm_w]
    aliases = {}
    if s_prev is not None:
        aliases = {len(args): 2, len(args) + 1: 3}
        in_specs += [pl.BlockSpec(memory_space=pl.ANY), pl.BlockSpec(memory_space=pl.ANY)]
        args += [s_prev, h_prev]
    if s_prev is None:
        depth = s_all.shape[0]
        state_specs = [pl.BlockSpec((depth, None, GDN_HEADS, GDN_DK, LANE), lambda i: (0, i, 0, 0, 0)),
                       pl.BlockSpec((depth, None, SSM_DINNER, SSM_DSTATE), lambda i: (0, i, 0, 0))]
    else:
        state_specs = [per_layer_seq((GDN_HEADS, GDN_DK, LANE)), per_layer_seq((SSM_DINNER, SSM_DSTATE))]
    kern = functools.partial(_sample_kernel, n_alias=len(aliases))
    og, yz, s_new, h_new = pl.pallas_call(
        kern,
        out_shape=(jax.ShapeDtypeStruct((n, 1, GDN_VDIM), BF16),
                   jax.ShapeDtypeStruct((n, 1, SSM_DINNER), BF16),
                   jax.ShapeDtypeStruct(s_all.shape, F32),
                   jax.ShapeDtypeStruct(h_all.shape, F32)),
        grid=(n,),
        in_specs=in_specs,
        out_specs=(per_seq((1, GDN_VDIM)), per_seq((1, SSM_DINNER)), state_specs[0], state_specs[1]),
        scratch_shapes=[pltpu.VMEM((LANE, LANE), F32), pltpu.VMEM((LANE, gw), F32)],
        input_output_aliases=aliases,
        compiler_params=pltpu.CompilerParams(
            dimension_semantics=("parallel",), vmem_limit_bytes=VMEM_LIMIT),
        name="sample_mix",
    )(*args)
    return og.reshape(n, GDN_VDIM), yz.reshape(n, SSM_DINNER), s_new, h_new


def _merge_kernel(og_ref, yz_ref, ga_ref, gb_ref, x_ref, wbg_ref, wbs_ref, wout_ref, lng_ref, lnb_ref,
                  rwh_ref, rwl_ref, rb_ref, x1_ref, x1t_ref, route_ref, *, alpha):
    a = _dot(og_ref[...], wbg_ref[...])
    b = _dot(yz_ref[...], wbs_ref[...])
    merged = _sigmoid(ga_ref[...]) * a + _sigmoid(gb_ref[...]) * b
    mix = _dot(_bf(merged), wout_ref[...])
    x1 = _layer_norm(alpha * x_ref[...] + mix, lng_ref[...], lnb_ref[...])
    x1_ref[...] = x1
    _store_token_tiles(x1t_ref, x1)

    xh = _bf(x1)
    xl = _bf(x1 - xh.astype(F32))
    rwh = rwh_ref[...]
    lg = (_dot(xh, rwl_ref[...]) + _dot(xl, rwh)) + _dot(xh, rwh) + rb_ref[...]
    colf = lax.broadcasted_iota(jnp.int32, lg.shape, 1).astype(F32)
    vals, idxs = [], []
    for _ in range(TOP_K):
        m = jnp.max(lg, axis=-1, keepdims=True)
        idx = jnp.min(jnp.where(lg == m, colf, float(LANE)), axis=-1, keepdims=True)
        vals.append(m)
        idxs.append(idx)
        lg = jnp.where(colf == idx, 2.0 * NEG_BIG, lg)
    es = [jnp.exp(v - vals[0]) for v in vals]
    den = es[0] + es[1] + es[2] + es[3]
    route = jnp.zeros_like(lg)
    for kk in range(TOP_K):
        route = jnp.where(colf == float(kk), es[kk] / den, route)
        route = jnp.where(colf == float(TOP_K + kk), idxs[kk], route)
    route_ref[...] = route


def _merge(og, yz, proj, x, wbg, wbs, wout, lng, lnb, rw, rb, alpha, tm):
    nt = x.shape[0]
    ka = (W_BIG - 2 * D_MODEL) // D_MODEL
    kern = functools.partial(_merge_kernel, alpha=alpha)
    rw_hi = _bf(rw)

    def full(shape):
        return pl.BlockSpec(shape, lambda i: (0, 0))

    return pl.pallas_call(
        kern,
        out_shape=(jax.ShapeDtypeStruct((nt, D_MODEL), F32),
                   jax.ShapeDtypeStruct((nt * TOK_ROWS, LANE), F32),
                   jax.ShapeDtypeStruct((nt, LANE), F32)),
        grid=(nt // tm,),
        in_specs=[pl.BlockSpec((tm, GDN_VDIM), lambda i: (i, 0)),
                  pl.BlockSpec((tm, SSM_DINNER), lambda i: (i, 0)),
                  pl.BlockSpec((tm, D_MODEL), lambda i: (i, ka)),
                  pl.BlockSpec((tm, D_MODEL), lambda i: (i, ka + 1)),
                  pl.BlockSpec((tm, D_MODEL), lambda i: (i, 0)),
                  full((GDN_VDIM, D_MODEL)), full((SSM_DINNER, D_MODEL)), full((D_MODEL, D_MODEL)),
                  full((1, D_MODEL)), full((1, D_MODEL)), full((D_MODEL, LANE)), full((D_MODEL, LANE)),
                  full((1, LANE))],
        out_specs=(pl.BlockSpec((tm, D_MODEL), lambda i: (i, 0)),
                   pl.BlockSpec((tm * TOK_ROWS, LANE), lambda i: (i, 0)),
                   pl.BlockSpec((tm, LANE), lambda i: (i, 0))),
        compiler_params=pltpu.CompilerParams(
            dimension_semantics=("parallel",), vmem_limit_bytes=VMEM_LIMIT),
        name="merge_ln_router",
    )(og, yz, proj, proj, x, wbg, wbs, wout, lng, lnb, rw_hi, _bf(rw - rw_hi.astype(F32)), rb)


def _store_token_tiles(ref, val):
    n = val.shape[0]
    for cc in range(TOK_ROWS):
        ref[pl.ds(cc, n, stride=TOK_ROWS), :] = val[:, cc * LANE:(cc + 1) * LANE]


def _load_token_chunk(ref, cc, n):
    return ref[pl.ds(cc, n, stride=TOK_ROWS), :]


def _token_rows(t):
    return pl.ds(pl.multiple_of(t * TOK_ROWS, TOK_ROWS), TOK_ROWS)


def _expert_kernel(be_ref, nu_ref, tok_hbm, x_hbm, wg_ref, wu_ref, wd_ref, bg_ref, bu_ref, bd_ref, y_ref,
                   wgb, wub, wdb, idx, xbuf, sem_idx, sem):
    j = pl.program_id(0)
    n = pl.num_programs(0)
    n_used = nu_ref[0]
    e = be_ref[j]
    prev = be_ref[jnp.maximum(j - 1, 0)]
    used = j < n_used
    slot = j % 2
    nxt = 1 - slot

    def idx_copy(blk, s):
        return pltpu.make_async_copy(tok_hbm.at[blk], idx.at[s], sem_idx.at[s])

    def issue_gathers(s):
        def issue(rr, carry):
            for u in range(2):
                r = rr * 2 + u
                pltpu.make_async_copy(x_hbm.at[_token_rows(idx[s, r]), :], xbuf.at[s, _token_rows(r), :],
                                      sem.at[s]).start(priority=u)
            return carry

        lax.fori_loop(0, MOE_BLK // 2, issue, 0, unroll=8)

    @pl.when(j == 0)
    def _():
        first = idx_copy(0, 0)
        first.start()
        first.wait()
        issue_gathers(0)

        @pl.when(n > 1)
        def _():
            idx_copy(1, 1).start()

    @pl.when(j + 1 < n)
    def _():
        idx_copy(j + 1, nxt).wait()

        @pl.when(j + 1 < n_used)
        def _():
            issue_gathers(nxt)

    @pl.when(j + 2 < n)
    def _():
        idx_copy(j + 2, slot).start()

    @pl.when(used & ((j == 0) | (e != prev)))
    def _():
        wgb[...] = _bf(wg_ref[...])
        wub[...] = _bf(wu_ref[...])
        wdb[...] = _bf(wd_ref[...])

    @pl.when(used)
    def _():
        pltpu.make_async_copy(x_hbm.at[pl.ds(0, MOE_BLK * TOK_ROWS), :], xbuf.at[slot], sem.at[slot]).wait()
        x_ref = xbuf.at[slot]
        x = jnp.concatenate([_bf(_load_token_chunk(x_ref, cc, MOE_BLK)) for cc in range(TOK_ROWS)], axis=1)
        gt = _dot(x, wgb[...]) + bg_ref[...]
        up = _dot(x, wub[...]) + bu_ref[...]
        gt = jnp.minimum(gt, SWIGLU_LIMIT)
        up = jnp.clip(up, -SWIGLU_LIMIT, SWIGLU_LIMIT)
        h = (up + 1.0) * (gt * _sigmoid(SWIGLU_ALPHA * gt))
        _store_token_tiles(y_ref, _dot(_bf(h), wdb[...]) + bd_ref[...])

    @pl.when(jnp.logical_not(used))
    def _():
        y_ref[...] = jnp.zeros_like(y_ref)


def _experts(layer, block_e, n_used, row_tok2, x1t, wg, wu, wd, bg, bu, bd):
    nblk = row_tok2.shape[0]
    blk_rows = MOE_BLK * TOK_ROWS
    rows = nblk * blk_rows
    depth = wg.shape[0]
    d_e = wg.shape[-1]
    wspec_in = pl.BlockSpec((None, None, D_MODEL, d_e), lambda j, be, nu: (layer, be[j], 0, 0))
    wspec_out = pl.BlockSpec((None, None, d_e, D_MODEL), lambda j, be, nu: (layer, be[j], 0, 0))
    bspec_e = pl.BlockSpec((None, None, 1, d_e), lambda j, be, nu: (layer, be[j], 0, 0))
    bspec_d = pl.BlockSpec((None, None, 1, D_MODEL), lambda j, be, nu: (layer, be[j], 0, 0))
    grid_spec = pltpu.PrefetchScalarGridSpec(
        num_scalar_prefetch=2,
        grid=(nblk,),
        in_specs=[pl.BlockSpec(memory_space=pl.ANY), pl.BlockSpec(memory_space=pl.ANY),
                  wspec_in, wspec_in, wspec_out, bspec_e, bspec_e, bspec_d],
        out_specs=pl.BlockSpec((blk_rows, LANE), lambda j, be, nu: (j, 0)),
        scratch_shapes=[pltpu.VMEM((D_MODEL, d_e), BF16), pltpu.VMEM((D_MODEL, d_e), BF16),
                        pltpu.VMEM((d_e, D_MODEL), BF16),
                        pltpu.SMEM((2, MOE_BLK), jnp.int32),
                        pltpu.VMEM((2, blk_rows, LANE), F32),
                        pltpu.SemaphoreType.DMA((2,)), pltpu.SemaphoreType.DMA((2,))],
    )
    return pl.pallas_call(
        _expert_kernel,
        out_shape=jax.ShapeDtypeStruct((rows, LANE), F32),
        grid_spec=grid_spec,
        compiler_params=pltpu.CompilerParams(
            dimension_semantics=("arbitrary",), vmem_limit_bytes=VMEM_LIMIT),
        name="moe_experts",
    )(block_e, n_used, row_tok2, x1t, wg, wu, wd, bg.reshape(depth, N_EXPERTS, 1, d_e),
      bu.reshape(depth, N_EXPERTS, 1, d_e), bd.reshape(depth, N_EXPERTS, 1, D_MODEL))


def _combine_kernel(dest_hbm, gates_ref, x1_ref, lng_ref, lnb_ref, yb_hbm, y_ref, ybf_ref, idx, buf,
                    sem_idx, sem, *, alpha):
    i = pl.program_id(0)
    n = pl.num_programs(0)
    slot = i % 2
    nxt = 1 - slot

    def idx_copy(tile, s):
        return pltpu.make_async_copy(dest_hbm.at[tile], idx.at[s], sem_idx.at[s])

    def issue_gathers(s):
        def issue(r, carry):
            for kk in range(TOP_K):
                pltpu.make_async_copy(yb_hbm.at[_token_rows(idx[s, r * TOP_K + kk]), :],
                                      buf.at[s, kk, _token_rows(r), :], sem.at[s]).start(priority=kk % 2)
            return carry

        lax.fori_loop(0, TOK_TILE, issue, 0, unroll=8)

    @pl.when(i == 0)
    def _():
        first = idx_copy(0, 0)
        first.start()
        first.wait()
        issue_gathers(0)

        @pl.when(n > 1)
        def _():
            idx_copy(1, 1).start()

    @pl.when(i + 1 < n)
    def _():
        idx_copy(i + 1, nxt).wait()
        issue_gathers(nxt)

    @pl.when(i + 2 < n)
    def _():
        idx_copy(i + 2, slot).start()

    for kk in range(TOP_K):
        pltpu.make_async_copy(yb_hbm.at[pl.ds(0, TOK_TILE * TOK_ROWS), :], buf.at[slot, kk], sem.at[slot]).wait()

    gates = gates_ref[...]
    chunks = []
    for cc in range(TOK_ROWS):
        acc = gates[:, 0:1] * _load_token_chunk(buf.at[slot, 0], cc, TOK_TILE)
        for kk in range(1, TOP_K):
            acc = acc + gates[:, kk:kk + 1] * _load_token_chunk(buf.at[slot, kk], cc, TOK_TILE)
        chunks.append(acc)
    moe = jnp.concatenate(chunks, axis=1)
    y = _layer_norm(alpha * x1_ref[...] + moe, lng_ref[...], lnb_ref[...])
    y_ref[...] = y
    ybf_ref[...] = _bf(y)


def _combine(dest2, route, x1, lng, lnb, yb, alpha):
    nt = x1.shape[0]
    kern = functools.partial(_combine_kernel, alpha=alpha)
    return pl.pallas_call(
        kern,
        out_shape=(jax.ShapeDtypeStruct((nt, D_MODEL), F32),
                   jax.ShapeDtypeStruct((nt, D_MODEL), BF16)),
        grid=(nt // TOK_TILE,),
        in_specs=[pl.BlockSpec(memory_space=pl.ANY),
                  pl.BlockSpec((TOK_TILE, LANE), lambda i: (i, 0)),
                  pl.BlockSpec((TOK_TILE, D_MODEL), lambda i: (i, 0)),
                  pl.BlockSpec((1, D_MODEL), lambda i: (0, 0)),
                  pl.BlockSpec((1, D_MODEL), lambda i: (0, 0)),
                  pl.BlockSpec(memory_space=pl.ANY)],
        out_specs=(pl.BlockSpec((TOK_TILE, D_MODEL), lambda i: (i, 0)),
                   pl.BlockSpec((TOK_TILE, D_MODEL), lambda i: (i, 0))),
        scratch_shapes=[pltpu.SMEM((2, TOK_TILE * TOP_K), jnp.int32),
                        pltpu.VMEM((2, TOP_K, TOK_TILE * TOK_ROWS, LANE), F32),
                        pltpu.SemaphoreType.DMA((2,)), pltpu.SemaphoreType.DMA((2,))],
        compiler_params=pltpu.CompilerParams(
            dimension_semantics=("arbitrary",), vmem_limit_bytes=VMEM_LIMIT),
        name="moe_combine_ln",
    )(dest2, route, x1, lng, lnb, yb)


def _routing_tables(top_i):
    m = top_i.size
    flat_e = top_i.reshape(-1)
    onehot = (flat_e[:, None] == jnp.arange(N_EXPERTS, dtype=jnp.int32)[None, :]).astype(jnp.int32)
    oh3 = onehot.reshape(m // LANE, LANE, N_EXPERTS)
    tri = jnp.tril(jnp.ones((LANE, LANE), F32))
    within = jnp.einsum("ij,tjk->tik", tri, oh3.astype(F32)).astype(jnp.int32)
    tile_tot = within[:, -1, :]
    tile_off = jnp.cumsum(tile_tot, axis=0) - tile_tot
    csum = (within + tile_off[:, None, :]).reshape(m, N_EXPERTS)
    rank = jnp.sum(onehot * csum, axis=1) - 1
    counts = csum[-1]
    padded = (counts + MOE_BLK - 1) // MOE_BLK * MOE_BLK
    pad_ends = jnp.cumsum(padded)
    pad_starts = pad_ends - padded
    dest = jnp.sum(onehot * pad_starts[None, :], axis=1) + rank
    nblk = m // MOE_BLK + N_EXPERTS
    blk_start = jnp.arange(nblk, dtype=jnp.int32) * MOE_BLK
    block_e = jnp.minimum(jnp.sum((blk_start[:, None] >= pad_ends[None, :]).astype(jnp.int32), axis=1),
                          N_EXPERTS - 1)
    n_used = (pad_ends[-1] // MOE_BLK).astype(jnp.int32).reshape(1)
    order = jnp.argsort(dest).astype(jnp.int32)
    starts = jnp.cumsum(counts) - counts
    seg_pos = (blk_start - pad_starts[block_e])[:, None] + jnp.arange(MOE_BLK, dtype=jnp.int32)[None, :]
    valid = seg_pos < counts[block_e][:, None]
    src = jnp.clip(starts[block_e][:, None] + seg_pos, 0, m - 1)
    row_tok = jnp.where(valid, order[src] // TOP_K, 0).astype(jnp.int32)
    return dest.astype(jnp.int32), block_e.astype(jnp.int32), n_used, row_tok


def kernel(x_prompt, x_sample, state_gdn, state_gdn_conv, state_ssm, state_ssm_conv, w_in, gdn_conv_w,
           gdn_a_log, gdn_dt_bias, gdn_norm_w, ssm_conv_w, ssm_conv_b, ssm_a_log, ssm_dt_bias, ssm_d,
           ssm_norm_w, w_br_gdn, w_br_ssm, w_out, ln1_g, ln1_b, router_w, router_b, exp_w_gate,
           exp_b_gate, exp_w_up, exp_b_up, exp_w_down, exp_b_down, ln2_g, ln2_b):
    batch, seq, _ = x_prompt.shape
    dec = x_sample.shape[0]
    depth = w_in.shape[0]
    n_p = batch * seq
    nt = n_p + dec
    alpha = (2.0 * depth) ** 0.25
    gw = SSM_DINNER // SSM_GROUPS
    tm = _pick(nt, (384, 256, 128, 64, 32, 16))
    assert seq % CHUNK == 0 and nt % TOK_TILE == 0 and (nt * TOP_K) % MOE_BLK == 0

    x = jnp.concatenate([x_prompt.reshape(n_p, D_MODEL), x_sample.reshape(dec, D_MODEL)], axis=0)
    x_bf = _bf(x)

    o_zg = GDN_CONV_CH
    o_b = o_zg + GDN_VDIM
    o_a = o_b + GDN_HEADS
    o_zs = o_a + GDN_HEADS
    o_x = o_zs + SSM_DINNER
    o_dt = o_x + SSM_CONV_CH
    o_ga = o_dt + SSM_HEADS

    rows = jnp.arange(SMALL_W, dtype=jnp.int32)[:, None]
    lanes = jnp.arange(SSM_DINNER, dtype=jnp.int32)[None, :]
    expand_all = (rows == COL_DT + lanes // SSM_HEADDIM).astype(BF16)

    outs = {k: [] for k in ("gdn_p", "gconv_p", "gconv_s", "ssm_p", "sconv_p", "sconv_s")}
    ssm_all = state_ssm.reshape(depth, dec, SSM_DINNER, SSM_DSTATE)
    gdn_s = ssm_s = None
    for l in range(depth):
        w = w_in[l]
        w_big = _bf(jnp.concatenate([w[:, :o_b], w[:, o_zs:o_dt], w[:, o_ga:]], axis=1))
        w_small = _bf(jnp.concatenate(
            [w[:, o_b:o_zs], w[:, o_dt:o_ga],
             jnp.zeros((D_MODEL, SMALL_W - 2 * GDN_HEADS - SSM_HEADS), F32)], axis=1))
        zpad = jnp.zeros((SMALL_W - COL_DT - SSM_HEADS,), F32)
        pv = jnp.zeros((SUBLANE, SMALL_W), F32)
        pv = pv.at[0].set(jnp.concatenate([jnp.zeros((COL_A,), F32), gdn_dt_bias[l], ssm_dt_bias[l], zpad]))
        pv = pv.at[1].set(jnp.concatenate([jnp.zeros((COL_A,), F32), gdn_a_log[l], ssm_a_log[l], zpad]))
        d_exp = jnp.repeat(ssm_d[l], SSM_HEADDIM).reshape(1, SSM_DINNER)
        gnw = gdn_norm_w[l].reshape(1, LANE)
        snw = ssm_norm_w[l].reshape(1, SSM_DINNER)
        scb = ssm_conv_b[l].reshape(1, SSM_CONV_CH)

        proj = _matmul(x_bf, w_big, tm, W_BIG // 4)
        small = _matmul(x_bf, w_small, tm, SMALL_W)

        og_p, s_p = _gdn_prompt(proj, small, gdn_conv_w[l], pv, gnw, batch, seq)
        yz_p, h_p = _ssd_prompt(proj, small, ssm_conv_w[l], scb, pv, expand_all, d_exp, snw, batch, seq)
        proj_s = lax.slice(proj, (n_p, 0), (nt, W_BIG))
        small_s = lax.slice(small, (n_p, 0), (nt, SMALL_W))
        og_s, yz_s, gdn_s, ssm_s = _sample_mix(l, proj_s, small_s, state_gdn_conv[l], state_ssm_conv[l],
                                               state_gdn, ssm_all, gdn_s, ssm_s, gdn_conv_w[l],
                                               ssm_conv_w[l], scb, pv, gnw, expand_all, d_exp, snw)
        og = jnp.concatenate([og_p, og_s], axis=0)
        yz = jnp.concatenate([yz_p, yz_s], axis=0)

        rw = jnp.concatenate([router_w[l], jnp.zeros((D_MODEL, LANE - N_EXPERTS), F32)], axis=1)
        rb = jnp.concatenate([router_b[l], jnp.full((LANE - N_EXPERTS,), NEG_BIG, F32)]).reshape(1, LANE)
        x1, x1t, route = _merge(og, yz, proj, x, _bf(w_br_gdn[l]), _bf(w_br_ssm[l]), _bf(w_out[l]),
                                ln1_g[l].reshape(1, D_MODEL), ln1_b[l].reshape(1, D_MODEL), rw, rb, alpha, tm)

        top_i = route[:, TOP_K:2 * TOP_K].astype(jnp.int32)
        dest, block_e, n_used, row_tok = _routing_tables(top_i)
        dest2 = dest.reshape(nt // TOK_TILE, TOK_TILE * TOP_K)
        yb = _experts(l, block_e, n_used, row_tok, x1t, exp_w_gate, exp_w_up, exp_w_down,
                      exp_b_gate, exp_b_up, exp_b_down)
        x, x_bf = _combine(dest2, route, x1, ln2_g[l].reshape(1, D_MODEL), ln2_b[l].reshape(1, D_MODEL),
                           yb, alpha)

        outs["gdn_p"].append(s_p)
        outs["ssm_p"].append(h_p.reshape(batch, SSM_HEADS, SSM_HEADDIM, SSM_DSTATE))
        tails_g = [lax.slice(proj, (b * seq + seq - (CONV_W - 1), 0), (b * seq + seq, GDN_CONV_CH))
                   for b in range(batch)]
        tails_s = [lax.slice(proj, (b * seq + seq - (CONV_W - 1), o_x - 2 * GDN_HEADS),
                             (b * seq + seq, o_x - 2 * GDN_HEADS + SSM_CONV_CH)) for b in range(batch)]
        outs["gconv_p"].append(jnp.stack(tails_g))
        outs["sconv_p"].append(jnp.stack(tails_s))
        outs["gconv_s"].append(jnp.concatenate(
            [state_gdn_conv[l][:, 1:], proj_s[:, None, :GDN_CONV_CH]], axis=1))
        xbc_off = o_x - 2 * GDN_HEADS
        outs["sconv_s"].append(jnp.concatenate(
            [state_ssm_conv[l][:, 1:], proj_s[:, None, xbc_off:xbc_off + SSM_CONV_CH]], axis=1))

    yp = x[:n_p].reshape(batch, seq, D_MODEL)
    ys = x[n_p:].reshape(dec, 1, D_MODEL)
    return (yp, ys, jnp.stack(outs["gdn_p"]), gdn_s, jnp.stack(outs["gconv_p"]),
            jnp.stack(outs["gconv_s"]), jnp.stack(outs["ssm_p"]),
            ssm_s.reshape(depth, dec, SSM_HEADS, SSM_HEADDIM, SSM_DSTATE),
            jnp.stack(outs["sconv_p"]), jnp.stack(outs["sconv_s"]))
```

```python
import functools

import jax
import jax.numpy as jnp
from jax import lax
from jax.experimental import pallas as pl
from jax.experimental.pallas import tpu as pltpu

F32 = jnp.float32
BF16 = jnp.bfloat16

D_MODEL = 1024
GDN_HEADS = 8
GDN_DK = 128
GDN_VDIM = 1024
GDN_CONV_CH = 3072
SSM_HEADS = 32
SSM_HEADDIM = 64
SSM_GROUPS = 4
SSM_DINNER = 2048
SSM_DSTATE = 128
SSM_CONV_CH = 3072
CONV_W = 4
N_EXPERTS = 32
TOP_K = 4
SWIGLU_ALPHA = 1.702
SWIGLU_LIMIT = 7.0
LN_EPS = 1e-5
RMS_EPS = 1e-6
L2_EPS = 1e-6
NEG_BIG = -1e30

W_BIG = 11264
SMALL_W = 128
COL_B, COL_A, COL_DT = 0, 8, 16

LANE = 128
SUBLANE = 8
CHUNK = 128
GDN_HB = 8
MOE_BLK = 512
TOK_TILE = 128
TOK_ROWS = D_MODEL // LANE
VMEM_LIMIT = 56 * 1024 * 1024


def _pick(n, cands):
    for c in cands:
        if n % c == 0:
            return c
    raise ValueError(f"no tile for {n}")


def _bf(x):
    return x.astype(BF16)


def _dot(a, b, prec=None):
    return jnp.dot(a, b, preferred_element_type=F32, precision=prec)


def _dot_nt(a, b):
    return lax.dot_general(a, b, (((1,), (1,)), ((), ())), preferred_element_type=F32)


def _dot_tn(a, b):
    return lax.dot_general(a, b, (((0,), (0,)), ((), ())), preferred_element_type=F32)


def _split3(x):
    hi = _bf(x)
    r = x - hi.astype(F32)
    mid = _bf(r)
    return hi, mid, _bf(r - mid.astype(F32))


def _dot_sel_rhs(x, sel):
    hi, mid, lo = _split3(x)
    return (_dot(lo, sel) + _dot(mid, sel)) + _dot(hi, sel)


def _dot_sel_lhs(sel, x):
    hi, mid, lo = _split3(x)
    return (_dot(sel, lo) + _dot(sel, mid)) + _dot(sel, hi)


def _sigmoid(x):
    return jax.nn.sigmoid(x)


def _silu(x):
    return x * jax.nn.sigmoid(x)


def _softplus(x):
    return jnp.maximum(x, 0.0) + jnp.log(1.0 + jnp.exp(-jnp.abs(x)))


def _layer_norm(x, g, b):
    mu = jnp.mean(x, axis=-1, keepdims=True)
    xc = x - mu
    var = jnp.mean(xc * xc, axis=-1, keepdims=True)
    return xc * lax.rsqrt(var + LN_EPS) * g + b


def _small_act(raw, pv):
    col = lax.broadcasted_iota(jnp.int32, raw.shape, 1)
    sp = _softplus(raw + pv[0:1, :])
    act = jnp.where(col < COL_A, _sigmoid(raw), sp)
    gda = sp * (-jnp.exp(pv[1:2, :]))
    return act, gda


def _mm_kernel(x_ref, w_ref, o_ref):
    o_ref[...] = _dot(x_ref[...], w_ref[...])


def _matmul(x, w, tm, tn):
    m, k = x.shape
    n = w.shape[1]
    return pl.pallas_call(
        _mm_kernel,
        out_shape=jax.ShapeDtypeStruct((m, n), F32),
        grid=(n // tn, m // tm),
        in_specs=[pl.BlockSpec((tm, k), lambda j, i: (i, 0)),
                  pl.BlockSpec((k, tn), lambda j, i: (0, j))],
        out_specs=pl.BlockSpec((tm, tn), lambda j, i: (i, j)),
        compiler_params=pltpu.CompilerParams(
            dimension_semantics=("parallel", "parallel"), vmem_limit_bytes=VMEM_LIMIT),
        name="in_proj",
    )(x, w)


def _tri_inv_all(mats, ii, jj, c):
    eye = (ii == jj).astype(F32)
    pair = (ii >> 1) == (jj >> 1)
    ts = [eye - jnp.where(pair, a, 0.0) for a in mats]
    abs_ = [_bf(a) for a in mats]
    s = 1
    while (2 << s) <= c:
        same_outer = (ii >> (s + 1)) == (jj >> (s + 1))
        same_inner = (ii >> s) == (jj >> s)
        off = _bf((same_outer & jnp.logical_not(same_inner)).astype(F32))
        tbs = [_bf(t) for t in ts]
        tes = [_dot(tb, ab * off) for tb, ab in zip(tbs, abs_)]
        ts = [t - _dot(_bf(te), tb) for t, te, tb in zip(ts, tes, tbs)]
        s += 1
    return ts


def _with_sample_rows(chunk_fn, n_in):
    def kern(*refs):
        b = pl.program_id(0)
        nb = pl.num_programs(0) - 1
        tail_ref = refs[n_in]
        out_ref = refs[n_in + 1]

        @pl.when(b < nb)
        def _():
            chunk_fn(*refs[:n_in], *refs[n_in + 1:])

        @pl.when(b == nb)
        def _():
            out_ref[...] = tail_ref[...]

    return kern


def _gdn_chunk(q_ref, k_ref, v_ref, zg_ref, small_ref, cwq_ref, cwk_ref, cwv_ref, pv_ref, nw_ref,
               o_ref, sfin_ref, xf, tail, s_scr):
    assert GDN_HB == GDN_HEADS
    c_len = q_ref.shape[0]
    c = pl.program_id(2)
    nc = pl.num_programs(2)

    @pl.when(c == 0)
    def _():
        tail[...] = jnp.zeros_like(tail)
        s_scr[...] = jnp.zeros_like(s_scr)

    for p, r in enumerate((q_ref, k_ref, v_ref)):
        xf[p, 0:SUBLANE, :] = tail[p]
        xf[p, SUBLANE:SUBLANE + c_len, :] = r[...]
        tail[p] = r[c_len - SUBLANE:c_len, :]
    cws = (cwq_ref, cwk_ref, cwv_ref)

    def conv(p, hs):
        acc = None
        for i in range(CONV_W):
            term = cws[p][i:i + 1, hs] * xf[p, pl.ds(SUBLANE - (CONV_W - 1) + i, c_len), hs]
            acc = term if acc is None else acc + term
        return _silu(acc)

    act, gda = _small_act(small_ref[...], pv_ref[...])
    ii = lax.broadcasted_iota(jnp.int32, (c_len, c_len), 0)
    jj = lax.broadcasted_iota(jnp.int32, (c_len, c_len), 1)
    incl = ii >= jj
    strict = ii > jj
    gcum = _dot_sel_lhs(_bf(incl.astype(F32)), gda)
    gcum_t = gcum.T
    nw = nw_ref[...]

    heads = range(GDN_HB)
    hsl = [slice(hh * LANE, (hh + 1) * LANE) for hh in heads]
    betas = [act[:, COL_B + hh:COL_B + hh + 1] for hh in heads]
    gcs = [gcum[:, COL_A + hh:COL_A + hh + 1] for hh in heads]
    grs = [gcum_t[COL_A + hh:COL_A + hh + 1, :] for hh in heads]
    ks = []
    for hh in heads:
        k = conv(1, hsl[hh])
        ks.append(k * lax.rsqrt(jnp.sum(k * k, axis=-1, keepdims=True) + L2_EPS))
    kbs = [ks[hh] * betas[hh] for hh in heads]
    kbfs = [_bf(k) for k in ks]
    kks = [_dot_nt(_bf(kbs[hh]), kbfs[hh]) for hh in heads]
    gams = [jnp.exp(jnp.where(incl, gcs[hh] - grs[hh], NEG_BIG)) for hh in heads]
    amats = [jnp.where(strict, kks[hh] * gams[hh], 0.0) for hh in heads]
    qs = []
    for hh in heads:
        q = conv(0, hsl[hh])
        qs.append(q * lax.rsqrt(jnp.sum(q * q, axis=-1, keepdims=True) + L2_EPS) * (GDN_DK ** -0.5))
    qks = [_dot_nt(_bf(qs[hh]), kbfs[hh]) * gams[hh] for hh in heads]
    egs = [jnp.exp(gcs[hh]) for hh in heads]
    rhs = [_bf(jnp.concatenate([conv(2, hsl[hh]) * betas[hh], kbs[hh] * egs[hh]], axis=1)) for hh in heads]
    ts = _tri_inv_all(amats, ii, jj, c_len)
    uws = [_dot(_bf(ts[hh]), rhs[hh]) for hh in heads]
    glasts = [gcs[hh][c_len - 1:c_len, :] for hh in heads]
    s_olds = [s_scr[hh] for hh in heads]
    sbs = [_bf(s) for s in s_olds]
    v_news = [uws[hh][:, :LANE] - _dot(_bf(uws[hh][:, LANE:]), sbs[hh]) for hh in heads]
    vnbs = [_bf(v) for v in v_news]
    os_ = [_dot(_bf(qs[hh] * egs[hh]), sbs[hh]) + _dot(_bf(qks[hh]), vnbs[hh]) for hh in heads]
    for hh in heads:
        kdec = ks[hh] * jnp.exp(glasts[hh] - gcs[hh])
        s_scr[hh] = s_olds[hh] * jnp.exp(glasts[hh]) + _dot_tn(_bf(kdec), vnbs[hh])
    for hh in heads:
        o = os_[hh]
        o = (o * lax.rsqrt(jnp.mean(o * o, axis=-1, keepdims=True) + RMS_EPS) * nw
             * _silu(zg_ref[:, hsl[hh]]))
        o_ref[:, hsl[hh]] = _bf(o)

    @pl.when(c == nc - 1)
    def _():
        sfin_ref[0] = s_scr[...]


def _sample_tail_maps(batch, nc, n_tail):
    def seq(b):
        return jnp.minimum(b, batch - 1)

    def tail(c):
        return jnp.minimum(c, n_tail - 1)

    def in_row(b, c):
        return seq(b) * nc + c

    def out_row(b, c):
        return jnp.where(b < batch, b * nc + c, batch * nc + tail(c))

    return seq, tail, in_row, out_row


def _gdn_prompt(proj, small, conv_w, pv, norm_w, og_s, batch, seq):
    nc = seq // CHUNK
    n_tail = og_s.shape[0] // CHUNK
    hbw = GDN_HB * LANE
    ngrp = GDN_HEADS // GDN_HB
    kq, kk, kv, kz = 0, GDN_VDIM // hbw, 2 * GDN_VDIM // hbw, 3 * GDN_VDIM // hbw
    seq_of, tail_of, in_row, out_row = _sample_tail_maps(batch, nc, n_tail)

    def pspec(off):
        return pl.BlockSpec((CHUNK, hbw), lambda b, hg, c: (in_row(b, c), off + hg))

    def wspec(off):
        return pl.BlockSpec((CONV_W, hbw), lambda b, hg, c: (0, off + hg))

    return pl.pallas_call(
        _with_sample_rows(_gdn_chunk, 10),
        out_shape=(jax.ShapeDtypeStruct((batch * seq + og_s.shape[0], GDN_VDIM), BF16),
                   jax.ShapeDtypeStruct((batch, GDN_HEADS, GDN_DK, LANE), F32)),
        grid=(batch + 1, ngrp, nc),
        in_specs=[pspec(kq), pspec(kk), pspec(kv), pspec(kz),
                  pl.BlockSpec((CHUNK, SMALL_W), lambda b, hg, c: (in_row(b, c), 0)),
                  wspec(kq), wspec(kk), wspec(kv),
                  pl.BlockSpec((SUBLANE, SMALL_W), lambda b, hg, c: (0, 0)),
                  pl.BlockSpec((1, LANE), lambda b, hg, c: (0, 0)),
                  pl.BlockSpec((CHUNK, hbw), lambda b, hg, c: (tail_of(c), hg))],
        out_specs=(pl.BlockSpec((CHUNK, hbw), lambda b, hg, c: (out_row(b, c), hg)),
                   pl.BlockSpec((1, GDN_HB, GDN_DK, LANE), lambda b, hg, c: (seq_of(b), hg, 0, 0))),
        scratch_shapes=[pltpu.VMEM((3, CHUNK + SUBLANE, hbw), F32),
                        pltpu.VMEM((3, SUBLANE, hbw), F32),
                        pltpu.VMEM((GDN_HB, GDN_DK, LANE), F32)],
        compiler_params=pltpu.CompilerParams(
            dimension_semantics=("arbitrary", "arbitrary", "arbitrary"), vmem_limit_bytes=VMEM_LIMIT),
        name="gdn_prompt",
    )(proj, proj, proj, proj, small, conv_w, conv_w, conv_w, pv, norm_w, og_s)


def _ssd_chunk(xs_ref, b_ref, c_ref, zs_ref, small_ref, cwx_ref, cwb_ref, cwc_ref, cbx_ref, cbb_ref,
               cbc_ref, pv_ref, ex_ref, dexp_ref, nw_ref, yz_ref, hfin_ref, xf, tail, ht, ydiag):
    c_len = xs_ref.shape[0]
    gw = SSM_DINNER // SSM_GROUPS
    hpg = gw // SSM_HEADDIM
    gn = SSM_GROUPS * SSM_DSTATE
    c = pl.program_id(1)
    nc = pl.num_programs(1)

    @pl.when(c == 0)
    def _():
        tail[...] = jnp.zeros_like(tail)
        ht[...] = jnp.zeros_like(ht)

    parts = ((xs_ref, cwx_ref, cbx_ref, 0, SSM_DINNER), (b_ref, cwb_ref, cbb_ref, SSM_DINNER, gn),
             (c_ref, cwc_ref, cbc_ref, SSM_DINNER + gn, gn))
    convs = []
    for r, cw, cb, off, wd in parts:
        sl = slice(off, off + wd)
        xf[0:SUBLANE, sl] = tail[:, sl]
        xf[SUBLANE:SUBLANE + c_len, sl] = r[...]
        tail[:, sl] = r[c_len - SUBLANE:c_len, :]
        acc = cb[...]
        for i in range(CONV_W):
            acc = acc + cw[i:i + 1, :] * xf[pl.ds(SUBLANE - (CONV_W - 1) + i, c_len), sl]
        convs.append(_silu(acc))
    xs, bm_all, cm_all = convs

    act, gda = _small_act(small_ref[...], pv_ref[...])
    ii = lax.broadcasted_iota(jnp.int32, (c_len, c_len), 0)
    jj = lax.broadcasted_iota(jnp.int32, (c_len, c_len), 1)
    incl = ii >= jj
    acs = _dot_sel_lhs(_bf(incl.astype(F32)), gda)
    acs_t = acs.T
    ex = ex_ref[...]
    dt_x = _dot_sel_rhs(act, ex)
    acs_x = _dot_sel_rhs(acs, ex)
    last = acs_x[c_len - 1:c_len, :]
    xdt = xs * dt_x
    xdec = _bf(xdt * jnp.exp(last - acs_x))
    lane = lax.broadcasted_iota(jnp.int32, (c_len, LANE), 1)
    lo_half = lane < SSM_HEADDIM

    groups = range(SSM_GROUPS)
    bms = [_bf(bm_all[:, g * SSM_DSTATE:(g + 1) * SSM_DSTATE]) for g in groups]
    cms = [_bf(cm_all[:, g * SSM_DSTATE:(g + 1) * SSM_DSTATE]) for g in groups]
    cbs = [_dot_nt(cms[g], bms[g]) for g in groups]
    h_olds = [ht[:, g * gw:(g + 1) * gw] for g in groups]
    y_offs = [_dot(cms[g], _bf(h_olds[g])) for g in groups]
    for g in groups:
        for pr in range(hpg // 2):
            ps = slice(g * gw + pr * LANE, g * gw + (pr + 1) * LANE)
            xpair = xdt[:, ps]
            acc = None
            for half in range(2):
                head = g * hpg + pr * 2 + half
                ac = acs[:, COL_DT + head:COL_DT + head + 1]
                ar = acs_t[COL_DT + head:COL_DT + head + 1, :]
                sc = cbs[g] * jnp.exp(jnp.where(incl, ac - ar, NEG_BIG))
                keep = lo_half if half == 0 else jnp.logical_not(lo_half)
                term = _dot(_bf(sc), _bf(jnp.where(keep, xpair, 0.0)))
                acc = term if acc is None else acc + term
            ydiag[:, ps] = acc
    for g in groups:
        gs = slice(g * gw, (g + 1) * gw)
        ht[:, gs] = h_olds[g] * jnp.exp(last[:, gs]) + _dot_tn(bms[g], xdec[:, gs])
    y = ydiag[...] + jnp.concatenate(y_offs, axis=1) * jnp.exp(acs_x) + dexp_ref[...] * xs
    yz = y * _silu(zs_ref[...])
    nw = nw_ref[...]
    for g in groups:
        gs = slice(g * gw, (g + 1) * gw)
        yg = yz[:, gs]
        yz_ref[:, gs] = _bf(yg * lax.rsqrt(jnp.mean(yg * yg, axis=-1, keepdims=True) + RMS_EPS) * nw[:, gs])

    @pl.when(c == nc - 1)
    def _():
        hfin_ref[0] = ht[...].T


def _ssd_prompt(proj, small, conv_w, conv_b, pv, expand, d_exp, norm_w, yz_s, batch, seq):
    nc = seq // CHUNK
    gn = SSM_GROUPS * SSM_DSTATE
    x_off = (GDN_CONV_CH + GDN_VDIM + SSM_DINNER)
    z_off = GDN_CONV_CH + GDN_VDIM
    kx = x_off // SSM_DINNER
    kb = (x_off + SSM_DINNER) // gn
    kz = z_off // SSM_DINNER
    wb = SSM_DINNER // gn

    n_tail = yz_s.shape[0] // CHUNK
    seq_of, tail_of, row, out_row = _sample_tail_maps(batch, nc, n_tail)

    in_specs = [
        pl.BlockSpec((CHUNK, SSM_DINNER), lambda b, c: (row(b, c), kx)),
        pl.BlockSpec((CHUNK, gn), lambda b, c: (row(b, c), kb)),
        pl.BlockSpec((CHUNK, gn), lambda b, c: (row(b, c), kb + 1)),
        pl.BlockSpec((CHUNK, SSM_DINNER), lambda b, c: (row(b, c), kz)),
        pl.BlockSpec((CHUNK, SMALL_W), lambda b, c: (row(b, c), 0)),
        pl.BlockSpec((CONV_W, SSM_DINNER), lambda b, c: (0, 0)),
        pl.BlockSpec((CONV_W, gn), lambda b, c: (0, wb)),
        pl.BlockSpec((CONV_W, gn), lambda b, c: (0, wb + 1)),
        pl.BlockSpec((1, SSM_DINNER), lambda b, c: (0, 0)),
        pl.BlockSpec((1, gn), lambda b, c: (0, wb)),
        pl.BlockSpec((1, gn), lambda b, c: (0, wb + 1)),
        pl.BlockSpec((SUBLANE, SMALL_W), lambda b, c: (0, 0)),
        pl.BlockSpec((SMALL_W, SSM_DINNER), lambda b, c: (0, 0)),
        pl.BlockSpec((1, SSM_DINNER), lambda b, c: (0, 0)),
        pl.BlockSpec((1, SSM_DINNER), lambda b, c: (0, 0)),
        pl.BlockSpec((CHUNK, SSM_DINNER), lambda b, c: (tail_of(c), 0)),
    ]
    return pl.pallas_call(
        _with_sample_rows(_ssd_chunk, 15),
        out_shape=(jax.ShapeDtypeStruct((batch * seq + yz_s.shape[0], SSM_DINNER), BF16),
                   jax.ShapeDtypeStruct((batch, SSM_DINNER, SSM_DSTATE), F32)),
        grid=(batch + 1, nc),
        in_specs=in_specs,
        out_specs=(pl.BlockSpec((CHUNK, SSM_DINNER), lambda b, c: (out_row(b, c), 0)),
                   pl.BlockSpec((1, SSM_DINNER, SSM_DSTATE), lambda b, c: (seq_of(b), 0, 0))),
        scratch_shapes=[pltpu.VMEM((CHUNK + SUBLANE, SSM_CONV_CH), F32),
                        pltpu.VMEM((SUBLANE, SSM_CONV_CH), F32),
                        pltpu.VMEM((SSM_DSTATE, SSM_DINNER), F32),
                        pltpu.VMEM((CHUNK, SSM_DINNER), F32)],
        compiler_params=pltpu.CompilerParams(
            dimension_semantics=("arbitrary", "arbitrary"), vmem_limit_bytes=VMEM_LIMIT),
        name="ssd_prompt",
    )(proj, proj, proj, proj, small, conv_w, conv_w, conv_w, conv_b, conv_b, conv_b, pv, expand,
      d_exp, norm_w, yz_s)


def _sample_prep_kernel(small_ref, pv_ref, ex_ref, act_ref, gda_ref, dtx_ref, dax_ref):
    act, gda = _small_act(small_ref[...], pv_ref[...])
    act_ref[...] = act
    gda_ref[...] = gda
    ex = ex_ref[...]
    dtx_ref[...] = _dot_sel_rhs(act, ex)
    dax_ref[...] = _dot_sel_rhs(gda, ex)


def _sample_prep(small_s, pv, expand_all):
    n = small_s.shape[0]
    return pl.pallas_call(
        _sample_prep_kernel,
        out_shape=(jax.ShapeDtypeStruct((n, SMALL_W), F32), jax.ShapeDtypeStruct((n, SMALL_W), F32),
                   jax.ShapeDtypeStruct((n, SSM_DINNER), F32), jax.ShapeDtypeStruct((n, SSM_DINNER), F32)),
        name="sample_prep",
    )(small_s, pv, expand_all)


def _sample_kernel(proj_ref, act_ref, gda_ref, dtx_ref, dax_ref, gcs_ref, scs_ref, s0_ref, h0_ref, gcw_ref,
                   scw_ref, scb_ref, gnw_ref, dexp_ref, snw_ref, *rest, n_alias):
    og_ref, yz_ref, s_ref, h_ref, stk, stk2 = rest[n_alias:]
    if n_alias == 0:
        if s_ref.shape[0] > 1:
            s_ref[1:] = jnp.zeros((s_ref.shape[0] - 1,) + s_ref.shape[1:], F32)
            h_ref[1:] = jnp.zeros((h_ref.shape[0] - 1,) + h_ref.shape[1:], F32)
        s_ref = s_ref.at[0]
        h_ref = h_ref.at[0]
    gw = SSM_DINNER // SSM_GROUPS
    z_off = GDN_CONV_CH + GDN_VDIM
    x_off = z_off + SSM_DINNER

    def conv1(state_ref, w_ref, new_row):
        acc = w_ref[CONV_W - 1:CONV_W, :] * new_row
        for i in range(CONV_W - 1):
            acc = acc + w_ref[i:i + 1, :] * state_ref[i:i + 1, :]
        return acc

    act = act_ref[...]
    gda = gda_ref[...]

    qkv = _silu(conv1(gcs_ref, gcw_ref, proj_ref[:, 0:GDN_CONV_CH]))
    stk[...] = jnp.zeros_like(stk)
    qs, vs = [], []
    for h in range(GDN_HEADS):
        q = qkv[:, h * LANE:(h + 1) * LANE]
        k = qkv[:, GDN_VDIM + h * LANE:GDN_VDIM + (h + 1) * LANE]
        q = q * lax.rsqrt(jnp.sum(q * q, axis=-1, keepdims=True) + L2_EPS) * (GDN_DK ** -0.5)
        k = k * lax.rsqrt(jnp.sum(k * k, axis=-1, keepdims=True) + L2_EPS)
        stk[h:h + 1, :] = k
        stk[GDN_HEADS + h:GDN_HEADS + h + 1, :] = q
        vs.append(qkv[:, 2 * GDN_VDIM + h * LANE:2 * GDN_VDIM + (h + 1) * LANE])
    cols = stk[...].T
    gnw = gnw_ref[...]
    for h in range(GDN_HEADS):
        kc = cols[:, h:h + 1]
        qc = cols[:, GDN_HEADS + h:GDN_HEADS + h + 1]
        beta = act[:, COL_B + h:COL_B + h + 1]
        gh = gda[:, COL_A + h:COL_A + h + 1]
        sd = s0_ref[h] * jnp.exp(gh)
        v_old = jnp.sum(sd * kc, axis=0, keepdims=True)
        delta = (vs[h] - v_old) * beta
        s_new = sd + kc * delta
        s_ref[h] = s_new
        o = jnp.sum(s_new * qc, axis=0, keepdims=True)
        zg = proj_ref[:, GDN_CONV_CH + h * LANE:GDN_CONV_CH + (h + 1) * LANE]
        o = o * lax.rsqrt(jnp.mean(o * o, axis=-1, keepdims=True) + RMS_EPS) * gnw * _silu(zg)
        og_ref[:, h * LANE:(h + 1) * LANE] = _bf(o)

    xbc = _silu(conv1(scs_ref, scw_ref, proj_ref[:, x_off:x_off + SSM_CONV_CH]) + scb_ref[...])
    dexp = dexp_ref[...]
    stk2[...] = jnp.zeros_like(stk2)
    for g in range(SSM_GROUPS):
        gs = slice(g * gw, (g + 1) * gw)
        stk2[2 * g:2 * g + 1, :] = jnp.exp(dax_ref[:, gs])
        stk2[2 * g + 1:2 * g + 2, :] = xbc[:, gs] * dtx_ref[:, gs]
    cols2 = stk2[...].T
    for g in range(SSM_GROUPS):
        gs = slice(g * gw, (g + 1) * gw)
        xs = xbc[:, gs]
        bm = xbc[:, SSM_DINNER + g * SSM_DSTATE:SSM_DINNER + (g + 1) * SSM_DSTATE]
        cm = xbc[:, SSM_DINNER + SSM_GROUPS * SSM_DSTATE + g * SSM_DSTATE:
                 SSM_DINNER + SSM_GROUPS * SSM_DSTATE + (g + 1) * SSM_DSTATE]
        h_new = h0_ref[gs, :] * cols2[:, 2 * g:2 * g + 1] + cols2[:, 2 * g + 1:2 * g + 2] * bm
        h_ref[gs, :] = h_new
        cm16 = jnp.broadcast_to(cm, (2 * SUBLANE, SSM_DSTATE))
        y = _dot_nt(_bf(cm16), _bf(h_new))[0:1, :] + dexp[:, gs] * xs
        yz = y * _silu(proj_ref[:, z_off + g * gw:z_off + (g + 1) * gw])
        yz = yz * lax.rsqrt(jnp.mean(yz * yz, axis=-1, keepdims=True) + RMS_EPS) * snw_ref[:, gs]
        yz_ref[:, gs] = _bf(yz)


def _sample_mix(layer, proj_s, small_s, gconv_state, sconv_state, s_all, h_all, s_prev, h_prev, gconv_w,
                sconv_w, sconv_b, pv, gnorm_w, expand_all, d_exp, snorm_w):
    n = proj_s.shape[0]
    gw = SSM_DINNER // SSM_GROUPS
    proj3 = proj_s.reshape(n, 1, W_BIG)
    act, gda, dtx, dax = _sample_prep(small_s, pv, expand_all)

    def full(shape):
        nd = len(shape)
        return pl.BlockSpec(shape, lambda i: (0,) * nd)

    def per_seq(shape):
        nd = len(shape)
        return pl.BlockSpec((None,) + shape, lambda i: (i,) + (0,) * nd)

    def per_layer_seq(shape):
        nd = len(shape)
        return pl.BlockSpec((None, None) + shape, lambda i: (layer, i) + (0,) * nd)

    in_specs = [per_seq((1, W_BIG)), per_seq((1, SMALL_W)), per_seq((1, SMALL_W)),
                per_seq((1, SSM_DINNER)), per_seq((1, SSM_DINNER)),
                per_seq((CONV_W - 1, GDN_CONV_CH)), per_seq((CONV_W - 1, SSM_CONV_CH)),
                per_layer_seq((GDN_HEADS, GDN_DK, LANE)), per_layer_seq((SSM_DINNER, SSM_DSTATE)),
                full((CONV_W, GDN_CONV_CH)), full((CONV_W, SSM_CONV_CH)), full((1, SSM_CONV_CH)),
                full((1, LANE)), full((1, SSM_DINNER)), full((1, SSM_DINNER))]
    args = [proj3, act.reshape(n, 1, SMALL_W), gda.reshape(n, 1, SMALL_W), dtx.reshape(n, 1, SSM_DINNER),
            dax.reshape(n, 1, SSM_DINNER), gconv_state, sconv_state, s_all, h_all, gconv_w, sconv_w, sconv_b,
            gnorm_w, d_exp, snorm_w]
    aliases = {}
    if s_prev is not None:
        aliases = {len(args): 2, len(args) + 1: 3}
        in_specs += [pl.BlockSpec(memory_space=pl.ANY), pl.BlockSpec(memory_space=pl.ANY)]
        args += [s_prev, h_prev]
    if s_prev is None:
        depth = s_all.shape[0]
        state_specs = [pl.BlockSpec((depth, None, GDN_HEADS, GDN_DK, LANE), lambda i: (0, i, 0, 0, 0)),
                       pl.BlockSpec((depth, None, SSM_DINNER, SSM_DSTATE), lambda i: (0, i, 0, 0))]
    else:
        state_specs = [per_layer_seq((GDN_HEADS, GDN_DK, LANE)), per_layer_seq((SSM_DINNER, SSM_DSTATE))]
    kern = functools.partial(_sample_kernel, n_alias=len(aliases))
    og, yz, s_new, h_new = pl.pallas_call(
        kern,
        out_shape=(jax.ShapeDtypeStruct((n, 1, GDN_VDIM), BF16),
                   jax.ShapeDtypeStruct((n, 1, SSM_DINNER), BF16),
                   jax.ShapeDtypeStruct(s_all.shape, F32),
                   jax.ShapeDtypeStruct(h_all.shape, F32)),
        grid=(n,),
        in_specs=in_specs,
        out_specs=(per_seq((1, GDN_VDIM)), per_seq((1, SSM_DINNER)), state_specs[0], state_specs[1]),
        scratch_shapes=[pltpu.VMEM((LANE, LANE), F32), pltpu.VMEM((LANE, gw), F32)],
        input_output_aliases=aliases,
        compiler_params=pltpu.CompilerParams(
            dimension_semantics=("parallel",), vmem_limit_bytes=VMEM_LIMIT),
        name="sample_mix",
    )(*args)
    return og.reshape(n, GDN_VDIM), yz.reshape(n, SSM_DINNER), s_new, h_new


def _merge_kernel(og_ref, yz_ref, ga_ref, gb_ref, x_ref, wbg_ref, wbs_ref, wout_ref, lng_ref, lnb_ref,
                  rwh_ref, rwl_ref, rb_ref, x1_ref, x1t_ref, route_ref, *, alpha):
    a = _dot(og_ref[...], wbg_ref[...])
    b = _dot(yz_ref[...], wbs_ref[...])
    merged = _sigmoid(ga_ref[...]) * a + _sigmoid(gb_ref[...]) * b
    mix = _dot(_bf(merged), wout_ref[...])
    x1 = _layer_norm(alpha * x_ref[...] + mix, lng_ref[...], lnb_ref[...])
    x1_ref[...] = x1
    _store_token_tiles(x1t_ref, x1)

    xh = _bf(x1)
    xl = _bf(x1 - xh.astype(F32))
    rwh = rwh_ref[...]
    lg = (_dot(xh, rwl_ref[...]) + _dot(xl, rwh)) + _dot(xh, rwh) + rb_ref[...]
    colf = lax.broadcasted_iota(jnp.int32, lg.shape, 1).astype(F32)
    vals, idxs = [], []
    for _ in range(TOP_K):
        m = jnp.max(lg, axis=-1, keepdims=True)
        idx = jnp.min(jnp.where(lg == m, colf, float(LANE)), axis=-1, keepdims=True)
        vals.append(m)
        idxs.append(idx)
        lg = jnp.where(colf == idx, 2.0 * NEG_BIG, lg)
    es = [jnp.exp(v - vals[0]) for v in vals]
    den = es[0] + es[1] + es[2] + es[3]
    route = jnp.zeros_like(lg)
    for kk in range(TOP_K):
        route = jnp.where(colf == float(kk), es[kk] / den, route)
        route = jnp.where(colf == float(TOP_K + kk), idxs[kk], route)
    route_ref[...] = route


def _merge(og, yz, proj, x, wbg, wbs, wout, lng, lnb, rw, rb, alpha, tm):
    nt = x.shape[0]
    ka = (W_BIG - 2 * D_MODEL) // D_MODEL
    kern = functools.partial(_merge_kernel, alpha=alpha)
    rw_hi = _bf(rw)

    def full(shape):
        return pl.BlockSpec(shape, lambda i: (0, 0))

    return pl.pallas_call(
        kern,
        out_shape=(jax.ShapeDtypeStruct((nt, D_MODEL), F32),
                   jax.ShapeDtypeStruct((nt * TOK_ROWS, LANE), F32),
                   jax.ShapeDtypeStruct((nt, LANE), F32)),
        grid=(nt // tm,),
        in_specs=[pl.BlockSpec((tm, GDN_VDIM), lambda i: (i, 0)),
                  pl.BlockSpec((tm, SSM_DINNER), lambda i: (i, 0)),
                  pl.BlockSpec((tm, D_MODEL), lambda i: (i, ka)),
                  pl.BlockSpec((tm, D_MODEL), lambda i: (i, ka + 1)),
                  pl.BlockSpec((tm, D_MODEL), lambda i: (i, 0)),
                  full((GDN_VDIM, D_MODEL)), full((SSM_DINNER, D_MODEL)), full((D_MODEL, D_MODEL)),
                  full((1, D_MODEL)), full((1, D_MODEL)), full((D_MODEL, LANE)), full((D_MODEL, LANE)),
                  full((1, LANE))],
        out_specs=(pl.BlockSpec((tm, D_MODEL), lambda i: (i, 0)),
                   pl.BlockSpec((tm * TOK_ROWS, LANE), lambda i: (i, 0)),
                   pl.BlockSpec((tm, LANE), lambda i: (i, 0))),
        compiler_params=pltpu.CompilerParams(
            dimension_semantics=("parallel",), vmem_limit_bytes=VMEM_LIMIT),
        name="merge_ln_router",
    )(og, yz, proj, proj, x, wbg, wbs, wout, lng, lnb, rw_hi, _bf(rw - rw_hi.astype(F32)), rb)


def _store_token_tiles(ref, val):
    n = val.shape[0]
    for cc in range(TOK_ROWS):
        ref[pl.ds(cc, n, stride=TOK_ROWS), :] = val[:, cc * LANE:(cc + 1) * LANE]


def _load_token_chunk(ref, cc, n):
    return ref[pl.ds(cc, n, stride=TOK_ROWS), :]


def _token_rows(t):
    return pl.ds(pl.multiple_of(t * TOK_ROWS, TOK_ROWS), TOK_ROWS)


def _dispatch_kernel(dest_hbm, x_ref, xb_in, xb_out, idx, sem_idx, sem):
    del xb_in
    i = pl.program_id(0)
    cp = pltpu.make_async_copy(dest_hbm.at[i], idx, sem_idx)
    cp.start()
    cp.wait()

    def issue(r, carry):
        for kk in range(TOP_K):
            pltpu.make_async_copy(x_ref.at[_token_rows(r), :],
                                  xb_out.at[_token_rows(idx[r * TOP_K + kk]), :], sem).start(priority=kk % 2)
        return carry

    lax.fori_loop(0, TOK_TILE, issue, 0, unroll=8)
    for _ in range(TOP_K):
        pltpu.make_async_copy(x_ref, xb_out.at[pl.ds(0, TOK_TILE * TOK_ROWS), :], sem).wait()


def _dispatch(dest2, x1t, xb_zero):
    n_tiles = dest2.shape[0]
    return pl.pallas_call(
        _dispatch_kernel,
        out_shape=jax.ShapeDtypeStruct(xb_zero.shape, F32),
        grid=(n_tiles,),
        in_specs=[pl.BlockSpec(memory_space=pl.ANY),
                  pl.BlockSpec((TOK_TILE * TOK_ROWS, LANE), lambda i: (i, 0)),
                  pl.BlockSpec(memory_space=pl.ANY)],
        out_specs=pl.BlockSpec(memory_space=pl.ANY),
        scratch_shapes=[pltpu.SMEM((TOK_TILE * TOP_K,), jnp.int32),
                        pltpu.SemaphoreType.DMA, pltpu.SemaphoreType.DMA],
        input_output_aliases={2: 0},
        compiler_params=pltpu.CompilerParams(
            dimension_semantics=("arbitrary",), vmem_limit_bytes=VMEM_LIMIT),
        name="moe_dispatch",
    )(dest2, x1t, xb_zero)


def _expert_kernel(be_ref, nu_ref, x_ref, wg_ref, wu_ref, wd_ref, bg_ref, bu_ref, bd_ref, y_ref,
                   wgb, wub, wdb):
    j = pl.program_id(0)
    e = be_ref[j]
    prev = be_ref[jnp.maximum(j - 1, 0)]
    used = j < nu_ref[0]

    @pl.when(used & ((j == 0) | (e != prev)))
    def _():
        wgb[...] = _bf(wg_ref[...])
        wub[...] = _bf(wu_ref[...])
        wdb[...] = _bf(wd_ref[...])

    @pl.when(used)
    def _():
        x = jnp.concatenate([_bf(_load_token_chunk(x_ref, cc, MOE_BLK)) for cc in range(TOK_ROWS)], axis=1)
        gt = _dot(x, wgb[...]) + bg_ref[...]
        up = _dot(x, wub[...]) + bu_ref[...]
        gt = jnp.minimum(gt, SWIGLU_LIMIT)
        up = jnp.clip(up, -SWIGLU_LIMIT, SWIGLU_LIMIT)
        h = (up + 1.0) * (gt * _sigmoid(SWIGLU_ALPHA * gt))
        _store_token_tiles(y_ref, _dot(_bf(h), wdb[...]) + bd_ref[...])

    @pl.when(jnp.logical_not(used))
    def _():
        y_ref[...] = jnp.zeros_like(y_ref)


def _experts(layer, block_e, n_used, xb, wg, wu, wd, bg, bu, bd):
    rows = xb.shape[0]
    blk_rows = MOE_BLK * TOK_ROWS
    nblk = rows // blk_rows
    depth = wg.shape[0]
    d_e = wg.shape[-1]
    wspec_in = pl.BlockSpec((None, None, D_MODEL, d_e), lambda j, be, nu: (layer, be[j], 0, 0))
    wspec_out = pl.BlockSpec((None, None, d_e, D_MODEL), lambda j, be, nu: (layer, be[j], 0, 0))
    bspec_e = pl.BlockSpec((None, None, 1, d_e), lambda j, be, nu: (layer, be[j], 0, 0))
    bspec_d = pl.BlockSpec((None, None, 1, D_MODEL), lambda j, be, nu: (layer, be[j], 0, 0))
    grid_spec = pltpu.PrefetchScalarGridSpec(
        num_scalar_prefetch=2,
        grid=(nblk,),
        in_specs=[pl.BlockSpec((blk_rows, LANE), lambda j, be, nu: (j, 0)),
                  wspec_in, wspec_in, wspec_out, bspec_e, bspec_e, bspec_d],
        out_specs=pl.BlockSpec((blk_rows, LANE), lambda j, be, nu: (j, 0)),
        scratch_shapes=[pltpu.VMEM((D_MODEL, d_e), BF16), pltpu.VMEM((D_MODEL, d_e), BF16),
                        pltpu.VMEM((d_e, D_MODEL), BF16)],
    )
    return pl.pallas_call(
        _expert_kernel,
        out_shape=jax.ShapeDtypeStruct((rows, LANE), F32),
        grid_spec=grid_spec,
        compiler_params=pltpu.CompilerParams(
            dimension_semantics=("arbitrary",), vmem_limit_bytes=VMEM_LIMIT),
        name="moe_experts",
    )(block_e, n_used, xb, wg, wu, wd, bg.reshape(depth, N_EXPERTS, 1, d_e),
      bu.reshape(depth, N_EXPERTS, 1, d_e), bd.reshape(depth, N_EXPERTS, 1, D_MODEL))


def _combine_kernel(dest_hbm, gates_ref, x1_ref, lng_ref, lnb_ref, yb_hbm, y_ref, ybf_ref, idx, buf,
                    sem_idx, sem, *, alpha):
    i = pl.program_id(0)
    n = pl.num_programs(0)
    slot = i % 2
    nxt = 1 - slot

    def idx_copy(tile, s):
        return pltpu.make_async_copy(dest_hbm.at[tile], idx.at[s], sem_idx.at[s])

    def issue_gathers(s):
        def issue(r, carry):
            for kk in range(TOP_K):
                pltpu.make_async_copy(yb_hbm.at[_token_rows(idx[s, r * TOP_K + kk]), :],
                                      buf.at[s, kk, _token_rows(r), :], sem.at[s]).start(priority=kk % 2)
            return carry

        lax.fori_loop(0, TOK_TILE, issue, 0, unroll=8)

    @pl.when(i == 0)
    def _():
        first = idx_copy(0, 0)
        first.start()
        first.wait()
        issue_gathers(0)

        @pl.when(n > 1)
        def _():
            idx_copy(1, 1).start()

    @pl.when(i + 1 < n)
    def _():
        idx_copy(i + 1, nxt).wait()
        issue_gathers(nxt)

    @pl.when(i + 2 < n)
    def _():
        idx_copy(i + 2, slot).start()

    for kk in range(TOP_K):
        pltpu.make_async_copy(yb_hbm.at[pl.ds(0, TOK_TILE * TOK_ROWS), :], buf.at[slot, kk], sem.at[slot]).wait()

    gates = gates_ref[...]
    chunks = []
    for cc in range(TOK_ROWS):
        acc = gates[:, 0:1] * _load_token_chunk(buf.at[slot, 0], cc, TOK_TILE)
        for kk in range(1, TOP_K):
            acc = acc + gates[:, kk:kk + 1] * _load_token_chunk(buf.at[slot, kk], cc, TOK_TILE)
        chunks.append(acc)
    moe = jnp.concatenate(chunks, axis=1)
    y = _layer_norm(alpha * x1_ref[...] + moe, lng_ref[...], lnb_ref[...])
    y_ref[...] = y
    ybf_ref[...] = _bf(y)


def _combine(dest2, route, x1, lng, lnb, yb, alpha):
    nt = x1.shape[0]
    kern = functools.partial(_combine_kernel, alpha=alpha)
    return pl.pallas_call(
        kern,
        out_shape=(jax.ShapeDtypeStruct((nt, D_MODEL), F32),
                   jax.ShapeDtypeStruct((nt, D_MODEL), BF16)),
        grid=(nt // TOK_TILE,),
        in_specs=[pl.BlockSpec(memory_space=pl.ANY),
                  pl.BlockSpec((TOK_TILE, LANE), lambda i: (i, 0)),
                  pl.BlockSpec((TOK_TILE, D_MODEL), lambda i: (i, 0)),
                  pl.BlockSpec((1, D_MODEL), lambda i: (0, 0)),
                  pl.BlockSpec((1, D_MODEL), lambda i: (0, 0)),
                  pl.BlockSpec(memory_space=pl.ANY)],
        out_specs=(pl.BlockSpec((TOK_TILE, D_MODEL), lambda i: (i, 0)),
                   pl.BlockSpec((TOK_TILE, D_MODEL), lambda i: (i, 0))),
        scratch_shapes=[pltpu.SMEM((2, TOK_TILE * TOP_K), jnp.int32),
                        pltpu.VMEM((2, TOP_K, TOK_TILE * TOK_ROWS, LANE), F32),
                        pltpu.SemaphoreType.DMA((2,)), pltpu.SemaphoreType.DMA((2,))],
        compiler_params=pltpu.CompilerParams(
            dimension_semantics=("arbitrary",), vmem_limit_bytes=VMEM_LIMIT),
        name="moe_combine_ln",
    )(dest2, route, x1, lng, lnb, yb)


def _routing_tables(top_i):
    m = top_i.size
    flat_e = top_i.reshape(-1)
    onehot = (flat_e[:, None] == jnp.arange(N_EXPERTS, dtype=jnp.int32)[None, :]).astype(jnp.int32)
    oh3 = onehot.reshape(m // LANE, LANE, N_EXPERTS)
    tri = jnp.tril(jnp.ones((LANE, LANE), F32))
    within = jnp.einsum("ij,tjk->tik", tri, oh3.astype(F32)).astype(jnp.int32)
    tile_tot = within[:, -1, :]
    tile_off = jnp.cumsum(tile_tot, axis=0) - tile_tot
    csum = (within + tile_off[:, None, :]).reshape(m, N_EXPERTS)
    rank = jnp.sum(onehot * csum, axis=1) - 1
    counts = csum[-1]
    padded = (counts + MOE_BLK - 1) // MOE_BLK * MOE_BLK
    pad_ends = jnp.cumsum(padded)
    pad_starts = pad_ends - padded
    dest = jnp.sum(onehot * pad_starts[None, :], axis=1) + rank
    nblk = m // MOE_BLK + N_EXPERTS
    blk_start = jnp.arange(nblk, dtype=jnp.int32) * MOE_BLK
    block_e = jnp.minimum(jnp.sum((blk_start[:, None] >= pad_ends[None, :]).astype(jnp.int32), axis=1),
                          N_EXPERTS - 1)
    n_used = (pad_ends[-1] // MOE_BLK).astype(jnp.int32).reshape(1)
    return dest.astype(jnp.int32), block_e.astype(jnp.int32), n_used, nblk


def kernel(x_prompt, x_sample, state_gdn, state_gdn_conv, state_ssm, state_ssm_conv, w_in, gdn_conv_w,
           gdn_a_log, gdn_dt_bias, gdn_norm_w, ssm_conv_w, ssm_conv_b, ssm_a_log, ssm_dt_bias, ssm_d,
           ssm_norm_w, w_br_gdn, w_br_ssm, w_out, ln1_g, ln1_b, router_w, router_b, exp_w_gate,
           exp_b_gate, exp_w_up, exp_b_up, exp_w_down, exp_b_down, ln2_g, ln2_b):
    batch, seq, _ = x_prompt.shape
    dec = x_sample.shape[0]
    depth = w_in.shape[0]
    n_p = batch * seq
    nt = n_p + dec
    alpha = (2.0 * depth) ** 0.25
    gw = SSM_DINNER // SSM_GROUPS
    tm = _pick(nt, (384, 256, 128, 64, 32, 16))
    assert seq % CHUNK == 0 and dec % CHUNK == 0 and nt % TOK_TILE == 0 and (nt * TOP_K) % MOE_BLK == 0

    x = jnp.concatenate([x_prompt.reshape(n_p, D_MODEL), x_sample.reshape(dec, D_MODEL)], axis=0)
    x_bf = _bf(x)

    o_zg = GDN_CONV_CH
    o_b = o_zg + GDN_VDIM
    o_a = o_b + GDN_HEADS
    o_zs = o_a + GDN_HEADS
    o_x = o_zs + SSM_DINNER
    o_dt = o_x + SSM_CONV_CH
    o_ga = o_dt + SSM_HEADS

    rows = jnp.arange(SMALL_W, dtype=jnp.int32)[:, None]
    lanes = jnp.arange(SSM_DINNER, dtype=jnp.int32)[None, :]
    expand_all = (rows == COL_DT + lanes // SSM_HEADDIM).astype(BF16)

    outs = {k: [] for k in ("gdn_p", "gconv_p", "gconv_s", "ssm_p", "sconv_p", "sconv_s")}
    ssm_all = state_ssm.reshape(depth, dec, SSM_DINNER, SSM_DSTATE)
    gdn_s = ssm_s = None
    xb = None
    for l in range(depth):
        w = w_in[l]
        w_big = _bf(jnp.concatenate([w[:, :o_b], w[:, o_zs:o_dt], w[:, o_ga:]], axis=1))
        w_small = _bf(jnp.concatenate(
            [w[:, o_b:o_zs], w[:, o_dt:o_ga],
             jnp.zeros((D_MODEL, SMALL_W - 2 * GDN_HEADS - SSM_HEADS), F32)], axis=1))
        zpad = jnp.zeros((SMALL_W - COL_DT - SSM_HEADS,), F32)
        pv = jnp.zeros((SUBLANE, SMALL_W), F32)
        pv = pv.at[0].set(jnp.concatenate([jnp.zeros((COL_A,), F32), gdn_dt_bias[l], ssm_dt_bias[l], zpad]))
        pv = pv.at[1].set(jnp.concatenate([jnp.zeros((COL_A,), F32), gdn_a_log[l], ssm_a_log[l], zpad]))
        d_exp = jnp.repeat(ssm_d[l], SSM_HEADDIM).reshape(1, SSM_DINNER)
        gnw = gdn_norm_w[l].reshape(1, LANE)
        snw = ssm_norm_w[l].reshape(1, SSM_DINNER)
        scb = ssm_conv_b[l].reshape(1, SSM_CONV_CH)

        proj = _matmul(x_bf, w_big, tm, W_BIG // 4)
        small = _matmul(x_bf, w_small, tm, SMALL_W)

        proj_s = lax.slice(proj, (n_p, 0), (nt, W_BIG))
        small_s = lax.slice(small, (n_p, 0), (nt, SMALL_W))
        og_s, yz_s, gdn_s, ssm_s = _sample_mix(l, proj_s, small_s, state_gdn_conv[l], state_ssm_conv[l],
                                               state_gdn, ssm_all, gdn_s, ssm_s, gdn_conv_w[l],
                                               ssm_conv_w[l], scb, pv, gnw, expand_all, d_exp, snw)
        og, s_p = _gdn_prompt(proj, small, gdn_conv_w[l], pv, gnw, og_s, batch, seq)
        yz, h_p = _ssd_prompt(proj, small, ssm_conv_w[l], scb, pv, expand_all, d_exp, snw, yz_s, batch, seq)

        rw = jnp.concatenate([router_w[l], jnp.zeros((D_MODEL, LANE - N_EXPERTS), F32)], axis=1)
        rb = jnp.concatenate([router_b[l], jnp.full((LANE - N_EXPERTS,), NEG_BIG, F32)]).reshape(1, LANE)
        x1, x1t, route = _merge(og, yz, proj, x, _bf(w_br_gdn[l]), _bf(w_br_ssm[l]), _bf(w_out[l]),
                                ln1_g[l].reshape(1, D_MODEL), ln1_b[l].reshape(1, D_MODEL), rw, rb, alpha, tm)

        top_i = route[:, TOP_K:2 * TOP_K].astype(jnp.int32)
        dest, block_e, n_used, nblk = _routing_tables(top_i)
        dest2 = dest.reshape(nt // TOK_TILE, TOK_TILE * TOP_K)
        xb = _dispatch(dest2, x1t, jnp.zeros((nblk * MOE_BLK * TOK_ROWS, LANE), F32) if xb is None else xb)
        yb = _experts(l, block_e, n_used, xb, exp_w_gate, exp_w_up, exp_w_down,
                      exp_b_gate, exp_b_up, exp_b_down)
        x, x_bf = _combine(dest2, route, x1, ln2_g[l].reshape(1, D_MODEL), ln2_b[l].reshape(1, D_MODEL),
                           yb, alpha)

        outs["gdn_p"].append(s_p)
        outs["ssm_p"].append(h_p.reshape(batch, SSM_HEADS, SSM_HEADDIM, SSM_DSTATE))
        tails_g = [lax.slice(proj, (b * seq + seq - (CONV_W - 1), 0), (b * seq + seq, GDN_CONV_CH))
                   for b in range(batch)]
        tails_s = [lax.slice(proj, (b * seq + seq - (CONV_W - 1), o_x - 2 * GDN_HEADS),
                             (b * seq + seq, o_x - 2 * GDN_HEADS + SSM_CONV_CH)) for b in range(batch)]
        outs["gconv_p"].append(jnp.stack(tails_g))
        outs["sconv_p"].append(jnp.stack(tails_s))
        outs["gconv_s"].append(jnp.concatenate(
            [state_gdn_conv[l][:, 1:], proj_s[:, None, :GDN_CONV_CH]], axis=1))
        xbc_off = o_x - 2 * GDN_HEADS
        outs["sconv_s"].append(jnp.concatenate(
            [state_ssm_conv[l][:, 1:], proj_s[:, None, xbc_off:xbc_off + SSM_CONV_CH]], axis=1))

    yp = x[:n_p].reshape(batch, seq, D_MODEL)
    ys = x[n_p:].reshape(dec, 1, D_MODEL)
    return (yp, ys, jnp.stack(outs["gdn_p"]), gdn_s, jnp.stack(outs["gconv_p"]),
            jnp.stack(outs["gconv_s"]), jnp.stack(outs["ssm_p"]),
            ssm_s.reshape(depth, dec, SSM_HEADS, SSM_HEADDIM, SSM_DSTATE),
            jnp.stack(outs["sconv_p"]), jnp.stack(outs["sconv_s"]))
```

```python
import functools

import jax
import jax.numpy as jnp
from jax import lax
from jax.experimental import pallas as pl
from jax.experimental.pallas import tpu as pltpu

F32 = jnp.float32
BF16 = jnp.bfloat16

D_MODEL = 1024
GDN_HEADS = 8
GDN_DK = 128
GDN_VDIM = 1024
GDN_CONV_CH = 3072
SSM_HEADS = 32
SSM_HEADDIM = 64
SSM_GROUPS = 4
SSM_DINNER = 2048
SSM_DSTATE = 128
SSM_CONV_CH = 3072
CONV_W = 4
N_EXPERTS = 32
TOP_K = 4
SWIGLU_ALPHA = 1.702
SWIGLU_LIMIT = 7.0
LN_EPS = 1e-5
RMS_EPS = 1e-6
L2_EPS = 1e-6
NEG_BIG = -1e30

W_BIG = 11264
SMALL_W = 128
COL_B, COL_A, COL_DT = 0, 8, 16

LANE = 128
SUBLANE = 8
CHUNK = 128
GDN_HB = 8
MOE_BLK = 512
TOK_TILE = 128
TOK_ROWS = D_MODEL // LANE
SAMPLE_SEQS = 2
VMEM_LIMIT = 56 * 1024 * 1024


def _pick(n, cands):
    for c in cands:
        if n % c == 0:
            return c
    raise ValueError(f"no tile for {n}")


def _bf(x):
    return x.astype(BF16)


def _dot(a, b, prec=None):
    return jnp.dot(a, b, preferred_element_type=F32, precision=prec)


def _dot_nt(a, b):
    return lax.dot_general(a, b, (((1,), (1,)), ((), ())), preferred_element_type=F32)


def _dot_tn(a, b):
    return lax.dot_general(a, b, (((0,), (0,)), ((), ())), preferred_element_type=F32)


def _split3(x):
    hi = _bf(x)
    r = x - hi.astype(F32)
    mid = _bf(r)
    return hi, mid, _bf(r - mid.astype(F32))


def _dot_sel_rhs(x, sel):
    hi, mid, lo = _split3(x)
    return (_dot(lo, sel) + _dot(mid, sel)) + _dot(hi, sel)


def _dot_sel_lhs(sel, x):
    hi, mid, lo = _split3(x)
    return (_dot(sel, lo) + _dot(sel, mid)) + _dot(sel, hi)


def _sigmoid(x):
    return jax.nn.sigmoid(x)


def _silu(x):
    return x * jax.nn.sigmoid(x)


def _softplus(x):
    return jnp.maximum(x, 0.0) + jnp.log(1.0 + jnp.exp(-jnp.abs(x)))


def _layer_norm(x, g, b):
    mu = jnp.mean(x, axis=-1, keepdims=True)
    xc = x - mu
    var = jnp.mean(xc * xc, axis=-1, keepdims=True)
    return xc * lax.rsqrt(var + LN_EPS) * g + b


def _small_act(raw, pv):
    col = lax.broadcasted_iota(jnp.int32, raw.shape, 1)
    sp = _softplus(raw + pv[0:1, :])
    act = jnp.where(col < COL_A, _sigmoid(raw), sp)
    gda = sp * (-jnp.exp(pv[1:2, :]))
    return act, gda


def _mm_kernel(x_ref, w_ref, o_ref):
    o_ref[...] = _dot(x_ref[...], w_ref[...])


def _matmul(x, w, tm, tn):
    m, k = x.shape
    n = w.shape[1]
    return pl.pallas_call(
        _mm_kernel,
        out_shape=jax.ShapeDtypeStruct((m, n), F32),
        grid=(n // tn, m // tm),
        in_specs=[pl.BlockSpec((tm, k), lambda j, i: (i, 0)),
                  pl.BlockSpec((k, tn), lambda j, i: (0, j))],
        out_specs=pl.BlockSpec((tm, tn), lambda j, i: (i, j)),
        compiler_params=pltpu.CompilerParams(
            dimension_semantics=("parallel", "parallel"), vmem_limit_bytes=VMEM_LIMIT),
        name="in_proj",
    )(x, w)


def _tri_inv_all(mats, ii, jj, c):
    eye = (ii == jj).astype(F32)
    pair = (ii >> 1) == (jj >> 1)
    ts = [eye - jnp.where(pair, a, 0.0) for a in mats]
    abs_ = [_bf(a) for a in mats]
    s = 1
    while (2 << s) <= c:
        same_outer = (ii >> (s + 1)) == (jj >> (s + 1))
        same_inner = (ii >> s) == (jj >> s)
        off = _bf((same_outer & jnp.logical_not(same_inner)).astype(F32))
        tbs = [_bf(t) for t in ts]
        tes = [_dot(tb, ab * off) for tb, ab in zip(tbs, abs_)]
        ts = [t - _dot(_bf(te), tb) for t, te, tb in zip(ts, tes, tbs)]
        s += 1
    return ts


def _with_sample_rows(chunk_fn, n_in):
    def kern(*refs):
        b = pl.program_id(0)
        nb = pl.num_programs(0) - 1
        tail_ref = refs[n_in]
        out_ref = refs[n_in + 1]

        @pl.when(b < nb)
        def _():
            chunk_fn(*refs[:n_in], *refs[n_in + 1:])

        @pl.when(b == nb)
        def _():
            out_ref[...] = tail_ref[...]

    return kern


def _gdn_chunk(q_ref, k_ref, v_ref, zg_ref, small_ref, cwq_ref, cwk_ref, cwv_ref, pv_ref, nw_ref,
               o_ref, sfin_ref, xf, tail, s_scr):
    assert GDN_HB == GDN_HEADS
    c_len = q_ref.shape[0]
    c = pl.program_id(2)
    nc = pl.num_programs(2)

    @pl.when(c == 0)
    def _():
        tail[...] = jnp.zeros_like(tail)
        s_scr[...] = jnp.zeros_like(s_scr)

    for p, r in enumerate((q_ref, k_ref, v_ref)):
        xf[p, 0:SUBLANE, :] = tail[p]
        xf[p, SUBLANE:SUBLANE + c_len, :] = r[...]
        tail[p] = r[c_len - SUBLANE:c_len, :]
    cws = (cwq_ref, cwk_ref, cwv_ref)

    def conv(p, hs):
        acc = None
        for i in range(CONV_W):
            term = cws[p][i:i + 1, hs] * xf[p, pl.ds(SUBLANE - (CONV_W - 1) + i, c_len), hs]
            acc = term if acc is None else acc + term
        return _silu(acc)

    act, gda = _small_act(small_ref[...], pv_ref[...])
    ii = lax.broadcasted_iota(jnp.int32, (c_len, c_len), 0)
    jj = lax.broadcasted_iota(jnp.int32, (c_len, c_len), 1)
    incl = ii >= jj
    strict = ii > jj
    gcum = _dot_sel_lhs(_bf(incl.astype(F32)), gda)
    gcum_t = gcum.T
    nw = nw_ref[...]

    heads = range(GDN_HB)
    hsl = [slice(hh * LANE, (hh + 1) * LANE) for hh in heads]
    betas = [act[:, COL_B + hh:COL_B + hh + 1] for hh in heads]
    gcs = [gcum[:, COL_A + hh:COL_A + hh + 1] for hh in heads]
    grs = [gcum_t[COL_A + hh:COL_A + hh + 1, :] for hh in heads]
    ks = []
    for hh in heads:
        k = conv(1, hsl[hh])
        ks.append(k * lax.rsqrt(jnp.sum(k * k, axis=-1, keepdims=True) + L2_EPS))
    kbs = [ks[hh] * betas[hh] for hh in heads]
    kbfs = [_bf(k) for k in ks]
    kks = [_dot_nt(_bf(kbs[hh]), kbfs[hh]) for hh in heads]
    gams = [jnp.exp(jnp.where(incl, gcs[hh] - grs[hh], NEG_BIG)) for hh in heads]
    amats = [jnp.where(strict, kks[hh] * gams[hh], 0.0) for hh in heads]
    qs = []
    for hh in heads:
        q = conv(0, hsl[hh])
        qs.append(q * lax.rsqrt(jnp.sum(q * q, axis=-1, keepdims=True) + L2_EPS) * (GDN_DK ** -0.5))
    qks = [_dot_nt(_bf(qs[hh]), kbfs[hh]) * gams[hh] for hh in heads]
    egs = [jnp.exp(gcs[hh]) for hh in heads]
    rhs = [_bf(jnp.concatenate([conv(2, hsl[hh]) * betas[hh], kbs[hh] * egs[hh]], axis=1)) for hh in heads]
    ts = _tri_inv_all(amats, ii, jj, c_len)
    uws = [_dot(_bf(ts[hh]), rhs[hh]) for hh in heads]
    glasts = [gcs[hh][c_len - 1:c_len, :] for hh in heads]
    s_olds = [s_scr[hh] for hh in heads]
    sbs = [_bf(s) for s in s_olds]
    v_news = [uws[hh][:, :LANE] - _dot(_bf(uws[hh][:, LANE:]), sbs[hh]) for hh in heads]
    vnbs = [_bf(v) for v in v_news]
    os_ = [_dot(_bf(qs[hh] * egs[hh]), sbs[hh]) + _dot(_bf(qks[hh]), vnbs[hh]) for hh in heads]
    for hh in heads:
        kdec = ks[hh] * jnp.exp(glasts[hh] - gcs[hh])
        s_scr[hh] = s_olds[hh] * jnp.exp(glasts[hh]) + _dot_tn(_bf(kdec), vnbs[hh])
    for hh in heads:
        o = os_[hh]
        o = (o * lax.rsqrt(jnp.mean(o * o, axis=-1, keepdims=True) + RMS_EPS) * nw
             * _silu(zg_ref[:, hsl[hh]]))
        o_ref[:, hsl[hh]] = _bf(o)

    @pl.when(c == nc - 1)
    def _():
        sfin_ref[0] = s_scr[...]


def _sample_tail_maps(batch, nc, n_tail):
    def seq(b):
        return jnp.minimum(b, batch - 1)

    def tail(c):
        return jnp.minimum(c, n_tail - 1)

    def in_row(b, c):
        return seq(b) * nc + c

    def out_row(b, c):
        return jnp.where(b < batch, b * nc + c, batch * nc + tail(c))

    return seq, tail, in_row, out_row


def _gdn_prompt(proj, small, conv_w, pv, norm_w, og_s, batch, seq):
    nc = seq // CHUNK
    n_tail = og_s.shape[0] // CHUNK
    hbw = GDN_HB * LANE
    ngrp = GDN_HEADS // GDN_HB
    kq, kk, kv, kz = 0, GDN_VDIM // hbw, 2 * GDN_VDIM // hbw, 3 * GDN_VDIM // hbw
    seq_of, tail_of, in_row, out_row = _sample_tail_maps(batch, nc, n_tail)

    def pspec(off):
        return pl.BlockSpec((CHUNK, hbw), lambda b, hg, c: (in_row(b, c), off + hg))

    def wspec(off):
        return pl.BlockSpec((CONV_W, hbw), lambda b, hg, c: (0, off + hg))

    return pl.pallas_call(
        _with_sample_rows(_gdn_chunk, 10),
        out_shape=(jax.ShapeDtypeStruct((batch * seq + og_s.shape[0], GDN_VDIM), BF16),
                   jax.ShapeDtypeStruct((batch, GDN_HEADS, GDN_DK, LANE), F32)),
        grid=(batch + 1, ngrp, nc),
        in_specs=[pspec(kq), pspec(kk), pspec(kv), pspec(kz),
                  pl.BlockSpec((CHUNK, SMALL_W), lambda b, hg, c: (in_row(b, c), 0)),
                  wspec(kq), wspec(kk), wspec(kv),
                  pl.BlockSpec((SUBLANE, SMALL_W), lambda b, hg, c: (0, 0)),
                  pl.BlockSpec((1, LANE), lambda b, hg, c: (0, 0)),
                  pl.BlockSpec((CHUNK, hbw), lambda b, hg, c: (tail_of(c), hg))],
        out_specs=(pl.BlockSpec((CHUNK, hbw), lambda b, hg, c: (out_row(b, c), hg)),
                   pl.BlockSpec((1, GDN_HB, GDN_DK, LANE), lambda b, hg, c: (seq_of(b), hg, 0, 0))),
        scratch_shapes=[pltpu.VMEM((3, CHUNK + SUBLANE, hbw), F32),
                        pltpu.VMEM((3, SUBLANE, hbw), F32),
                        pltpu.VMEM((GDN_HB, GDN_DK, LANE), F32)],
        compiler_params=pltpu.CompilerParams(
            dimension_semantics=("arbitrary", "arbitrary", "arbitrary"), vmem_limit_bytes=VMEM_LIMIT),
        name="gdn_prompt",
    )(proj, proj, proj, proj, small, conv_w, conv_w, conv_w, pv, norm_w, og_s)


def _ssd_chunk(xs_ref, b_ref, c_ref, zs_ref, small_ref, cwx_ref, cwb_ref, cwc_ref, cbx_ref, cbb_ref,
               cbc_ref, pv_ref, ex_ref, dexp_ref, nw_ref, yz_ref, hfin_ref, xf, tail, ht, ydiag):
    c_len = xs_ref.shape[0]
    gw = SSM_DINNER // SSM_GROUPS
    hpg = gw // SSM_HEADDIM
    gn = SSM_GROUPS * SSM_DSTATE
    c = pl.program_id(1)
    nc = pl.num_programs(1)

    @pl.when(c == 0)
    def _():
        tail[...] = jnp.zeros_like(tail)
        ht[...] = jnp.zeros_like(ht)

    parts = ((xs_ref, cwx_ref, cbx_ref, 0, SSM_DINNER), (b_ref, cwb_ref, cbb_ref, SSM_DINNER, gn),
             (c_ref, cwc_ref, cbc_ref, SSM_DINNER + gn, gn))
    convs = []
    for r, cw, cb, off, wd in parts:
        sl = slice(off, off + wd)
        xf[0:SUBLANE, sl] = tail[:, sl]
        xf[SUBLANE:SUBLANE + c_len, sl] = r[...]
        tail[:, sl] = r[c_len - SUBLANE:c_len, :]
        acc = cb[...]
        for i in range(CONV_W):
            acc = acc + cw[i:i + 1, :] * xf[pl.ds(SUBLANE - (CONV_W - 1) + i, c_len), sl]
        convs.append(_silu(acc))
    xs, bm_all, cm_all = convs

    act, gda = _small_act(small_ref[...], pv_ref[...])
    ii = lax.broadcasted_iota(jnp.int32, (c_len, c_len), 0)
    jj = lax.broadcasted_iota(jnp.int32, (c_len, c_len), 1)
    incl = ii >= jj
    acs = _dot_sel_lhs(_bf(incl.astype(F32)), gda)
    acs_t = acs.T
    ex = ex_ref[...]
    dt_x = _dot_sel_rhs(act, ex)
    acs_x = _dot_sel_rhs(acs, ex)
    last = acs_x[c_len - 1:c_len, :]
    xdt = xs * dt_x
    xdec = _bf(xdt * jnp.exp(last - acs_x))
    lane = lax.broadcasted_iota(jnp.int32, (c_len, LANE), 1)
    lo_half = lane < SSM_HEADDIM

    groups = range(SSM_GROUPS)
    bms = [_bf(bm_all[:, g * SSM_DSTATE:(g + 1) * SSM_DSTATE]) for g in groups]
    cms = [_bf(cm_all[:, g * SSM_DSTATE:(g + 1) * SSM_DSTATE]) for g in groups]
    cbs = [_dot_nt(cms[g], bms[g]) for g in groups]
    h_olds = [ht[:, g * gw:(g + 1) * gw] for g in groups]
    y_offs = [_dot(cms[g], _bf(h_olds[g])) for g in groups]
    for g in groups:
        for pr in range(hpg // 2):
            ps = slice(g * gw + pr * LANE, g * gw + (pr + 1) * LANE)
            xpair = xdt[:, ps]
            acc = None
            for half in range(2):
                head = g * hpg + pr * 2 + half
                ac = acs[:, COL_DT + head:COL_DT + head + 1]
                ar = acs_t[COL_DT + head:COL_DT + head + 1, :]
                sc = cbs[g] * jnp.exp(jnp.where(incl, ac - ar, NEG_BIG))
                keep = lo_half if half == 0 else jnp.logical_not(lo_half)
                term = _dot(_bf(sc), _bf(jnp.where(keep, xpair, 0.0)))
                acc = term if acc is None else acc + term
            ydiag[:, ps] = acc
    for g in groups:
        gs = slice(g * gw, (g + 1) * gw)
        ht[:, gs] = h_olds[g] * jnp.exp(last[:, gs]) + _dot_tn(bms[g], xdec[:, gs])
    y = ydiag[...] + jnp.concatenate(y_offs, axis=1) * jnp.exp(acs_x) + dexp_ref[...] * xs
    yz = y * _silu(zs_ref[...])
    nw = nw_ref[...]
    for g in groups:
        gs = slice(g * gw, (g + 1) * gw)
        yg = yz[:, gs]
        yz_ref[:, gs] = _bf(yg * lax.rsqrt(jnp.mean(yg * yg, axis=-1, keepdims=True) + RMS_EPS) * nw[:, gs])

    @pl.when(c == nc - 1)
    def _():
        hfin_ref[0] = ht[...].T


def _ssd_prompt(proj, small, conv_w, conv_b, pv, expand, d_exp, norm_w, yz_s, batch, seq):
    nc = seq // CHUNK
    gn = SSM_GROUPS * SSM_DSTATE
    x_off = (GDN_CONV_CH + GDN_VDIM + SSM_DINNER)
    z_off = GDN_CONV_CH + GDN_VDIM
    kx = x_off // SSM_DINNER
    kb = (x_off + SSM_DINNER) // gn
    kz = z_off // SSM_DINNER
    wb = SSM_DINNER // gn

    n_tail = yz_s.shape[0] // CHUNK
    seq_of, tail_of, row, out_row = _sample_tail_maps(batch, nc, n_tail)

    in_specs = [
        pl.BlockSpec((CHUNK, SSM_DINNER), lambda b, c: (row(b, c), kx)),
        pl.BlockSpec((CHUNK, gn), lambda b, c: (row(b, c), kb)),
        pl.BlockSpec((CHUNK, gn), lambda b, c: (row(b, c), kb + 1)),
        pl.BlockSpec((CHUNK, SSM_DINNER), lambda b, c: (row(b, c), kz)),
        pl.BlockSpec((CHUNK, SMALL_W), lambda b, c: (row(b, c), 0)),
        pl.BlockSpec((CONV_W, SSM_DINNER), lambda b, c: (0, 0)),
        pl.BlockSpec((CONV_W, gn), lambda b, c: (0, wb)),
        pl.BlockSpec((CONV_W, gn), lambda b, c: (0, wb + 1)),
        pl.BlockSpec((1, SSM_DINNER), lambda b, c: (0, 0)),
        pl.BlockSpec((1, gn), lambda b, c: (0, wb)),
        pl.BlockSpec((1, gn), lambda b, c: (0, wb + 1)),
        pl.BlockSpec((SUBLANE, SMALL_W), lambda b, c: (0, 0)),
        pl.BlockSpec((SMALL_W, SSM_DINNER), lambda b, c: (0, 0)),
        pl.BlockSpec((1, SSM_DINNER), lambda b, c: (0, 0)),
        pl.BlockSpec((1, SSM_DINNER), lambda b, c: (0, 0)),
        pl.BlockSpec((CHUNK, SSM_DINNER), lambda b, c: (tail_of(c), 0)),
    ]
    return pl.pallas_call(
        _with_sample_rows(_ssd_chunk, 15),
        out_shape=(jax.ShapeDtypeStruct((batch * seq + yz_s.shape[0], SSM_DINNER), BF16),
                   jax.ShapeDtypeStruct((batch, SSM_DINNER, SSM_DSTATE), F32)),
        grid=(batch + 1, nc),
        in_specs=in_specs,
        out_specs=(pl.BlockSpec((CHUNK, SSM_DINNER), lambda b, c: (out_row(b, c), 0)),
                   pl.BlockSpec((1, SSM_DINNER, SSM_DSTATE), lambda b, c: (seq_of(b), 0, 0))),
        scratch_shapes=[pltpu.VMEM((CHUNK + SUBLANE, SSM_CONV_CH), F32),
                        pltpu.VMEM((SUBLANE, SSM_CONV_CH), F32),
                        pltpu.VMEM((SSM_DSTATE, SSM_DINNER), F32),
                        pltpu.VMEM((CHUNK, SSM_DINNER), F32)],
        compiler_params=pltpu.CompilerParams(
            dimension_semantics=("arbitrary", "arbitrary"), vmem_limit_bytes=VMEM_LIMIT),
        name="ssd_prompt",
    )(proj, proj, proj, proj, small, conv_w, conv_w, conv_w, conv_b, conv_b, conv_b, pv, expand,
      d_exp, norm_w, yz_s)


def _sample_prep_kernel(small_ref, pv_ref, ex_ref, act_ref, gda_ref, dtx_ref, dax_ref):
    act, gda = _small_act(small_ref[...], pv_ref[...])
    act_ref[...] = act
    gda_ref[...] = gda
    ex = ex_ref[...]
    dtx_ref[...] = _dot_sel_rhs(act, ex)
    dax_ref[...] = _dot_sel_rhs(gda, ex)


def _sample_prep(small_s, pv, expand_all):
    n = small_s.shape[0]
    return pl.pallas_call(
        _sample_prep_kernel,
        out_shape=(jax.ShapeDtypeStruct((n, SMALL_W), F32), jax.ShapeDtypeStruct((n, SMALL_W), F32),
                   jax.ShapeDtypeStruct((n, SSM_DINNER), F32), jax.ShapeDtypeStruct((n, SSM_DINNER), F32)),
        name="sample_prep",
    )(small_s, pv, expand_all)


def _sample_kernel(proj_ref, act_ref, gda_ref, dtx_ref, dax_ref, gcs_ref, scs_ref, s0_ref, h0_ref, gcw_ref,
                   scw_ref, scb_ref, gnw_ref, dexp_ref, snw_ref, *rest, n_alias):
    og_ref, yz_ref, s_ref, h_ref, stk, stk2 = rest[n_alias:]
    if n_alias == 0:
        if s_ref.shape[0] > 1:
            s_ref[1:] = jnp.zeros((s_ref.shape[0] - 1,) + s_ref.shape[1:], F32)
            h_ref[1:] = jnp.zeros((h_ref.shape[0] - 1,) + h_ref.shape[1:], F32)
        s_ref = s_ref.at[0]
        h_ref = h_ref.at[0]
    for q in range(SAMPLE_SEQS):
        _sample_one(proj_ref.at[q], act_ref.at[q], gda_ref.at[q], dtx_ref.at[q], dax_ref.at[q], gcs_ref.at[q],
                    scs_ref.at[q], s0_ref.at[q], h0_ref.at[q], gcw_ref, scw_ref, scb_ref, gnw_ref, dexp_ref,
                    snw_ref, og_ref.at[q], yz_ref.at[q], s_ref.at[q], h_ref.at[q], stk.at[q], stk2.at[q])


def _sample_one(proj_ref, act_ref, gda_ref, dtx_ref, dax_ref, gcs_ref, scs_ref, s0_ref, h0_ref, gcw_ref,
                scw_ref, scb_ref, gnw_ref, dexp_ref, snw_ref, og_ref, yz_ref, s_ref, h_ref, stk, stk2):
    gw = SSM_DINNER // SSM_GROUPS
    z_off = GDN_CONV_CH + GDN_VDIM
    x_off = z_off + SSM_DINNER

    def conv1(state_ref, w_ref, new_row):
        acc = w_ref[CONV_W - 1:CONV_W, :] * new_row
        for i in range(CONV_W - 1):
            acc = acc + w_ref[i:i + 1, :] * state_ref[i:i + 1, :]
        return acc

    act = act_ref[...]
    gda = gda_ref[...]

    qkv = _silu(conv1(gcs_ref, gcw_ref, proj_ref[:, 0:GDN_CONV_CH]))
    stk[...] = jnp.zeros_like(stk)
    qs, vs = [], []
    for h in range(GDN_HEADS):
        q = qkv[:, h * LANE:(h + 1) * LANE]
        k = qkv[:, GDN_VDIM + h * LANE:GDN_VDIM + (h + 1) * LANE]
        q = q * lax.rsqrt(jnp.sum(q * q, axis=-1, keepdims=True) + L2_EPS) * (GDN_DK ** -0.5)
        k = k * lax.rsqrt(jnp.sum(k * k, axis=-1, keepdims=True) + L2_EPS)
        stk[h:h + 1, :] = k
        stk[GDN_HEADS + h:GDN_HEADS + h + 1, :] = q
        vs.append(qkv[:, 2 * GDN_VDIM + h * LANE:2 * GDN_VDIM + (h + 1) * LANE])
    cols = stk[...].T
    gnw = gnw_ref[...]
    for h in range(GDN_HEADS):
        kc = cols[:, h:h + 1]
        qc = cols[:, GDN_HEADS + h:GDN_HEADS + h + 1]
        beta = act[:, COL_B + h:COL_B + h + 1]
        gh = gda[:, COL_A + h:COL_A + h + 1]
        sd = s0_ref[h] * jnp.exp(gh)
        v_old = jnp.sum(sd * kc, axis=0, keepdims=True)
        delta = (vs[h] - v_old) * beta
        s_new = sd + kc * delta
        s_ref[h] = s_new
        o = jnp.sum(s_new * qc, axis=0, keepdims=True)
        zg = proj_ref[:, GDN_CONV_CH + h * LANE:GDN_CONV_CH + (h + 1) * LANE]
        o = o * lax.rsqrt(jnp.mean(o * o, axis=-1, keepdims=True) + RMS_EPS) * gnw * _silu(zg)
        og_ref[:, h * LANE:(h + 1) * LANE] = _bf(o)

    xbc = _silu(conv1(scs_ref, scw_ref, proj_ref[:, x_off:x_off + SSM_CONV_CH]) + scb_ref[...])
    dexp = dexp_ref[...]
    stk2[...] = jnp.zeros_like(stk2)
    for g in range(SSM_GROUPS):
        gs = slice(g * gw, (g + 1) * gw)
        stk2[2 * g:2 * g + 1, :] = jnp.exp(dax_ref[:, gs])
        stk2[2 * g + 1:2 * g + 2, :] = xbc[:, gs] * dtx_ref[:, gs]
    cols2 = stk2[...].T
    for g in range(SSM_GROUPS):
        gs = slice(g * gw, (g + 1) * gw)
        xs = xbc[:, gs]
        bm = xbc[:, SSM_DINNER + g * SSM_DSTATE:SSM_DINNER + (g + 1) * SSM_DSTATE]
        cm = xbc[:, SSM_DINNER + SSM_GROUPS * SSM_DSTATE + g * SSM_DSTATE:
                 SSM_DINNER + SSM_GROUPS * SSM_DSTATE + (g + 1) * SSM_DSTATE]
        h_new = h0_ref[gs, :] * cols2[:, 2 * g:2 * g + 1] + cols2[:, 2 * g + 1:2 * g + 2] * bm
        h_ref[gs, :] = h_new
        cm16 = jnp.broadcast_to(cm, (2 * SUBLANE, SSM_DSTATE))
        y = _dot_nt(_bf(cm16), _bf(h_new))[0:1, :] + dexp[:, gs] * xs
        yz = y * _silu(proj_ref[:, z_off + g * gw:z_off + (g + 1) * gw])
        yz = yz * lax.rsqrt(jnp.mean(yz * yz, axis=-1, keepdims=True) + RMS_EPS) * snw_ref[:, gs]
        yz_ref[:, gs] = _bf(yz)


def _sample_mix(layer, proj_s, small_s, gconv_state, sconv_state, s_all, h_all, s_prev, h_prev, gconv_w,
                sconv_w, sconv_b, pv, gnorm_w, expand_all, d_exp, snorm_w):
    n = proj_s.shape[0]
    gw = SSM_DINNER // SSM_GROUPS
    proj3 = proj_s.reshape(n, 1, W_BIG)
    act, gda, dtx, dax = _sample_prep(small_s, pv, expand_all)

    def full(shape):
        nd = len(shape)
        return pl.BlockSpec(shape, lambda i: (0,) * nd)

    assert n % SAMPLE_SEQS == 0

    def per_seq(shape):
        nd = len(shape)
        return pl.BlockSpec((SAMPLE_SEQS,) + shape, lambda i: (i,) + (0,) * nd)

    def per_layer_seq(shape):
        nd = len(shape)
        return pl.BlockSpec((None, SAMPLE_SEQS) + shape, lambda i: (layer, i) + (0,) * nd)

    in_specs = [per_seq((1, W_BIG)), per_seq((1, SMALL_W)), per_seq((1, SMALL_W)),
                per_seq((1, SSM_DINNER)), per_seq((1, SSM_DINNER)),
                per_seq((CONV_W - 1, GDN_CONV_CH)), per_seq((CONV_W - 1, SSM_CONV_CH)),
                per_layer_seq((GDN_HEADS, GDN_DK, LANE)), per_layer_seq((SSM_DINNER, SSM_DSTATE)),
                full((CONV_W, GDN_CONV_CH)), full((CONV_W, SSM_CONV_CH)), full((1, SSM_CONV_CH)),
                full((1, LANE)), full((1, SSM_DINNER)), full((1, SSM_DINNER))]
    args = [proj3, act.reshape(n, 1, SMALL_W), gda.reshape(n, 1, SMALL_W), dtx.reshape(n, 1, SSM_DINNER),
            dax.reshape(n, 1, SSM_DINNER), gconv_state, sconv_state, s_all, h_all, gconv_w, sconv_w, sconv_b,
            gnorm_w, d_exp, snorm_w]
    aliases = {}
    if s_prev is not None:
        aliases = {len(args): 2, len(args) + 1: 3}
        in_specs += [pl.BlockSpec(memory_space=pl.ANY), pl.BlockSpec(memory_space=pl.ANY)]
        args += [s_prev, h_prev]
    if s_prev is None:
        depth = s_all.shape[0]
        state_specs = [pl.BlockSpec((depth, SAMPLE_SEQS, GDN_HEADS, GDN_DK, LANE), lambda i: (0, i, 0, 0, 0)),
                       pl.BlockSpec((depth, SAMPLE_SEQS, SSM_DINNER, SSM_DSTATE), lambda i: (0, i, 0, 0))]
    else:
        state_specs = [per_layer_seq((GDN_HEADS, GDN_DK, LANE)), per_layer_seq((SSM_DINNER, SSM_DSTATE))]
    kern = functools.partial(_sample_kernel, n_alias=len(aliases))
    og, yz, s_new, h_new = pl.pallas_call(
        kern,
        out_shape=(jax.ShapeDtypeStruct((n, 1, GDN_VDIM), BF16),
                   jax.ShapeDtypeStruct((n, 1, SSM_DINNER), BF16),
                   jax.ShapeDtypeStruct(s_all.shape, F32),
                   jax.ShapeDtypeStruct(h_all.shape, F32)),
        grid=(n // SAMPLE_SEQS,),
        in_specs=in_specs,
        out_specs=(per_seq((1, GDN_VDIM)), per_seq((1, SSM_DINNER)), state_specs[0], state_specs[1]),
        scratch_shapes=[pltpu.VMEM((SAMPLE_SEQS, LANE, LANE), F32), pltpu.VMEM((SAMPLE_SEQS, LANE, gw), F32)],
        input_output_aliases=aliases,
        compiler_params=pltpu.CompilerParams(
            dimension_semantics=("parallel",), vmem_limit_bytes=VMEM_LIMIT),
        name="sample_mix",
    )(*args)
    return og.reshape(n, GDN_VDIM), yz.reshape(n, SSM_DINNER), s_new, h_new


def _merge_kernel(og_ref, yz_ref, ga_ref, gb_ref, x_ref, wbg_ref, wbs_ref, wout_ref, lng_ref, lnb_ref,
                  rwh_ref, rwl_ref, rb_ref, x1_ref, x1t_ref, route_ref, *, alpha):
    a = _dot(og_ref[...], wbg_ref[...])
    b = _dot(yz_ref[...], wbs_ref[...])
    merged = _sigmoid(ga_ref[...]) * a + _sigmoid(gb_ref[...]) * b
    mix = _dot(_bf(merged), wout_ref[...])
    x1 = _layer_norm(alpha * x_ref[...] + mix, lng_ref[...], lnb_ref[...])
    x1_ref[...] = x1
    _store_token_tiles(x1t_ref, x1)

    xh = _bf(x1)
    xl = _bf(x1 - xh.astype(F32))
    rwh = rwh_ref[...]
    lg = (_dot(xh, rwl_ref[...]) + _dot(xl, rwh)) + _dot(xh, rwh) + rb_ref[...]
    colf = lax.broadcasted_iota(jnp.int32, lg.shape, 1).astype(F32)
    vals, idxs = [], []
    for _ in range(TOP_K):
        m = jnp.max(lg, axis=-1, keepdims=True)
        idx = jnp.min(jnp.where(lg == m, colf, float(LANE)), axis=-1, keepdims=True)
        vals.append(m)
        idxs.append(idx)
        lg = jnp.where(colf == idx, 2.0 * NEG_BIG, lg)
    es = [jnp.exp(v - vals[0]) for v in vals]
    den = es[0] + es[1] + es[2] + es[3]
    route = jnp.zeros_like(lg)
    for kk in range(TOP_K):
        route = jnp.where(colf == float(kk), es[kk] / den, route)
        route = jnp.where(colf == float(TOP_K + kk), idxs[kk], route)
    route_ref[...] = route


def _merge(og, yz, proj, x, wbg, wbs, wout, lng, lnb, rw, rb, alpha, tm):
    nt = x.shape[0]
    ka = (W_BIG - 2 * D_MODEL) // D_MODEL
    kern = functools.partial(_merge_kernel, alpha=alpha)
    rw_hi = _bf(rw)

    def full(shape):
        return pl.BlockSpec(shape, lambda i: (0, 0))

    return pl.pallas_call(
        kern,
        out_shape=(jax.ShapeDtypeStruct((nt, D_MODEL), F32),
                   jax.ShapeDtypeStruct((nt * TOK_ROWS, LANE), F32),
                   jax.ShapeDtypeStruct((nt, LANE), F32)),
        grid=(nt // tm,),
        in_specs=[pl.BlockSpec((tm, GDN_VDIM), lambda i: (i, 0)),
                  pl.BlockSpec((tm, SSM_DINNER), lambda i: (i, 0)),
                  pl.BlockSpec((tm, D_MODEL), lambda i: (i, ka)),
                  pl.BlockSpec((tm, D_MODEL), lambda i: (i, ka + 1)),
                  pl.BlockSpec((tm, D_MODEL), lambda i: (i, 0)),
                  full((GDN_VDIM, D_MODEL)), full((SSM_DINNER, D_MODEL)), full((D_MODEL, D_MODEL)),
                  full((1, D_MODEL)), full((1, D_MODEL)), full((D_MODEL, LANE)), full((D_MODEL, LANE)),
                  full((1, LANE))],
        out_specs=(pl.BlockSpec((tm, D_MODEL), lambda i: (i, 0)),
                   pl.BlockSpec((tm * TOK_ROWS, LANE), lambda i: (i, 0)),
                   pl.BlockSpec((tm, LANE), lambda i: (i, 0))),
        compiler_params=pltpu.CompilerParams(
            dimension_semantics=("parallel",), vmem_limit_bytes=VMEM_LIMIT),
        name="merge_ln_router",
    )(og, yz, proj, proj, x, wbg, wbs, wout, lng, lnb, rw_hi, _bf(rw - rw_hi.astype(F32)), rb)


def _store_token_tiles(ref, val):
    n = val.shape[0]
    for cc in range(TOK_ROWS):
        ref[pl.ds(cc, n, stride=TOK_ROWS), :] = val[:, cc * LANE:(cc + 1) * LANE]


def _load_token_chunk(ref, cc, n):
    return ref[pl.ds(cc, n, stride=TOK_ROWS), :]


def _token_rows(t):
    return pl.ds(pl.multiple_of(t * TOK_ROWS, TOK_ROWS), TOK_ROWS)


def _dispatch_kernel(dest_hbm, x_ref, xb_in, xb_out, idx, sem_idx, sem):
    del xb_in
    i = pl.program_id(0)
    cp = pltpu.make_async_copy(dest_hbm.at[i], idx, sem_idx)
    cp.start()
    cp.wait()

    def issue(r, carry):
        for kk in range(TOP_K):
            pltpu.make_async_copy(x_ref.at[_token_rows(r), :],
                                  xb_out.at[_token_rows(idx[r * TOP_K + kk]), :], sem).start(priority=kk % 2)
        return carry

    lax.fori_loop(0, TOK_TILE, issue, 0, unroll=8)
    for _ in range(TOP_K):
        pltpu.make_async_copy(x_ref, xb_out.at[pl.ds(0, TOK_TILE * TOK_ROWS), :], sem).wait()


def _dispatch(dest2, x1t, xb_zero):
    n_tiles = dest2.shape[0]
    return pl.pallas_call(
        _dispatch_kernel,
        out_shape=jax.ShapeDtypeStruct(xb_zero.shape, F32),
        grid=(n_tiles,),
        in_specs=[pl.BlockSpec(memory_space=pl.ANY),
                  pl.BlockSpec((TOK_TILE * TOK_ROWS, LANE), lambda i: (i, 0)),
                  pl.BlockSpec(memory_space=pl.ANY)],
        out_specs=pl.BlockSpec(memory_space=pl.ANY),
        scratch_shapes=[pltpu.SMEM((TOK_TILE * TOP_K,), jnp.int32),
                        pltpu.SemaphoreType.DMA, pltpu.SemaphoreType.DMA],
        input_output_aliases={2: 0},
        compiler_params=pltpu.CompilerParams(
            dimension_semantics=("arbitrary",), vmem_limit_bytes=VMEM_LIMIT),
        name="moe_dispatch",
    )(dest2, x1t, xb_zero)


def _expert_kernel(be_ref, nu_ref, x_ref, wg_ref, wu_ref, wd_ref, bg_ref, bu_ref, bd_ref, y_ref,
                   wgb, wub, wdb):
    j = pl.program_id(0)
    e = be_ref[j]
    prev = be_ref[jnp.maximum(j - 1, 0)]
    used = j < nu_ref[0]

    @pl.when(used & ((j == 0) | (e != prev)))
    def _():
        wgb[...] = _bf(wg_ref[...])
        wub[...] = _bf(wu_ref[...])
        wdb[...] = _bf(wd_ref[...])

    @pl.when(used)
    def _():
        x = jnp.concatenate([_bf(_load_token_chunk(x_ref, cc, MOE_BLK)) for cc in range(TOK_ROWS)], axis=1)
        gt = _dot(x, wgb[...]) + bg_ref[...]
        up = _dot(x, wub[...]) + bu_ref[...]
        gt = jnp.minimum(gt, SWIGLU_LIMIT)
        up = jnp.clip(up, -SWIGLU_LIMIT, SWIGLU_LIMIT)
        h = (up + 1.0) * (gt * _sigmoid(SWIGLU_ALPHA * gt))
        _store_token_tiles(y_ref, _dot(_bf(h), wdb[...]) + bd_ref[...])

    @pl.when(jnp.logical_not(used))
    def _():
        y_ref[...] = jnp.zeros_like(y_ref)


def _experts(layer, block_e, n_used, xb, wg, wu, wd, bg, bu, bd):
    rows = xb.shape[0]
    blk_rows = MOE_BLK * TOK_ROWS
    nblk = rows // blk_rows
    depth = wg.shape[0]
    d_e = wg.shape[-1]
    wspec_in = pl.BlockSpec((None, None, D_MODEL, d_e), lambda j, be, nu: (layer, be[j], 0, 0))
    wspec_out = pl.BlockSpec((None, None, d_e, D_MODEL), lambda j, be, nu: (layer, be[j], 0, 0))
    bspec_e = pl.BlockSpec((None, None, 1, d_e), lambda j, be, nu: (layer, be[j], 0, 0))
    bspec_d = pl.BlockSpec((None, None, 1, D_MODEL), lambda j, be, nu: (layer, be[j], 0, 0))
    grid_spec = pltpu.PrefetchScalarGridSpec(
        num_scalar_prefetch=2,
        grid=(nblk,),
        in_specs=[pl.BlockSpec((blk_rows, LANE), lambda j, be, nu: (j, 0)),
                  wspec_in, wspec_in, wspec_out, bspec_e, bspec_e, bspec_d],
        out_specs=pl.BlockSpec((blk_rows, LANE), lambda j, be, nu: (j, 0)),
        scratch_shapes=[pltpu.VMEM((D_MODEL, d_e), BF16), pltpu.VMEM((D_MODEL, d_e), BF16),
                        pltpu.VMEM((d_e, D_MODEL), BF16)],
    )
    return pl.pallas_call(
        _expert_kernel,
        out_shape=jax.ShapeDtypeStruct((rows, LANE), F32),
        grid_spec=grid_spec,
        compiler_params=pltpu.CompilerParams(
            dimension_semantics=("arbitrary",), vmem_limit_bytes=VMEM_LIMIT),
        name="moe_experts",
    )(block_e, n_used, xb, wg, wu, wd, bg.reshape(depth, N_EXPERTS, 1, d_e),
      bu.reshape(depth, N_EXPERTS, 1, d_e), bd.reshape(depth, N_EXPERTS, 1, D_MODEL))


def _combine_kernel(dest_hbm, gates_ref, x1_ref, lng_ref, lnb_ref, yb_hbm, y_ref, ybf_ref, idx, buf,
                    sem_idx, sem, *, alpha):
    i = pl.program_id(0)
    n = pl.num_programs(0)
    slot = i % 2
    nxt = 1 - slot

    def idx_copy(tile, s):
        return pltpu.make_async_copy(dest_hbm.at[tile], idx.at[s], sem_idx.at[s])

    def issue_gathers(s):
        def issue(r, carry):
            for kk in range(TOP_K):
                pltpu.make_async_copy(yb_hbm.at[_token_rows(idx[s, r * TOP_K + kk]), :],
                                      buf.at[s, kk, _token_rows(r), :], sem.at[s]).start(priority=kk % 2)
            return carry

        lax.fori_loop(0, TOK_TILE, issue, 0, unroll=8)

    @pl.when(i == 0)
    def _():
        first = idx_copy(0, 0)
        first.start()
        first.wait()
        issue_gathers(0)

        @pl.when(n > 1)
        def _():
            idx_copy(1, 1).start()

    @pl.when(i + 1 < n)
    def _():
        idx_copy(i + 1, nxt).wait()
        issue_gathers(nxt)

    @pl.when(i + 2 < n)
    def _():
        idx_copy(i + 2, slot).start()

    for kk in range(TOP_K):
        pltpu.make_async_copy(yb_hbm.at[pl.ds(0, TOK_TILE * TOK_ROWS), :], buf.at[slot, kk], sem.at[slot]).wait()

    gates = gates_ref[...]
    chunks = []
    for cc in range(TOK_ROWS):
        acc = gates[:, 0:1] * _load_token_chunk(buf.at[slot, 0], cc, TOK_TILE)
        for kk in range(1, TOP_K):
            acc = acc + gates[:, kk:kk + 1] * _load_token_chunk(buf.at[slot, kk], cc, TOK_TILE)
        chunks.append(acc)
    moe = jnp.concatenate(chunks, axis=1)
    y = _layer_norm(alpha * x1_ref[...] + moe, lng_ref[...], lnb_ref[...])
    y_ref[...] = y
    ybf_ref[...] = _bf(y)


def _combine(dest2, route, x1, lng, lnb, yb, alpha):
    nt = x1.shape[0]
    kern = functools.partial(_combine_kernel, alpha=alpha)
    return pl.pallas_call(
        kern,
        out_shape=(jax.ShapeDtypeStruct((nt, D_MODEL), F32),
                   jax.ShapeDtypeStruct((nt, D_MODEL), BF16)),
        grid=(nt // TOK_TILE,),
        in_specs=[pl.BlockSpec(memory_space=pl.ANY),
                  pl.BlockSpec((TOK_TILE, LANE), lambda i: (i, 0)),
                  pl.BlockSpec((TOK_TILE, D_MODEL), lambda i: (i, 0)),
                  pl.BlockSpec((1, D_MODEL), lambda i: (0, 0)),
                  pl.BlockSpec((1, D_MODEL), lambda i: (0, 0)),
                  pl.BlockSpec(memory_space=pl.ANY)],
        out_specs=(pl.BlockSpec((TOK_TILE, D_MODEL), lambda i: (i, 0)),
                   pl.BlockSpec((TOK_TILE, D_MODEL), lambda i: (i, 0))),
        scratch_shapes=[pltpu.SMEM((2, TOK_TILE * TOP_K), jnp.int32),
                        pltpu.VMEM((2, TOP_K, TOK_TILE * TOK_ROWS, LANE), F32),
                        pltpu.SemaphoreType.DMA((2,)), pltpu.SemaphoreType.DMA((2,))],
        compiler_params=pltpu.CompilerParams(
            dimension_semantics=("arbitrary",), vmem_limit_bytes=VMEM_LIMIT),
        name="moe_combine_ln",
    )(dest2, route, x1, lng, lnb, yb)


def _routing_tables(top_i):
    m = top_i.size
    flat_e = top_i.reshape(-1)
    onehot = (flat_e[:, None] == jnp.arange(N_EXPERTS, dtype=jnp.int32)[None, :]).astype(jnp.int32)
    oh3 = onehot.reshape(m // LANE, LANE, N_EXPERTS)
    tri = jnp.tril(jnp.ones((LANE, LANE), F32))
    within = jnp.einsum("ij,tjk->tik", tri, oh3.astype(F32)).astype(jnp.int32)
    tile_tot = within[:, -1, :]
    tile_off = jnp.cumsum(tile_tot, axis=0) - tile_tot
    csum = (within + tile_off[:, None, :]).reshape(m, N_EXPERTS)
    rank = jnp.sum(onehot * csum, axis=1) - 1
    counts = csum[-1]
    padded = (counts + MOE_BLK - 1) // MOE_BLK * MOE_BLK
    pad_ends = jnp.cumsum(padded)
    pad_starts = pad_ends - padded
    dest = jnp.sum(onehot * pad_starts[None, :], axis=1) + rank
    nblk = m // MOE_BLK + N_EXPERTS
    blk_start = jnp.arange(nblk, dtype=jnp.int32) * MOE_BLK
    block_e = jnp.minimum(jnp.sum((blk_start[:, None] >= pad_ends[None, :]).astype(jnp.int32), axis=1),
                          N_EXPERTS - 1)
    n_used = (pad_ends[-1] // MOE_BLK).astype(jnp.int32).reshape(1)
    return dest.astype(jnp.int32), block_e.astype(jnp.int32), n_used, nblk


def kernel(x_prompt, x_sample, state_gdn, state_gdn_conv, state_ssm, state_ssm_conv, w_in, gdn_conv_w,
           gdn_a_log, gdn_dt_bias, gdn_norm_w, ssm_conv_w, ssm_conv_b, ssm_a_log, ssm_dt_bias, ssm_d,
           ssm_norm_w, w_br_gdn, w_br_ssm, w_out, ln1_g, ln1_b, router_w, router_b, exp_w_gate,
           exp_b_gate, exp_w_up, exp_b_up, exp_w_down, exp_b_down, ln2_g, ln2_b):
    batch, seq, _ = x_prompt.shape
    dec = x_sample.shape[0]
    depth = w_in.shape[0]
    n_p = batch * seq
    nt = n_p + dec
    alpha = (2.0 * depth) ** 0.25
    gw = SSM_DINNER // SSM_GROUPS
    tm = _pick(nt, (384, 256, 128, 64, 32, 16))
    assert seq % CHUNK == 0 and dec % CHUNK == 0 and nt % TOK_TILE == 0 and (nt * TOP_K) % MOE_BLK == 0

    x = jnp.concatenate([x_prompt.reshape(n_p, D_MODEL), x_sample.reshape(dec, D_MODEL)], axis=0)
    x_bf = _bf(x)

    o_zg = GDN_CONV_CH
    o_b = o_zg + GDN_VDIM
    o_a = o_b + GDN_HEADS
    o_zs = o_a + GDN_HEADS
    o_x = o_zs + SSM_DINNER
    o_dt = o_x + SSM_CONV_CH
    o_ga = o_dt + SSM_HEADS

    rows = jnp.arange(SMALL_W, dtype=jnp.int32)[:, None]
    lanes = jnp.arange(SSM_DINNER, dtype=jnp.int32)[None, :]
    expand_all = (rows == COL_DT + lanes // SSM_HEADDIM).astype(BF16)

    outs = {k: [] for k in ("gdn_p", "gconv_p", "gconv_s", "ssm_p", "sconv_p", "sconv_s")}
    ssm_all = state_ssm.reshape(depth, dec, SSM_DINNER, SSM_DSTATE)
    gdn_s = ssm_s = None
    xb = None
    for l in range(depth):
        w = w_in[l]
        w_big = _bf(jnp.concatenate([w[:, :o_b], w[:, o_zs:o_dt], w[:, o_ga:]], axis=1))
        w_small = _bf(jnp.concatenate(
            [w[:, o_b:o_zs], w[:, o_dt:o_ga],
             jnp.zeros((D_MODEL, SMALL_W - 2 * GDN_HEADS - SSM_HEADS), F32)], axis=1))
        zpad = jnp.zeros((SMALL_W - COL_DT - SSM_HEADS,), F32)
        pv = jnp.zeros((SUBLANE, SMALL_W), F32)
        pv = pv.at[0].set(jnp.concatenate([jnp.zeros((COL_A,), F32), gdn_dt_bias[l], ssm_dt_bias[l], zpad]))
        pv = pv.at[1].set(jnp.concatenate([jnp.zeros((COL_A,), F32), gdn_a_log[l], ssm_a_log[l], zpad]))
        d_exp = jnp.repeat(ssm_d[l], SSM_HEADDIM).reshape(1, SSM_DINNER)
        gnw = gdn_norm_w[l].reshape(1, LANE)
        snw = ssm_norm_w[l].reshape(1, SSM_DINNER)
        scb = ssm_conv_b[l].reshape(1, SSM_CONV_CH)

        proj = _matmul(x_bf, w_big, tm, W_BIG // 2)
        small = _matmul(x_bf, w_small, tm, SMALL_W)

        proj_s = lax.slice(proj, (n_p, 0), (nt, W_BIG))
        small_s = lax.slice(small, (n_p, 0), (nt, SMALL_W))
        og_s, yz_s, gdn_s, ssm_s = _sample_mix(l, proj_s, small_s, state_gdn_conv[l], state_ssm_conv[l],
                                               state_gdn, ssm_all, gdn_s, ssm_s, gdn_conv_w[l],
                                               ssm_conv_w[l], scb, pv, gnw, expand_all, d_exp, snw)
        og, s_p = _gdn_prompt(proj, small, gdn_conv_w[l], pv, gnw, og_s, batch, seq)
        yz, h_p = _ssd_prompt(proj, small, ssm_conv_w[l], scb, pv, expand_all, d_exp, snw, yz_s, batch, seq)

        rw = jnp.concatenate([router_w[l], jnp.zeros((D_MODEL, LANE - N_EXPERTS), F32)], axis=1)
        rb = jnp.concatenate([router_b[l], jnp.full((LANE - N_EXPERTS,), NEG_BIG, F32)]).reshape(1, LANE)
        x1, x1t, route = _merge(og, yz, proj, x, _bf(w_br_gdn[l]), _bf(w_br_ssm[l]), _bf(w_out[l]),
                                ln1_g[l].reshape(1, D_MODEL), ln1_b[l].reshape(1, D_MODEL), rw, rb, alpha, tm)

        top_i = route[:, TOP_K:2 * TOP_K].astype(jnp.int32)
        dest, block_e, n_used, nblk = _routing_tables(top_i)
        dest2 = dest.reshape(nt // TOK_TILE, TOK_TILE * TOP_K)
        xb = _dispatch(dest2, x1t, jnp.zeros((nblk * MOE_BLK * TOK_ROWS, LANE), F32) if xb is None else xb)
        yb = _experts(l, block_e, n_used, xb, exp_w_gate, exp_w_up, exp_w_down,
                      exp_b_gate, exp_b_up, exp_b_down)
        x, x_bf = _combine(dest2, route, x1, ln2_g[l].reshape(1, D_MODEL), ln2_b[l].reshape(1, D_MODEL),
                           yb, alpha)

        outs["gdn_p"].append(s_p)
        outs["ssm_p"].append(h_p.reshape(batch, SSM_HEADS, SSM_HEADDIM, SSM_DSTATE))
        tails_g = [lax.slice(proj, (b * seq + seq - (CONV_W - 1), 0), (b * seq + seq, GDN_CONV_CH))
                   for b in range(batch)]
        tails_s = [lax.slice(proj, (b * seq + seq - (CONV_W - 1), o_x - 2 * GDN_HEADS),
                             (b * seq + seq, o_x - 2 * GDN_HEADS + SSM_CONV_CH)) for b in range(batch)]
        outs["gconv_p"].append(jnp.stack(tails_g))
        outs["sconv_p"].append(jnp.stack(tails_s))
        outs["gconv_s"].append(jnp.concatenate(
            [state_gdn_conv[l][:, 1:], proj_s[:, None, :GDN_CONV_CH]], axis=1))
        xbc_off = o_x - 2 * GDN_HEADS
        outs["sconv_s"].append(jnp.concatenate(
            [state_ssm_conv[l][:, 1:], proj_s[:, None, xbc_off:xbc_off + SSM_CONV_CH]], axis=1))

    yp = x[:n_p].reshape(batch, seq, D_MODEL)
    ys = x[n_p:].reshape(dec, 1, D_MODEL)
    return (yp, ys, jnp.stack(outs["gdn_p"]), gdn_s, jnp.stack(outs["gconv_p"]),
            jnp.stack(outs["gconv_s"]), jnp.stack(outs["ssm_p"]),
            ssm_s.reshape(depth, dec, SSM_HEADS, SSM_HEADDIM, SSM_DSTATE),
            jnp.stack(outs["sconv_p"]), jnp.stack(outs["sconv_s"]))
```

```python
import functools

import jax
import jax.numpy as jnp
from jax import lax
from jax.experimental import pallas as pl
from jax.experimental.pallas import tpu as pltpu

F32 = jnp.float32
BF16 = jnp.bfloat16

D_MODEL = 1024
GDN_HEADS = 8
GDN_DK = 128
GDN_VDIM = 1024
GDN_CONV_CH = 3072
SSM_HEADS = 32
SSM_HEADDIM = 64
SSM_GROUPS = 4
SSM_DINNER = 2048
SSM_DSTATE = 128
SSM_CONV_CH = 3072
CONV_W = 4
N_EXPERTS = 32
TOP_K = 4
SWIGLU_ALPHA = 1.702
SWIGLU_LIMIT = 7.0
LN_EPS = 1e-5
RMS_EPS = 1e-6
L2_EPS = 1e-6
NEG_BIG = -1e30

W_BIG = 11264
SMALL_W = 128
COL_B, COL_A, COL_DT = 0, 8, 16

LANE = 128
SUBLANE = 8
CHUNK = 128
GDN_HB = 8
MOE_BLK = 512
TOK_TILE = 128
TOK_ROWS = D_MODEL // LANE
SAMPLE_SEQS = 2
VMEM_LIMIT = 56 * 1024 * 1024


def _pick(n, cands):
    for c in cands:
        if n % c == 0:
            return c
    raise ValueError(f"no tile for {n}")


def _bf(x):
    return x.astype(BF16)


def _dot(a, b, prec=None):
    return jnp.dot(a, b, preferred_element_type=F32, precision=prec)


def _dot_nt(a, b):
    return lax.dot_general(a, b, (((1,), (1,)), ((), ())), preferred_element_type=F32)


def _dot_tn(a, b):
    return lax.dot_general(a, b, (((0,), (0,)), ((), ())), preferred_element_type=F32)


def _split3(x):
    hi = _bf(x)
    r = x - hi.astype(F32)
    mid = _bf(r)
    return hi, mid, _bf(r - mid.astype(F32))


def _dot_sel_rhs(x, sel):
    hi, mid, lo = _split3(x)
    return (_dot(lo, sel) + _dot(mid, sel)) + _dot(hi, sel)


def _dot_sel_lhs(sel, x):
    hi, mid, lo = _split3(x)
    return (_dot(sel, lo) + _dot(sel, mid)) + _dot(sel, hi)


def _sigmoid(x):
    return jax.nn.sigmoid(x)


def _silu(x):
    return x * jax.nn.sigmoid(x)


def _softplus(x):
    return jnp.maximum(x, 0.0) + jnp.log(1.0 + jnp.exp(-jnp.abs(x)))


def _layer_norm(x, g, b):
    mu = jnp.mean(x, axis=-1, keepdims=True)
    xc = x - mu
    var = jnp.mean(xc * xc, axis=-1, keepdims=True)
    return xc * lax.rsqrt(var + LN_EPS) * g + b


def _small_act(raw, pv):
    col = lax.broadcasted_iota(jnp.int32, raw.shape, 1)
    sp = _softplus(raw + pv[0:1, :])
    act = jnp.where(col < COL_A, _sigmoid(raw), sp)
    gda = sp * (-jnp.exp(pv[1:2, :]))
    return act, gda


def _mm_kernel(x_ref, w_ref, o_ref):
    o_ref[...] = _dot(x_ref[...], w_ref[...])


def _matmul(x, w, tm, tn):
    m, k = x.shape
    n = w.shape[1]
    return pl.pallas_call(
        _mm_kernel,
        out_shape=jax.ShapeDtypeStruct((m, n), F32),
        grid=(n // tn, m // tm),
        in_specs=[pl.BlockSpec((tm, k), lambda j, i: (i, 0)),
                  pl.BlockSpec((k, tn), lambda j, i: (0, j))],
        out_specs=pl.BlockSpec((tm, tn), lambda j, i: (i, j)),
        compiler_params=pltpu.CompilerParams(
            dimension_semantics=("parallel", "parallel"), vmem_limit_bytes=VMEM_LIMIT),
        name="in_proj",
    )(x, w)


def _tri_inv_all(mats, ii, jj, c):
    eye = (ii == jj).astype(F32)
    pair = (ii >> 1) == (jj >> 1)
    ts = [eye - jnp.where(pair, a, 0.0) for a in mats]
    abs_ = [_bf(a) for a in mats]
    s = 1
    while (2 << s) <= c:
        same_outer = (ii >> (s + 1)) == (jj >> (s + 1))
        same_inner = (ii >> s) == (jj >> s)
        off = _bf((same_outer & jnp.logical_not(same_inner)).astype(F32))
        tbs = [_bf(t) for t in ts]
        tes = [_dot(tb, ab * off) for tb, ab in zip(tbs, abs_)]
        ts = [t - _dot(_bf(te), tb) for t, te, tb in zip(ts, tes, tbs)]
        s += 1
    return ts


def _with_sample_rows(chunk_fn, n_in):
    def kern(*refs):
        b = pl.program_id(0)
        nb = pl.num_programs(0) - 1
        tail_ref = refs[n_in]
        out_ref = refs[n_in + 1]

        @pl.when(b < nb)
        def _():
            chunk_fn(*refs[:n_in], *refs[n_in + 1:])

        @pl.when(b == nb)
        def _():
            out_ref[...] = tail_ref[...]

    return kern


def _gdn_chunk(q_ref, k_ref, v_ref, zg_ref, small_ref, cwq_ref, cwk_ref, cwv_ref, pv_ref, nw_ref,
               o_ref, sfin_ref, xf, tail, s_scr):
    assert GDN_HB == GDN_HEADS
    c_len = q_ref.shape[0]
    c = pl.program_id(2)
    nc = pl.num_programs(2)

    @pl.when(c == 0)
    def _():
        tail[...] = jnp.zeros_like(tail)
        s_scr[...] = jnp.zeros_like(s_scr)

    for p, r in enumerate((q_ref, k_ref, v_ref)):
        xf[p, 0:SUBLANE, :] = tail[p]
        xf[p, SUBLANE:SUBLANE + c_len, :] = r[...]
        tail[p] = r[c_len - SUBLANE:c_len, :]
    cws = (cwq_ref, cwk_ref, cwv_ref)

    def conv(p, hs):
        acc = None
        for i in range(CONV_W):
            term = cws[p][i:i + 1, hs] * xf[p, pl.ds(SUBLANE - (CONV_W - 1) + i, c_len), hs]
            acc = term if acc is None else acc + term
        return _silu(acc)

    act, gda = _small_act(small_ref[...], pv_ref[...])
    ii = lax.broadcasted_iota(jnp.int32, (c_len, c_len), 0)
    jj = lax.broadcasted_iota(jnp.int32, (c_len, c_len), 1)
    incl = ii >= jj
    strict = ii > jj
    gcum = _dot_sel_lhs(_bf(incl.astype(F32)), gda)
    gcum_t = gcum.T
    nw = nw_ref[...]

    heads = range(GDN_HB)
    hsl = [slice(hh * LANE, (hh + 1) * LANE) for hh in heads]
    betas = [act[:, COL_B + hh:COL_B + hh + 1] for hh in heads]
    gcs = [gcum[:, COL_A + hh:COL_A + hh + 1] for hh in heads]
    grs = [gcum_t[COL_A + hh:COL_A + hh + 1, :] for hh in heads]
    ks = []
    for hh in heads:
        k = conv(1, hsl[hh])
        ks.append(k * lax.rsqrt(jnp.sum(k * k, axis=-1, keepdims=True) + L2_EPS))
    kbs = [ks[hh] * betas[hh] for hh in heads]
    kbfs = [_bf(k) for k in ks]
    kks = [_dot_nt(_bf(kbs[hh]), kbfs[hh]) for hh in heads]
    gams = [jnp.exp(jnp.where(incl, gcs[hh] - grs[hh], NEG_BIG)) for hh in heads]
    amats = [jnp.where(strict, kks[hh] * gams[hh], 0.0) for hh in heads]
    qs = []
    for hh in heads:
        q = conv(0, hsl[hh])
        qs.append(q * lax.rsqrt(jnp.sum(q * q, axis=-1, keepdims=True) + L2_EPS) * (GDN_DK ** -0.5))
    qks = [_dot_nt(_bf(qs[hh]), kbfs[hh]) * gams[hh] for hh in heads]
    egs = [jnp.exp(gcs[hh]) for hh in heads]
    rhs = [_bf(jnp.concatenate([conv(2, hsl[hh]) * betas[hh], kbs[hh] * egs[hh]], axis=1)) for hh in heads]
    ts = _tri_inv_all(amats, ii, jj, c_len)
    uws = [_dot(_bf(ts[hh]), rhs[hh]) for hh in heads]
    glasts = [gcs[hh][c_len - 1:c_len, :] for hh in heads]
    s_olds = [s_scr[hh] for hh in heads]
    sbs = [_bf(s) for s in s_olds]
    v_news = [uws[hh][:, :LANE] - _dot(_bf(uws[hh][:, LANE:]), sbs[hh]) for hh in heads]
    vnbs = [_bf(v) for v in v_news]
    os_ = [_dot(_bf(qs[hh] * egs[hh]), sbs[hh]) + _dot(_bf(qks[hh]), vnbs[hh]) for hh in heads]
    for hh in heads:
        kdec = ks[hh] * jnp.exp(glasts[hh] - gcs[hh])
        s_scr[hh] = s_olds[hh] * jnp.exp(glasts[hh]) + _dot_tn(_bf(kdec), vnbs[hh])
    for hh in heads:
        o = os_[hh]
        o = (o * lax.rsqrt(jnp.mean(o * o, axis=-1, keepdims=True) + RMS_EPS) * nw
             * _silu(zg_ref[:, hsl[hh]]))
        o_ref[:, hsl[hh]] = _bf(o)

    @pl.when(c == nc - 1)
    def _():
        sfin_ref[0] = s_scr[...]


def _sample_tail_maps(batch, nc, n_tail):
    def seq(b):
        return jnp.minimum(b, batch - 1)

    def tail(c):
        return jnp.minimum(c, n_tail - 1)

    def in_row(b, c):
        return seq(b) * nc + c

    def out_row(b, c):
        return jnp.where(b < batch, b * nc + c, batch * nc + tail(c))

    return seq, tail, in_row, out_row


def _gdn_prompt(proj, small, conv_w, pv, norm_w, og_s, batch, seq):
    nc = seq // CHUNK
    n_tail = og_s.shape[0] // CHUNK
    hbw = GDN_HB * LANE
    ngrp = GDN_HEADS // GDN_HB
    kq, kk, kv, kz = 0, GDN_VDIM // hbw, 2 * GDN_VDIM // hbw, 3 * GDN_VDIM // hbw
    seq_of, tail_of, in_row, out_row = _sample_tail_maps(batch, nc, n_tail)

    def pspec(off):
        return pl.BlockSpec((CHUNK, hbw), lambda b, hg, c: (in_row(b, c), off + hg))

    def wspec(off):
        return pl.BlockSpec((CONV_W, hbw), lambda b, hg, c: (0, off + hg))

    return pl.pallas_call(
        _with_sample_rows(_gdn_chunk, 10),
        out_shape=(jax.ShapeDtypeStruct((batch * seq + og_s.shape[0], GDN_VDIM), BF16),
                   jax.ShapeDtypeStruct((batch, GDN_HEADS, GDN_DK, LANE), F32)),
        grid=(batch + 1, ngrp, nc),
        in_specs=[pspec(kq), pspec(kk), pspec(kv), pspec(kz),
                  pl.BlockSpec((CHUNK, SMALL_W), lambda b, hg, c: (in_row(b, c), 0)),
                  wspec(kq), wspec(kk), wspec(kv),
                  pl.BlockSpec((SUBLANE, SMALL_W), lambda b, hg, c: (0, 0)),
                  pl.BlockSpec((1, LANE), lambda b, hg, c: (0, 0)),
                  pl.BlockSpec((CHUNK, hbw), lambda b, hg, c: (tail_of(c), hg))],
        out_specs=(pl.BlockSpec((CHUNK, hbw), lambda b, hg, c: (out_row(b, c), hg)),
                   pl.BlockSpec((1, GDN_HB, GDN_DK, LANE), lambda b, hg, c: (seq_of(b), hg, 0, 0))),
        scratch_shapes=[pltpu.VMEM((3, CHUNK + SUBLANE, hbw), F32),
                        pltpu.VMEM((3, SUBLANE, hbw), F32),
                        pltpu.VMEM((GDN_HB, GDN_DK, LANE), F32)],
        compiler_params=pltpu.CompilerParams(
            dimension_semantics=("arbitrary", "arbitrary", "arbitrary"), vmem_limit_bytes=VMEM_LIMIT),
        name="gdn_prompt",
    )(proj, proj, proj, proj, small, conv_w, conv_w, conv_w, pv, norm_w, og_s)


def _ssd_chunk(xs_ref, b_ref, c_ref, zs_ref, small_ref, cwx_ref, cwb_ref, cwc_ref, cbx_ref, cbb_ref,
               cbc_ref, pv_ref, ex_ref, dexp_ref, nw_ref, yz_ref, hfin_ref, xf, tail, ht, ydiag):
    c_len = xs_ref.shape[0]
    gw = SSM_DINNER // SSM_GROUPS
    hpg = gw // SSM_HEADDIM
    gn = SSM_GROUPS * SSM_DSTATE
    c = pl.program_id(1)
    nc = pl.num_programs(1)

    @pl.when(c == 0)
    def _():
        tail[...] = jnp.zeros_like(tail)
        ht[...] = jnp.zeros_like(ht)

    parts = ((xs_ref, cwx_ref, cbx_ref, 0, SSM_DINNER), (b_ref, cwb_ref, cbb_ref, SSM_DINNER, gn),
             (c_ref, cwc_ref, cbc_ref, SSM_DINNER + gn, gn))
    convs = []
    for r, cw, cb, off, wd in parts:
        sl = slice(off, off + wd)
        xf[0:SUBLANE, sl] = tail[:, sl]
        xf[SUBLANE:SUBLANE + c_len, sl] = r[...]
        tail[:, sl] = r[c_len - SUBLANE:c_len, :]
        acc = cb[...]
        for i in range(CONV_W):
            acc = acc + cw[i:i + 1, :] * xf[pl.ds(SUBLANE - (CONV_W - 1) + i, c_len), sl]
        convs.append(_silu(acc))
    xs, bm_all, cm_all = convs

    act, gda = _small_act(small_ref[...], pv_ref[...])
    ii = lax.broadcasted_iota(jnp.int32, (c_len, c_len), 0)
    jj = lax.broadcasted_iota(jnp.int32, (c_len, c_len), 1)
    incl = ii >= jj
    acs = _dot_sel_lhs(_bf(incl.astype(F32)), gda)
    acs_t = acs.T
    ex = ex_ref[...]
    dt_x = _dot_sel_rhs(act, ex)
    acs_x = _dot_sel_rhs(acs, ex)
    last = acs_x[c_len - 1:c_len, :]
    xdt = xs * dt_x
    xdec = _bf(xdt * jnp.exp(last - acs_x))
    lane = lax.broadcasted_iota(jnp.int32, (c_len, LANE), 1)
    lo_half = lane < SSM_HEADDIM

    groups = range(SSM_GROUPS)
    bms = [_bf(bm_all[:, g * SSM_DSTATE:(g + 1) * SSM_DSTATE]) for g in groups]
    cms = [_bf(cm_all[:, g * SSM_DSTATE:(g + 1) * SSM_DSTATE]) for g in groups]
    cbs = [_dot_nt(cms[g], bms[g]) for g in groups]
    h_olds = [ht[:, g * gw:(g + 1) * gw] for g in groups]
    y_offs = [_dot(cms[g], _bf(h_olds[g])) for g in groups]
    for g in groups:
        for pr in range(hpg // 2):
            ps = slice(g * gw + pr * LANE, g * gw + (pr + 1) * LANE)
            xpair = xdt[:, ps]
            acc = None
            for half in range(2):
                head = g * hpg + pr * 2 + half
                ac = acs[:, COL_DT + head:COL_DT + head + 1]
                ar = acs_t[COL_DT + head:COL_DT + head + 1, :]
                sc = cbs[g] * jnp.exp(jnp.where(incl, ac - ar, NEG_BIG))
                keep = lo_half if half == 0 else jnp.logical_not(lo_half)
                term = _dot(_bf(sc), _bf(jnp.where(keep, xpair, 0.0)))
                acc = term if acc is None else acc + term
            ydiag[:, ps] = acc
    for g in groups:
        gs = slice(g * gw, (g + 1) * gw)
        ht[:, gs] = h_olds[g] * jnp.exp(last[:, gs]) + _dot_tn(bms[g], xdec[:, gs])
    y = ydiag[...] + jnp.concatenate(y_offs, axis=1) * jnp.exp(acs_x) + dexp_ref[...] * xs
    yz = y * _silu(zs_ref[...])
    nw = nw_ref[...]
    for g in groups:
        gs = slice(g * gw, (g + 1) * gw)
        yg = yz[:, gs]
        yz_ref[:, gs] = _bf(yg * lax.rsqrt(jnp.mean(yg * yg, axis=-1, keepdims=True) + RMS_EPS) * nw[:, gs])

    @pl.when(c == nc - 1)
    def _():
        hfin_ref[0] = ht[...].T


def _ssd_prompt(proj, small, conv_w, conv_b, pv, expand, d_exp, norm_w, yz_s, batch, seq):
    nc = seq // CHUNK
    gn = SSM_GROUPS * SSM_DSTATE
    x_off = (GDN_CONV_CH + GDN_VDIM + SSM_DINNER)
    z_off = GDN_CONV_CH + GDN_VDIM
    kx = x_off // SSM_DINNER
    kb = (x_off + SSM_DINNER) // gn
    kz = z_off // SSM_DINNER
    wb = SSM_DINNER // gn

    n_tail = yz_s.shape[0] // CHUNK
    seq_of, tail_of, row, out_row = _sample_tail_maps(batch, nc, n_tail)

    in_specs = [
        pl.BlockSpec((CHUNK, SSM_DINNER), lambda b, c: (row(b, c), kx)),
        pl.BlockSpec((CHUNK, gn), lambda b, c: (row(b, c), kb)),
        pl.BlockSpec((CHUNK, gn), lambda b, c: (row(b, c), kb + 1)),
        pl.BlockSpec((CHUNK, SSM_DINNER), lambda b, c: (row(b, c), kz)),
        pl.BlockSpec((CHUNK, SMALL_W), lambda b, c: (row(b, c), 0)),
        pl.BlockSpec((CONV_W, SSM_DINNER), lambda b, c: (0, 0)),
        pl.BlockSpec((CONV_W, gn), lambda b, c: (0, wb)),
        pl.BlockSpec((CONV_W, gn), lambda b, c: (0, wb + 1)),
        pl.BlockSpec((1, SSM_DINNER), lambda b, c: (0, 0)),
        pl.BlockSpec((1, gn), lambda b, c: (0, wb)),
        pl.BlockSpec((1, gn), lambda b, c: (0, wb + 1)),
        pl.BlockSpec((SUBLANE, SMALL_W), lambda b, c: (0, 0)),
        pl.BlockSpec((SMALL_W, SSM_DINNER), lambda b, c: (0, 0)),
        pl.BlockSpec((1, SSM_DINNER), lambda b, c: (0, 0)),
        pl.BlockSpec((1, SSM_DINNER), lambda b, c: (0, 0)),
        pl.BlockSpec((CHUNK, SSM_DINNER), lambda b, c: (tail_of(c), 0)),
    ]
    return pl.pallas_call(
        _with_sample_rows(_ssd_chunk, 15),
        out_shape=(jax.ShapeDtypeStruct((batch * seq + yz_s.shape[0], SSM_DINNER), BF16),
                   jax.ShapeDtypeStruct((batch, SSM_DINNER, SSM_DSTATE), F32)),
        grid=(batch + 1, nc),
        in_specs=in_specs,
        out_specs=(pl.BlockSpec((CHUNK, SSM_DINNER), lambda b, c: (out_row(b, c), 0)),
                   pl.BlockSpec((1, SSM_DINNER, SSM_DSTATE), lambda b, c: (seq_of(b), 0, 0))),
        scratch_shapes=[pltpu.VMEM((CHUNK + SUBLANE, SSM_CONV_CH), F32),
                        pltpu.VMEM((SUBLANE, SSM_CONV_CH), F32),
                        pltpu.VMEM((SSM_DSTATE, SSM_DINNER), F32),
                        pltpu.VMEM((CHUNK, SSM_DINNER), F32)],
        compiler_params=pltpu.CompilerParams(
            dimension_semantics=("arbitrary", "arbitrary"), vmem_limit_bytes=VMEM_LIMIT),
        name="ssd_prompt",
    )(proj, proj, proj, proj, small, conv_w, conv_w, conv_w, conv_b, conv_b, conv_b, pv, expand,
      d_exp, norm_w, yz_s)


def _sample_prep_kernel(small_ref, pv_ref, ex_ref, act_ref, gda_ref, dtx_ref, dax_ref):
    act, gda = _small_act(small_ref[...], pv_ref[...])
    act_ref[...] = act
    gda_ref[...] = gda
    ex = ex_ref[...]
    dtx_ref[...] = _dot_sel_rhs(act, ex)
    dax_ref[...] = _dot_sel_rhs(gda, ex)


def _sample_prep(small_s, pv, expand_all):
    n = small_s.shape[0]
    return pl.pallas_call(
        _sample_prep_kernel,
        out_shape=(jax.ShapeDtypeStruct((n, SMALL_W), F32), jax.ShapeDtypeStruct((n, SMALL_W), F32),
                   jax.ShapeDtypeStruct((n, SSM_DINNER), F32), jax.ShapeDtypeStruct((n, SSM_DINNER), F32)),
        name="sample_prep",
    )(small_s, pv, expand_all)


def _sample_kernel(proj_ref, act_ref, gda_ref, dtx_ref, dax_ref, gcs_ref, scs_ref, s0_ref, h0_ref, gcw_ref,
                   scw_ref, scb_ref, gnw_ref, dexp_ref, snw_ref, *rest, n_alias):
    og_ref, yz_ref, s_ref, h_ref, stk, stk2 = rest[n_alias:]
    if n_alias == 0:
        if s_ref.shape[0] > 1:
            s_ref[1:] = jnp.zeros((s_ref.shape[0] - 1,) + s_ref.shape[1:], F32)
            h_ref[1:] = jnp.zeros((h_ref.shape[0] - 1,) + h_ref.shape[1:], F32)
        s_ref = s_ref.at[0]
        h_ref = h_ref.at[0]
    for q in range(SAMPLE_SEQS):
        _sample_one(proj_ref.at[q], act_ref.at[q], gda_ref.at[q], dtx_ref.at[q], dax_ref.at[q], gcs_ref.at[q],
                    scs_ref.at[q], s0_ref.at[q], h0_ref.at[q], gcw_ref, scw_ref, scb_ref, gnw_ref, dexp_ref,
                    snw_ref, og_ref.at[q], yz_ref.at[q], s_ref.at[q], h_ref.at[q], stk.at[q], stk2.at[q])


def _sample_one(proj_ref, act_ref, gda_ref, dtx_ref, dax_ref, gcs_ref, scs_ref, s0_ref, h0_ref, gcw_ref,
                scw_ref, scb_ref, gnw_ref, dexp_ref, snw_ref, og_ref, yz_ref, s_ref, h_ref, stk, stk2):
    gw = SSM_DINNER // SSM_GROUPS
    z_off = GDN_CONV_CH + GDN_VDIM
    x_off = z_off + SSM_DINNER

    def conv1(state_ref, w_ref, new_row):
        acc = w_ref[CONV_W - 1:CONV_W, :] * new_row
        for i in range(CONV_W - 1):
            acc = acc + w_ref[i:i + 1, :] * state_ref[i:i + 1, :]
        return acc

    act = act_ref[...]
    gda = gda_ref[...]

    qkv = _silu(conv1(gcs_ref, gcw_ref, proj_ref[:, 0:GDN_CONV_CH]))
    stk[...] = jnp.zeros_like(stk)
    qs, vs = [], []
    for h in range(GDN_HEADS):
        q = qkv[:, h * LANE:(h + 1) * LANE]
        k = qkv[:, GDN_VDIM + h * LANE:GDN_VDIM + (h + 1) * LANE]
        q = q * lax.rsqrt(jnp.sum(q * q, axis=-1, keepdims=True) + L2_EPS) * (GDN_DK ** -0.5)
        k = k * lax.rsqrt(jnp.sum(k * k, axis=-1, keepdims=True) + L2_EPS)
        stk[h:h + 1, :] = k
        stk[GDN_HEADS + h:GDN_HEADS + h + 1, :] = q
        vs.append(qkv[:, 2 * GDN_VDIM + h * LANE:2 * GDN_VDIM + (h + 1) * LANE])
    cols = stk[...].T
    gnw = gnw_ref[...]
    for h in range(GDN_HEADS):
        kc = cols[:, h:h + 1]
        qc = cols[:, GDN_HEADS + h:GDN_HEADS + h + 1]
        beta = act[:, COL_B + h:COL_B + h + 1]
        gh = gda[:, COL_A + h:COL_A + h + 1]
        sd = s0_ref[h] * jnp.exp(gh)
        v_old = jnp.sum(sd * kc, axis=0, keepdims=True)
        delta = (vs[h] - v_old) * beta
        s_new = sd + kc * delta
        s_ref[h] = s_new
        o = jnp.sum(s_new * qc, axis=0, keepdims=True)
        zg = proj_ref[:, GDN_CONV_CH + h * LANE:GDN_CONV_CH + (h + 1) * LANE]
        o = o * lax.rsqrt(jnp.mean(o * o, axis=-1, keepdims=True) + RMS_EPS) * gnw * _silu(zg)
        og_ref[:, h * LANE:(h + 1) * LANE] = _bf(o)

    xbc = _silu(conv1(scs_ref, scw_ref, proj_ref[:, x_off:x_off + SSM_CONV_CH]) + scb_ref[...])
    dexp = dexp_ref[...]
    stk2[...] = jnp.zeros_like(stk2)
    for g in range(SSM_GROUPS):
        gs = slice(g * gw, (g + 1) * gw)
        stk2[2 * g:2 * g + 1, :] = jnp.exp(dax_ref[:, gs])
        stk2[2 * g + 1:2 * g + 2, :] = xbc[:, gs] * dtx_ref[:, gs]
    cols2 = stk2[...].T
    for g in range(SSM_GROUPS):
        gs = slice(g * gw, (g + 1) * gw)
        xs = xbc[:, gs]
        bm = xbc[:, SSM_DINNER + g * SSM_DSTATE:SSM_DINNER + (g + 1) * SSM_DSTATE]
        cm = xbc[:, SSM_DINNER + SSM_GROUPS * SSM_DSTATE + g * SSM_DSTATE:
                 SSM_DINNER + SSM_GROUPS * SSM_DSTATE + (g + 1) * SSM_DSTATE]
        h_new = h0_ref[gs, :] * cols2[:, 2 * g:2 * g + 1] + cols2[:, 2 * g + 1:2 * g + 2] * bm
        h_ref[gs, :] = h_new
        cm16 = jnp.broadcast_to(cm, (2 * SUBLANE, SSM_DSTATE))
        y = _dot_nt(_bf(cm16), _bf(h_new))[0:1, :] + dexp[:, gs] * xs
        yz = y * _silu(proj_ref[:, z_off + g * gw:z_off + (g + 1) * gw])
        yz = yz * lax.rsqrt(jnp.mean(yz * yz, axis=-1, keepdims=True) + RMS_EPS) * snw_ref[:, gs]
        yz_ref[:, gs] = _bf(yz)


def _sample_mix(layer, proj_s, small_s, gconv_state, sconv_state, s_all, h_all, s_prev, h_prev, gconv_w,
                sconv_w, sconv_b, pv, gnorm_w, expand_all, d_exp, snorm_w):
    n = proj_s.shape[0]
    gw = SSM_DINNER // SSM_GROUPS
    proj3 = proj_s.reshape(n, 1, W_BIG)
    act, gda, dtx, dax = _sample_prep(small_s, pv, expand_all)

    def full(shape):
        nd = len(shape)
        return pl.BlockSpec(shape, lambda i: (0,) * nd)

    assert n % SAMPLE_SEQS == 0

    def per_seq(shape):
        nd = len(shape)
        return pl.BlockSpec((SAMPLE_SEQS,) + shape, lambda i: (i,) + (0,) * nd)

    def per_layer_seq(shape):
        nd = len(shape)
        return pl.BlockSpec((None, SAMPLE_SEQS) + shape, lambda i: (layer, i) + (0,) * nd)

    in_specs = [per_seq((1, W_BIG)), per_seq((1, SMALL_W)), per_seq((1, SMALL_W)),
                per_seq((1, SSM_DINNER)), per_seq((1, SSM_DINNER)),
                per_seq((CONV_W - 1, GDN_CONV_CH)), per_seq((CONV_W - 1, SSM_CONV_CH)),
                per_layer_seq((GDN_HEADS, GDN_DK, LANE)), per_layer_seq((SSM_DINNER, SSM_DSTATE)),
                full((CONV_W, GDN_CONV_CH)), full((CONV_W, SSM_CONV_CH)), full((1, SSM_CONV_CH)),
                full((1, LANE)), full((1, SSM_DINNER)), full((1, SSM_DINNER))]
    args = [proj3, act.reshape(n, 1, SMALL_W), gda.reshape(n, 1, SMALL_W), dtx.reshape(n, 1, SSM_DINNER),
            dax.reshape(n, 1, SSM_DINNER), gconv_state, sconv_state, s_all, h_all, gconv_w, sconv_w, sconv_b,
            gnorm_w, d_exp, snorm_w]
    aliases = {}
    if s_prev is not None:
        aliases = {len(args): 2, len(args) + 1: 3}
        in_specs += [pl.BlockSpec(memory_space=pl.ANY), pl.BlockSpec(memory_space=pl.ANY)]
        args += [s_prev, h_prev]
    if s_prev is None:
        depth = s_all.shape[0]
        state_specs = [pl.BlockSpec((depth, SAMPLE_SEQS, GDN_HEADS, GDN_DK, LANE), lambda i: (0, i, 0, 0, 0)),
                       pl.BlockSpec((depth, SAMPLE_SEQS, SSM_DINNER, SSM_DSTATE), lambda i: (0, i, 0, 0))]
    else:
        state_specs = [per_layer_seq((GDN_HEADS, GDN_DK, LANE)), per_layer_seq((SSM_DINNER, SSM_DSTATE))]
    kern = functools.partial(_sample_kernel, n_alias=len(aliases))
    og, yz, s_new, h_new = pl.pallas_call(
        kern,
        out_shape=(jax.ShapeDtypeStruct((n, 1, GDN_VDIM), BF16),
                   jax.ShapeDtypeStruct((n, 1, SSM_DINNER), BF16),
                   jax.ShapeDtypeStruct(s_all.shape, F32),
                   jax.ShapeDtypeStruct(h_all.shape, F32)),
        grid=(n // SAMPLE_SEQS,),
        in_specs=in_specs,
        out_specs=(per_seq((1, GDN_VDIM)), per_seq((1, SSM_DINNER)), state_specs[0], state_specs[1]),
        scratch_shapes=[pltpu.VMEM((SAMPLE_SEQS, LANE, LANE), F32), pltpu.VMEM((SAMPLE_SEQS, LANE, gw), F32)],
        input_output_aliases=aliases,
        compiler_params=pltpu.CompilerParams(
            dimension_semantics=("parallel",), vmem_limit_bytes=VMEM_LIMIT),
        name="sample_mix",
    )(*args)
    return og.reshape(n, GDN_VDIM), yz.reshape(n, SSM_DINNER), s_new, h_new


def _merge_kernel(og_ref, yz_ref, ga_ref, gb_ref, x_ref, wbg_ref, wbs_ref, wout_ref, lng_ref, lnb_ref,
                  rw_ref, rb_ref, x1_ref, x1t_ref, route_ref, *, alpha):
    a = _dot(og_ref[...], wbg_ref[...])
    b = _dot(yz_ref[...], wbs_ref[...])
    merged = _sigmoid(ga_ref[...]) * a + _sigmoid(gb_ref[...]) * b
    mix = _dot(_bf(merged), wout_ref[...])
    x1 = _layer_norm(alpha * x_ref[...] + mix, lng_ref[...], lnb_ref[...])
    x1_ref[...] = x1
    _store_token_tiles(x1t_ref, x1)

    lg = _dot(_bf(x1), rw_ref[...]) + rb_ref[...]
    colf = lax.broadcasted_iota(jnp.int32, lg.shape, 1).astype(F32)
    vals, idxs = [], []
    for _ in range(TOP_K):
        m = jnp.max(lg, axis=-1, keepdims=True)
        idx = jnp.min(jnp.where(lg == m, colf, float(LANE)), axis=-1, keepdims=True)
        vals.append(m)
        idxs.append(idx)
        lg = jnp.where(colf == idx, 2.0 * NEG_BIG, lg)
    es = [jnp.exp(v - vals[0]) for v in vals]
    den = es[0] + es[1] + es[2] + es[3]
    route = jnp.zeros_like(lg)
    for kk in range(TOP_K):
        route = jnp.where(colf == float(kk), es[kk] / den, route)
        route = jnp.where(colf == float(TOP_K + kk), idxs[kk], route)
    route_ref[...] = route


def _merge(og, yz, proj, x, wbg, wbs, wout, lng, lnb, rw, rb, alpha, tm):
    nt = x.shape[0]
    ka = (W_BIG - 2 * D_MODEL) // D_MODEL
    kern = functools.partial(_merge_kernel, alpha=alpha)

    def full(shape):
        return pl.BlockSpec(shape, lambda i: (0, 0))

    return pl.pallas_call(
        kern,
        out_shape=(jax.ShapeDtypeStruct((nt, D_MODEL), F32),
                   jax.ShapeDtypeStruct((nt * TOK_ROWS, LANE), F32),
                   jax.ShapeDtypeStruct((nt, LANE), F32)),
        grid=(nt // tm,),
        in_specs=[pl.BlockSpec((tm, GDN_VDIM), lambda i: (i, 0)),
                  pl.BlockSpec((tm, SSM_DINNER), lambda i: (i, 0)),
                  pl.BlockSpec((tm, D_MODEL), lambda i: (i, ka)),
                  pl.BlockSpec((tm, D_MODEL), lambda i: (i, ka + 1)),
                  pl.BlockSpec((tm, D_MODEL), lambda i: (i, 0)),
                  full((GDN_VDIM, D_MODEL)), full((SSM_DINNER, D_MODEL)), full((D_MODEL, D_MODEL)),
                  full((1, D_MODEL)), full((1, D_MODEL)), full((D_MODEL, LANE)), full((1, LANE))],
        out_specs=(pl.BlockSpec((tm, D_MODEL), lambda i: (i, 0)),
                   pl.BlockSpec((tm * TOK_ROWS, LANE), lambda i: (i, 0)),
                   pl.BlockSpec((tm, LANE), lambda i: (i, 0))),
        compiler_params=pltpu.CompilerParams(
            dimension_semantics=("parallel",), vmem_limit_bytes=VMEM_LIMIT),
        name="merge_ln_router",
    )(og, yz, proj, proj, x, wbg, wbs, wout, lng, lnb, _bf(rw), rb)


def _store_token_tiles(ref, val):
    n = val.shape[0]
    for cc in range(TOK_ROWS):
        ref[pl.ds(cc, n, stride=TOK_ROWS), :] = val[:, cc * LANE:(cc + 1) * LANE]


def _load_token_chunk(ref, cc, n):
    return ref[pl.ds(cc, n, stride=TOK_ROWS), :]


def _token_rows(t):
    return pl.ds(pl.multiple_of(t * TOK_ROWS, TOK_ROWS), TOK_ROWS)


def _dispatch_kernel(dest_hbm, x_ref, xb_in, xb_out, idx, sem_idx, sem):
    del xb_in
    i = pl.program_id(0)
    cp = pltpu.make_async_copy(dest_hbm.at[i], idx, sem_idx)
    cp.start()
    cp.wait()

    def issue(r, carry):
        for kk in range(TOP_K):
            pltpu.make_async_copy(x_ref.at[_token_rows(r), :],
                                  xb_out.at[_token_rows(idx[r * TOP_K + kk]), :], sem).start(priority=kk % 2)
        return carry

    lax.fori_loop(0, TOK_TILE, issue, 0, unroll=8)
    for _ in range(TOP_K):
        pltpu.make_async_copy(x_ref, xb_out.at[pl.ds(0, TOK_TILE * TOK_ROWS), :], sem).wait()


def _dispatch(dest2, x1t, xb_zero):
    n_tiles = dest2.shape[0]
    return pl.pallas_call(
        _dispatch_kernel,
        out_shape=jax.ShapeDtypeStruct(xb_zero.shape, F32),
        grid=(n_tiles,),
        in_specs=[pl.BlockSpec(memory_space=pl.ANY),
                  pl.BlockSpec((TOK_TILE * TOK_ROWS, LANE), lambda i: (i, 0)),
                  pl.BlockSpec(memory_space=pl.ANY)],
        out_specs=pl.BlockSpec(memory_space=pl.ANY),
        scratch_shapes=[pltpu.SMEM((TOK_TILE * TOP_K,), jnp.int32),
                        pltpu.SemaphoreType.DMA, pltpu.SemaphoreType.DMA],
        input_output_aliases={2: 0},
        compiler_params=pltpu.CompilerParams(
            dimension_semantics=("arbitrary",), vmem_limit_bytes=VMEM_LIMIT),
        name="moe_dispatch",
    )(dest2, x1t, xb_zero)


def _expert_kernel(be_ref, nu_ref, x_ref, wg_ref, wu_ref, wd_ref, bg_ref, bu_ref, bd_ref, y_ref,
                   wgb, wub, wdb):
    j = pl.program_id(0)
    e = be_ref[j]
    prev = be_ref[jnp.maximum(j - 1, 0)]
    used = j < nu_ref[0]

    @pl.when(used & ((j == 0) | (e != prev)))
    def _():
        wgb[...] = _bf(wg_ref[...])
        wub[...] = _bf(wu_ref[...])
        wdb[...] = _bf(wd_ref[...])

    @pl.when(used)
    def _():
        x = jnp.concatenate([_bf(_load_token_chunk(x_ref, cc, MOE_BLK)) for cc in range(TOK_ROWS)], axis=1)
        gt = _dot(x, wgb[...]) + bg_ref[...]
        up = _dot(x, wub[...]) + bu_ref[...]
        gt = jnp.minimum(gt, SWIGLU_LIMIT)
        up = jnp.clip(up, -SWIGLU_LIMIT, SWIGLU_LIMIT)
        h = (up + 1.0) * (gt * _sigmoid(SWIGLU_ALPHA * gt))
        _store_token_tiles(y_ref, _dot(_bf(h), wdb[...]) + bd_ref[...])

    @pl.when(jnp.logical_not(used))
    def _():
        y_ref[...] = jnp.zeros_like(y_ref)


def _experts(layer, block_e, n_used, xb, wg, wu, wd, bg, bu, bd):
    rows = xb.shape[0]
    blk_rows = MOE_BLK * TOK_ROWS
    nblk = rows // blk_rows
    depth = wg.shape[0]
    d_e = wg.shape[-1]
    wspec_in = pl.BlockSpec((None, None, D_MODEL, d_e), lambda j, be, nu: (layer, be[j], 0, 0))
    wspec_out = pl.BlockSpec((None, None, d_e, D_MODEL), lambda j, be, nu: (layer, be[j], 0, 0))
    bspec_e = pl.BlockSpec((None, None, 1, d_e), lambda j, be, nu: (layer, be[j], 0, 0))
    bspec_d = pl.BlockSpec((None, None, 1, D_MODEL), lambda j, be, nu: (layer, be[j], 0, 0))
    grid_spec = pltpu.PrefetchScalarGridSpec(
        num_scalar_prefetch=2,
        grid=(nblk,),
        in_specs=[pl.BlockSpec((blk_rows, LANE), lambda j, be, nu: (j, 0)),
                  wspec_in, wspec_in, wspec_out, bspec_e, bspec_e, bspec_d],
        out_specs=pl.BlockSpec((blk_rows, LANE), lambda j, be, nu: (j, 0)),
        scratch_shapes=[pltpu.VMEM((D_MODEL, d_e), BF16), pltpu.VMEM((D_MODEL, d_e), BF16),
                        pltpu.VMEM((d_e, D_MODEL), BF16)],
    )
    return pl.pallas_call(
        _expert_kernel,
        out_shape=jax.ShapeDtypeStruct((rows, LANE), F32),
        grid_spec=grid_spec,
        compiler_params=pltpu.CompilerParams(
            dimension_semantics=("arbitrary",), vmem_limit_bytes=VMEM_LIMIT),
        name="moe_experts",
    )(block_e, n_used, xb, wg, wu, wd, bg.reshape(depth, N_EXPERTS, 1, d_e),
      bu.reshape(depth, N_EXPERTS, 1, d_e), bd.reshape(depth, N_EXPERTS, 1, D_MODEL))


def _combine_kernel(dest_hbm, gates_ref, x1_ref, lng_ref, lnb_ref, yb_hbm, y_ref, ybf_ref, idx, buf,
                    sem_idx, sem, *, alpha):
    i = pl.program_id(0)
    n = pl.num_programs(0)
    slot = i % 2
    nxt = 1 - slot

    def idx_copy(tile, s):
        return pltpu.make_async_copy(dest_hbm.at[tile], idx.at[s], sem_idx.at[s])

    def issue_gathers(s):
        def issue(r, carry):
            for kk in range(TOP_K):
                pltpu.make_async_copy(yb_hbm.at[_token_rows(idx[s, r * TOP_K + kk]), :],
                                      buf.at[s, kk, _token_rows(r), :], sem.at[s]).start(priority=kk % 2)
            return carry

        lax.fori_loop(0, TOK_TILE, issue, 0, unroll=8)

    @pl.when(i == 0)
    def _():
        first = idx_copy(0, 0)
        first.start()
        first.wait()
        issue_gathers(0)

        @pl.when(n > 1)
        def _():
            idx_copy(1, 1).start()

    @pl.when(i + 1 < n)
    def _():
        idx_copy(i + 1, nxt).wait()
        issue_gathers(nxt)

    @pl.when(i + 2 < n)
    def _():
        idx_copy(i + 2, slot).start()

    for kk in range(TOP_K):
        pltpu.make_async_copy(yb_hbm.at[pl.ds(0, TOK_TILE * TOK_ROWS), :], buf.at[slot, kk], sem.at[slot]).wait()

    gates = gates_ref[...]
    chunks = []
    for cc in range(TOK_ROWS):
        acc = gates[:, 0:1] * _load_token_chunk(buf.at[slot, 0], cc, TOK_TILE)
        for kk in range(1, TOP_K):
            acc = acc + gates[:, kk:kk + 1] * _load_token_chunk(buf.at[slot, kk], cc, TOK_TILE)
        chunks.append(acc)
    moe = jnp.concatenate(chunks, axis=1)
    y = _layer_norm(alpha * x1_ref[...] + moe, lng_ref[...], lnb_ref[...])
    y_ref[...] = y
    ybf_ref[...] = _bf(y)


def _combine(dest2, route, x1, lng, lnb, yb, alpha):
    nt = x1.shape[0]
    kern = functools.partial(_combine_kernel, alpha=alpha)
    return pl.pallas_call(
        kern,
        out_shape=(jax.ShapeDtypeStruct((nt, D_MODEL), F32),
                   jax.ShapeDtypeStruct((nt, D_MODEL), BF16)),
        grid=(nt // TOK_TILE,),
        in_specs=[pl.BlockSpec(memory_space=pl.ANY),
                  pl.BlockSpec((TOK_TILE, LANE), lambda i: (i, 0)),
                  pl.BlockSpec((TOK_TILE, D_MODEL), lambda i: (i, 0)),
                  pl.BlockSpec((1, D_MODEL), lambda i: (0, 0)),
                  pl.BlockSpec((1, D_MODEL), lambda i: (0, 0)),
                  pl.BlockSpec(memory_space=pl.ANY)],
        out_specs=(pl.BlockSpec((TOK_TILE, D_MODEL), lambda i: (i, 0)),
                   pl.BlockSpec((TOK_TILE, D_MODEL), lambda i: (i, 0))),
        scratch_shapes=[pltpu.SMEM((2, TOK_TILE * TOP_K), jnp.int32),
                        pltpu.VMEM((2, TOP_K, TOK_TILE * TOK_ROWS, LANE), F32),
                        pltpu.SemaphoreType.DMA((2,)), pltpu.SemaphoreType.DMA((2,))],
        compiler_params=pltpu.CompilerParams(
            dimension_semantics=("arbitrary",), vmem_limit_bytes=VMEM_LIMIT),
        name="moe_combine_ln",
    )(dest2, route, x1, lng, lnb, yb)


def _routing_tables(top_i):
    m = top_i.size
    flat_e = top_i.reshape(-1)
    onehot = (flat_e[:, None] == jnp.arange(N_EXPERTS, dtype=jnp.int32)[None, :]).astype(jnp.int32)
    oh3 = onehot.reshape(m // LANE, LANE, N_EXPERTS)
    tri = jnp.tril(jnp.ones((LANE, LANE), F32))
    within = jnp.einsum("ij,tjk->tik", tri, oh3.astype(F32)).astype(jnp.int32)
    tile_tot = within[:, -1, :]
    tile_off = jnp.cumsum(tile_tot, axis=0) - tile_tot
    csum = (within + tile_off[:, None, :]).reshape(m, N_EXPERTS)
    rank = jnp.sum(onehot * csum, axis=1) - 1
    counts = csum[-1]
    padded = (counts + MOE_BLK - 1) // MOE_BLK * MOE_BLK
    pad_ends = jnp.cumsum(padded)
    pad_starts = pad_ends - padded
    dest = jnp.sum(onehot * pad_starts[None, :], axis=1) + rank
    nblk = m // MOE_BLK + N_EXPERTS
    blk_start = jnp.arange(nblk, dtype=jnp.int32) * MOE_BLK
    block_e = jnp.minimum(jnp.sum((blk_start[:, None] >= pad_ends[None, :]).astype(jnp.int32), axis=1),
                          N_EXPERTS - 1)
    n_used = (pad_ends[-1] // MOE_BLK).astype(jnp.int32).reshape(1)
    return dest.astype(jnp.int32), block_e.astype(jnp.int32), n_used, nblk


def kernel(x_prompt, x_sample, state_gdn, state_gdn_conv, state_ssm, state_ssm_conv, w_in, gdn_conv_w,
           gdn_a_log, gdn_dt_bias, gdn_norm_w, ssm_conv_w, ssm_conv_b, ssm_a_log, ssm_dt_bias, ssm_d,
           ssm_norm_w, w_br_gdn, w_br_ssm, w_out, ln1_g, ln1_b, router_w, router_b, exp_w_gate,
           exp_b_gate, exp_w_up, exp_b_up, exp_w_down, exp_b_down, ln2_g, ln2_b):
    batch, seq, _ = x_prompt.shape
    dec = x_sample.shape[0]
    depth = w_in.shape[0]
    n_p = batch * seq
    nt = n_p + dec
    alpha = (2.0 * depth) ** 0.25
    gw = SSM_DINNER // SSM_GROUPS
    tm = _pick(nt, (384, 256, 128, 64, 32, 16))
    assert seq % CHUNK == 0 and dec % CHUNK == 0 and nt % TOK_TILE == 0 and (nt * TOP_K) % MOE_BLK == 0

    x = jnp.concatenate([x_prompt.reshape(n_p, D_MODEL), x_sample.reshape(dec, D_MODEL)], axis=0)
    x_bf = _bf(x)

    o_zg = GDN_CONV_CH
    o_b = o_zg + GDN_VDIM
    o_a = o_b + GDN_HEADS
    o_zs = o_a + GDN_HEADS
    o_x = o_zs + SSM_DINNER
    o_dt = o_x + SSM_CONV_CH
    o_ga = o_dt + SSM_HEADS

    rows = jnp.arange(SMALL_W, dtype=jnp.int32)[:, None]
    lanes = jnp.arange(SSM_DINNER, dtype=jnp.int32)[None, :]
    expand_all = (rows == COL_DT + lanes // SSM_HEADDIM).astype(BF16)

    outs = {k: [] for k in ("gdn_p", "gconv_p", "gconv_s", "ssm_p", "sconv_p", "sconv_s")}
    ssm_all = state_ssm.reshape(depth, dec, SSM_DINNER, SSM_DSTATE)
    gdn_s = ssm_s = None
    xb = None
    for l in range(depth):
        w = w_in[l]
        w_big = _bf(jnp.concatenate([w[:, :o_b], w[:, o_zs:o_dt], w[:, o_ga:]], axis=1))
        w_small = _bf(jnp.concatenate(
            [w[:, o_b:o_zs], w[:, o_dt:o_ga],
             jnp.zeros((D_MODEL, SMALL_W - 2 * GDN_HEADS - SSM_HEADS), F32)], axis=1))
        zpad = jnp.zeros((SMALL_W - COL_DT - SSM_HEADS,), F32)
        pv = jnp.zeros((SUBLANE, SMALL_W), F32)
        pv = pv.at[0].set(jnp.concatenate([jnp.zeros((COL_A,), F32), gdn_dt_bias[l], ssm_dt_bias[l], zpad]))
        pv = pv.at[1].set(jnp.concatenate([jnp.zeros((COL_A,), F32), gdn_a_log[l], ssm_a_log[l], zpad]))
        d_exp = jnp.repeat(ssm_d[l], SSM_HEADDIM).reshape(1, SSM_DINNER)
        gnw = gdn_norm_w[l].reshape(1, LANE)
        snw = ssm_norm_w[l].reshape(1, SSM_DINNER)
        scb = ssm_conv_b[l].reshape(1, SSM_CONV_CH)

        proj = _matmul(x_bf, w_big, tm, W_BIG // 2)
        small = _matmul(x_bf, w_small, tm, SMALL_W)

        proj_s = lax.slice(proj, (n_p, 0), (nt, W_BIG))
        small_s = lax.slice(small, (n_p, 0), (nt, SMALL_W))
        og_s, yz_s, gdn_s, ssm_s = _sample_mix(l, proj_s, small_s, state_gdn_conv[l], state_ssm_conv[l],
                                               state_gdn, ssm_all, gdn_s, ssm_s, gdn_conv_w[l],
                                               ssm_conv_w[l], scb, pv, gnw, expand_all, d_exp, snw)
        og, s_p = _gdn_prompt(proj, small, gdn_conv_w[l], pv, gnw, og_s, batch, seq)
        yz, h_p = _ssd_prompt(proj, small, ssm_conv_w[l], scb, pv, expand_all, d_exp, snw, yz_s, batch, seq)

        rw = jnp.concatenate([router_w[l], jnp.zeros((D_MODEL, LANE - N_EXPERTS), F32)], axis=1)
        rb = jnp.concatenate([router_b[l], jnp.full((LANE - N_EXPERTS,), NEG_BIG, F32)]).reshape(1, LANE)
        x1, x1t, route = _merge(og, yz, proj, x, _bf(w_br_gdn[l]), _bf(w_br_ssm[l]), _bf(w_out[l]),
                                ln1_g[l].reshape(1, D_MODEL), ln1_b[l].reshape(1, D_MODEL), rw, rb, alpha, tm)

        top_i = route[:, TOP_K:2 * TOP_K].astype(jnp.int32)
        dest, block_e, n_used, nblk = _routing_tables(top_i)
        dest2 = dest.reshape(nt // TOK_TILE, TOK_TILE * TOP_K)
        xb = _dispatch(dest2, x1t, jnp.zeros((nblk * MOE_BLK * TOK_ROWS, LANE), F32) if xb is None else xb)
        yb = _experts(l, block_e, n_used, xb, exp_w_gate, exp_w_up, exp_w_down,
                      exp_b_gate, exp_b_up, exp_b_down)
        x, x_bf = _combine(dest2, route, x1, ln2_g[l].reshape(1, D_MODEL), ln2_b[l].reshape(1, D_MODEL),
                           yb, alpha)

        outs["gdn_p"].append(s_p)
        outs["ssm_p"].append(h_p.reshape(batch, SSM_HEADS, SSM_HEADDIM, SSM_DSTATE))
        tails_g = [lax.slice(proj, (b * seq + seq - (CONV_W - 1), 0), (b * seq + seq, GDN_CONV_CH))
                   for b in range(batch)]
        tails_s = [lax.slice(proj, (b * seq + seq - (CONV_W - 1), o_x - 2 * GDN_HEADS),
                             (b * seq + seq, o_x - 2 * GDN_HEADS + SSM_CONV_CH)) for b in range(batch)]
        outs["gconv_p"].append(jnp.stack(tails_g))
        outs["sconv_p"].append(jnp.stack(tails_s))
        outs["gconv_s"].append(jnp.concatenate(
            [state_gdn_conv[l][:, 1:], proj_s[:, None, :GDN_CONV_CH]], axis=1))
        xbc_off = o_x - 2 * GDN_HEADS
        outs["sconv_s"].append(jnp.concatenate(
            [state_ssm_conv[l][:, 1:], proj_s[:, None, xbc_off:xbc_off + SSM_CONV_CH]], axis=1))

    yp = x[:n_p].reshape(batch, seq, D_MODEL)
    ys = x[n_p:].reshape(dec, 1, D_MODEL)
    return (yp, ys, jnp.stack(outs["gdn_p"]), gdn_s, jnp.stack(outs["gconv_p"]),
            jnp.stack(outs["gconv_s"]), jnp.stack(outs["ssm_p"]),
            ssm_s.reshape(depth, dec, SSM_HEADS, SSM_HEADDIM, SSM_DSTATE),
            jnp.stack(outs["sconv_p"]), jnp.stack(outs["sconv_s"]))
```
